```python
import math
import jax, jax.numpy as jnp
from jax import lax
import numpy as np

D_MODEL = 1024
BATCH = 4
SEQ = 8192
DEPTH = 2

N_META = 16
BLOCK = 128
PAD = BLOCK - N_META
ATT_HEADS = 8
ATT_HEAD_DIM = 64
ATT_WIDTH = ATT_HEADS * ATT_HEAD_DIM
SSD_HEADS = 16
SSD_HEAD_DIM = 64
SSD_INNER = SSD_HEADS * SSD_HEAD_DIM
SSD_GROUPS = 2
SSD_HPG = SSD_HEADS // SSD_GROUPS
SSD_STATE = 128
SSD_CONV = 4
SSD_CONV_DIM = SSD_INNER + 2 * SSD_GROUPS * SSD_STATE
SC_WIDTH = 512
SC_CONV = 3
N_BRANCH = 3
FF_DENSE = 2816
N_EXPERTS = 8
TOP_K = 2
FF_EXPERT = 3584
N_DENSE = (DEPTH + 1) // 2
N_MOE = DEPTH // 2
ALPHA = (2 * DEPTH) ** 0.25
BETA = (8 * DEPTH) ** -0.25
LN_EPS = 1e-5
RMS_EPS = 1e-5
NEG_INF = -1e30
IN_SIZES = (ATT_WIDTH, ATT_WIDTH, ATT_WIDTH, ATT_HEADS,
            SSD_INNER, SSD_CONV_DIM, SSD_HEADS,
            3 * SC_WIDTH, N_BRANCH * D_MODEL)
N_IN = sum(IN_SIZES)

kernel_name = "hybrid_fox_ssd_shortconv_moe_deepnorm"


def layer_norm(x, g, b):
    xf = x.astype(jnp.float32)
    mu = jnp.mean(xf, axis=-1, keepdims=True)
    var = jnp.mean(jnp.square(xf - mu), axis=-1, keepdims=True)
    return ((xf - mu) * lax.rsqrt(var + LN_EPS) * g + b).astype(x.dtype)


def front_pad(t):
    return jnp.pad(t, [(0, 0), (PAD, 0)] + [(0, 0)] * (t.ndim - 2))


def causal_depthwise_conv(u, w):
    k_w, c = w.shape
    return lax.conv_general_dilated(
        u, w[:, None, :].astype(u.dtype), window_strides=(1,),
        padding=[(k_w - 1, 0)], dimension_numbers=("NWC", "WIO", "NWC"),
        feature_group_count=c)


def forgetting_attention(q, k, v, log_f):
    b, lp, h, hd = q.shape
    n_blocks = lp // BLOCK
    c_t = jnp.moveaxis(jnp.cumsum(log_f, axis=1), -1, 1)
    kpos = jnp.arange(lp)
    key_valid = kpos >= PAD
    scale = hd ** -0.5

    def one_block(i):
        start = i * BLOCK
        qb = lax.dynamic_slice_in_dim(q, start, BLOCK, axis=1)
        cq = lax.dynamic_slice_in_dim(c_t, start, BLOCK, axis=2)
        s = jnp.einsum("bqhd,bkhd->bhqk", qb, k,
                       preferred_element_type=jnp.float32) * scale
        s = s + (cq[..., :, None] - c_t[..., None, :])
        qpos = start + jnp.arange(BLOCK)
        mask = (kpos[None, :] <= qpos[:, None]) & key_valid[None, :]
        p = jax.nn.softmax(jnp.where(mask[None, None], s, NEG_INF), axis=-1)
        return jnp.einsum("bhqk,bkhd->bqhd", p.astype(v.dtype), v)

    out = lax.map(one_block, jnp.arange(n_blocks))
    return jnp.moveaxis(out, 0, 1).reshape(b, lp, h, hd)


def ssd_chunked(x, dt, a, bm, cm):
    b, lp, g, r, p = x.shape
    nc = lp // BLOCK
    x = x.reshape(b, nc, BLOCK, g, r, p)
    dt = dt.reshape(b, nc, BLOCK, g, r)
    bm = bm.reshape(b, nc, BLOCK, g, -1)
    cm = cm.reshape(b, nc, BLOCK, g, -1)
    a_cum = jnp.cumsum(dt * a, axis=2)
    xdt = x * dt[..., None]
    seg = a_cum[:, :, :, None] - a_cum[:, :, None, :]
    causal = jnp.tril(jnp.ones((BLOCK, BLOCK), dtype=bool))[:, :, None, None]
    decay = jnp.exp(jnp.where(causal, seg, -jnp.inf))
    cb = jnp.einsum("bclgn,bcsgn->bclsg", cm, bm)
    y_diag = jnp.einsum("bclsg,bclsgr,bcsgrp->bclgrp", cb, decay, xdt)
    decay_states = jnp.exp(a_cum[:, :, -1:] - a_cum)
    states = jnp.einsum("bclgn,bclgr,bclgrp->bcgrpn", bm, decay_states, xdt)
    chunk_decay = jnp.exp(a_cum[:, :, -1])

    def step(h, inp):
        s_c, d_c = inp
        return h * d_c[..., None, None] + s_c, h

    h0 = jnp.zeros(states.shape[:1] + states.shape[2:], states.dtype)
    _, prev = lax.scan(step, h0, (jnp.moveaxis(states, 1, 0), jnp.moveaxis(chunk_decay, 1, 0)))
    prev = jnp.moveaxis(prev, 0, 1)
    y_off = jnp.einsum("bclgn,bcgrpn,bclgr->bclgrp", cm, prev, jnp.exp(a_cum))
    return (y_diag + y_off).reshape(b, lp, g, r, p)


def gated_group_rmsnorm(y, z, w):
    b, l, d = y.shape
    u = (y * jax.nn.silu(z)).astype(jnp.float32).reshape(b, l, SSD_GROUPS, d // SSD_GROUPS)
    u = u * lax.rsqrt(jnp.mean(jnp.square(u), axis=-1, keepdims=True) + RMS_EPS)
    return u.reshape(b, l, d) * w


def hybrid_mixer(x, w_in, b_forget, ssd_conv_w, ssd_conv_b, ssd_dt_bias, ssd_a_log,
                 ssd_d, ssd_norm_w, sc_conv_w, w_proj_attn, w_proj_ssd, w_proj_conv, w_out):
    b, l, _ = x.shape
    proj = x @ w_in
    split_idx = np.cumsum(IN_SIZES)[:-1].tolist()
    q, k, v, f_logit, z, xbc, dt_raw, sc_in, gate_logit = jnp.split(proj, split_idx, axis=-1)

    def heads(t):
        return front_pad(t.reshape(b, l, ATT_HEADS, ATT_HEAD_DIM))
    log_f = jax.nn.log_sigmoid((f_logit + b_forget).astype(jnp.float32))
    y_att = forgetting_attention(heads(q), heads(k), heads(v), front_pad(log_f))
    y_att = y_att[:, PAD:].reshape(b, l, ATT_WIDTH)

    xbc = jax.nn.silu(causal_depthwise_conv(xbc, ssd_conv_w) + ssd_conv_b)
    xs, bs, cs = jnp.split(xbc, [SSD_INNER, SSD_INNER + SSD_GROUPS * SSD_STATE], axis=-1)
    dt = jax.nn.softplus((dt_raw + ssd_dt_bias).astype(jnp.float32))
    a = -jnp.exp(ssd_a_log.astype(jnp.float32)).reshape(SSD_GROUPS, SSD_HPG)
    xs_h = xs.reshape(b, l, SSD_GROUPS, SSD_HPG, SSD_HEAD_DIM)
    y = ssd_chunked(front_pad(xs_h),
                    front_pad(dt.reshape(b, l, SSD_GROUPS, SSD_HPG)), a,
                    front_pad(bs.reshape(b, l, SSD_GROUPS, SSD_STATE)),
                    front_pad(cs.reshape(b, l, SSD_GROUPS, SSD_STATE)))[:, PAD:]
    y = y + ssd_d.reshape(SSD_GROUPS, SSD_HPG)[:, :, None] * xs_h
    y_ssd = gated_group_rmsnorm(y.reshape(b, l, SSD_INNER), z, ssd_norm_w)

    sc_b, sc_c, sc_h = jnp.split(sc_in, 3, axis=-1)
    y_conv = sc_b * causal_depthwise_conv(sc_c * sc_h, sc_conv_w)

    g_att, g_ssd, g_conv = jnp.split(jax.nn.sigmoid(gate_logit), N_BRANCH, axis=-1)
    merged = (g_att * (y_att @ w_proj_attn)
              + g_ssd * (y_ssd @ w_proj_ssd)
              + g_conv * (y_conv @ w_proj_conv))
    return merged @ w_out


def swiglu(t, w_gu, w_down):
    gate, up = jnp.split(t @ w_gu, 2, axis=-1)
    return (jax.nn.silu(gate) * up) @ w_down


def moe_swiglu(x, router_w, router_b, w_gu, w_down):
    b, l, d = x.shape
    t = x.reshape(b * l, d)
    logits = (t @ router_w + router_b).astype(jnp.float32)
    top_val, top_idx = lax.top_k(logits, TOP_K)
    top_w = jax.nn.softmax(top_val, axis=-1)
    combine = jnp.sum(jax.nn.one_hot(top_idx, N_EXPERTS, dtype=jnp.float32) * top_w[..., None], axis=1)
    out = jnp.zeros((b * l, d), jnp.float32)
    for e in range(N_EXPERTS):
        out = out + combine[:, e:e + 1] * swiglu(t, w_gu[e], w_down[e])
    return out.reshape(b, l, d).astype(x.dtype)


def setup_inputs(seed: int = 0) -> dict:
    key = jax.random.key(seed)
    ks = jax.random.split(key, 32)
    f32 = jnp.float32

    def nrm(k, shape, scale):
        return jax.random.normal(k, shape, f32) * scale

    dt0 = jnp.exp(jax.random.uniform(ks[8], (DEPTH, SSD_HEADS), f32,
                                     math.log(0.001), math.log(0.1)))
    return {
        "x": nrm(ks[0], (BATCH, SEQ, D_MODEL), 1.0),
        "meta_tokens": nrm(ks[1], (N_META, D_MODEL), 1.0),
        "ln_in_g": 1.0 + nrm(ks[2], (D_MODEL,), 0.02),
        "ln_in_b": nrm(ks[3], (D_MODEL,), 0.02),
        "w_in": nrm(ks[4], (DEPTH, D_MODEL, N_IN), D_MODEL ** -0.5),
        "b_forget": 2.0 + nrm(ks[5], (DEPTH, ATT_HEADS), 0.5),
        "ssd_conv_w": nrm(ks[6], (DEPTH, SSD_CONV, SSD_CONV_DIM), SSD_CONV ** -0.5),
        "ssd_conv_b": nrm(ks[7], (DEPTH, SSD_CONV_DIM), 0.02),
        "ssd_dt_bias": dt0 + jnp.log(-jnp.expm1(-dt0)),
        "ssd_a_log": jnp.log(jax.random.uniform(ks[9], (DEPTH, SSD_HEADS), f32, 1.0, 16.0)),
        "ssd_d": 1.0 + nrm(ks[10], (DEPTH, SSD_HEADS), 0.02),
        "ssd_norm_w": 1.0 + nrm(ks[11], (DEPTH, SSD_INNER), 0.02),
        "sc_conv_w": nrm(ks[12], (DEPTH, SC_CONV, SC_WIDTH), SC_CONV ** -0.5),
        "w_proj_attn": nrm(ks[13], (DEPTH, ATT_WIDTH, D_MODEL), ATT_WIDTH ** -0.5),
        "w_proj_ssd": nrm(ks[14], (DEPTH, SSD_INNER, D_MODEL), SSD_INNER ** -0.5),
        "w_proj_conv": nrm(ks[15], (DEPTH, SC_WIDTH, D_MODEL), SC_WIDTH ** -0.5),
        "w_out": nrm(ks[16], (DEPTH, D_MODEL, D_MODEL), BETA * D_MODEL ** -0.5),
        "ln_mix_g": 1.0 + nrm(ks[17], (DEPTH, D_MODEL), 0.02),
        "ln_mix_b": nrm(ks[18], (DEPTH, D_MODEL), 0.02),
        "dense_w_gu": nrm(ks[19], (N_DENSE, D_MODEL, 2 * FF_DENSE), D_MODEL ** -0.5),
        "dense_w_down": nrm(ks[20], (N_DENSE, FF_DENSE, D_MODEL), BETA * FF_DENSE ** -0.5),
        "router_w": nrm(ks[21], (N_MOE, D_MODEL, N_EXPERTS), D_MODEL ** -0.5),
        "router_b": nrm(ks[22], (N_MOE, N_EXPERTS), 0.01),
        "moe_w_gu": nrm(ks[23], (N_MOE, N_EXPERTS, D_MODEL, 2 * FF_EXPERT), D_MODEL ** -0.5),
        "moe_w_down": nrm(ks[24], (N_MOE, N_EXPERTS, FF_EXPERT, D_MODEL), BETA * FF_EXPERT ** -0.5),
        "ln_ffn_g": 1.0 + nrm(ks[25], (DEPTH, D_MODEL), 0.02),
        "ln_ffn_b": nrm(ks[26], (DEPTH, D_MODEL), 0.02),
    }


def reference(x, meta_tokens, ln_in_g, ln_in_b, w_in, b_forget, ssd_conv_w, ssd_conv_b,
              ssd_dt_bias, ssd_a_log, ssd_d, ssd_norm_w, sc_conv_w, w_proj_attn, w_proj_ssd,
              w_proj_conv, w_out, ln_mix_g, ln_mix_b, dense_w_gu, dense_w_down, router_w,
              router_b, moe_w_gu, moe_w_down, ln_ffn_g, ln_ffn_b):
    b = x.shape[0]
    meta = jnp.broadcast_to(meta_tokens.astype(x.dtype)[None], (b, N_META, D_MODEL))
    h = layer_norm(jnp.concatenate([meta, x], axis=1), ln_in_g, ln_in_b)
    for layer in range(DEPTH):
        mix = hybrid_mixer(h, w_in[layer], b_forget[layer], ssd_conv_w[layer], ssd_conv_b[layer],
                           ssd_dt_bias[layer], ssd_a_log[layer], ssd_d[layer], ssd_norm_w[layer],
                           sc_conv_w[layer], w_proj_attn[layer], w_proj_ssd[layer],
                           w_proj_conv[layer], w_out[layer])
        h = layer_norm(ALPHA * h + mix, ln_mix_g[layer], ln_mix_b[layer])
        j = layer // 2
        if layer % 2 == 0:
            ff = swiglu(h, dense_w_gu[j], dense_w_down[j])
        else:
            ff = moe_swiglu(h, router_w[j], router_b[j], moe_w_gu[j], moe_w_down[j])
        h = layer_norm(ALPHA * h + ff, ln_ffn_g[layer], ln_ffn_b[layer])
    return h[:, N_META:]
```

```python
import functools

import jax
import jax.numpy as jnp
from jax import lax
from jax.experimental import pallas as pl
from jax.experimental.pallas import tpu as pltpu

F32, BF16, I32 = jnp.float32, jnp.bfloat16, jnp.int32

N_META = 16
BLOCK = 128
PAD = BLOCK - N_META
ATT_HEADS = 8
ATT_HEAD_DIM = 64
SSD_HEADS = 16
SSD_HEAD_DIM = 64
SSD_GROUPS = 2
SSD_HPG = SSD_HEADS // SSD_GROUPS
SSD_STATE = 128
SSD_CONV = 4
SC_CONV = 3
N_EXPERTS = 8
LN_EPS = 1e-5
RMS_EPS = 1e-5
NEG_INF = -1e30

LANE = 128
SUBLANE = 8
VMEM_LIMIT = 56 * 1024 * 1024


def _cparams(*sem):
    return pltpu.CompilerParams(dimension_semantics=tuple(sem), vmem_limit_bytes=VMEM_LIMIT)


def _divisor_tile(n, target, quantum):
    best = None
    t = quantum
    while t <= min(n, target):
        if n % t == 0:
            best = t
        t += quantum
    assert best is not None, (n, target, quantum)
    return best


def _dot(a, b):
    return jnp.dot(a, b, preferred_element_type=F32)


def _dot_nt(a, b):
    return lax.dot_general(a, b, (((1,), (1,)), ((), ())), preferred_element_type=F32)


def _split2(x):
    hi = x.astype(BF16)
    lo = (x - hi.astype(F32)).astype(BF16)
    return hi, lo


def _split3(x):
    hi = x.astype(BF16)
    r = x - hi.astype(F32)
    mid = r.astype(BF16)
    lo = (r - mid.astype(F32)).astype(BF16)
    return hi, mid, lo


def _dot_x3(a_f32, w_hi, w_lo):
    a_hi, a_lo = _split2(a_f32)
    return _dot(a_hi, w_hi) + _dot(a_hi, w_lo) + _dot(a_lo, w_hi)


def _ln(v, g, b):
    mu = jnp.mean(v, axis=-1, keepdims=True)
    c = v - mu
    var = jnp.mean(c * c, axis=-1, keepdims=True)
    return c * lax.rsqrt(var + LN_EPS) * g + b


def _silu(v):
    return v * jax.nn.sigmoid(v)


def _softplus(v):
    return jnp.maximum(v, 0.0) + jnp.log(1.0 + jnp.exp(-jnp.abs(v)))


def _ln_kernel(x_ref, g_ref, b_ref, o_ref):
    o_ref[...] = _ln(x_ref[...].astype(F32), g_ref[...], b_ref[...])


def _ln_rows(x2d, g, b):
    n, d = x2d.shape
    tm = _divisor_tile(n, 1024, SUBLANE)
    return pl.pallas_call(
        _ln_kernel,
        grid=(n // tm,),
        in_specs=[pl.BlockSpec((tm, d), lambda i: (i, 0)),
                  pl.BlockSpec((1, d), lambda i: (0, 0)),
                  pl.BlockSpec((1, d), lambda i: (0, 0))],
        out_specs=pl.BlockSpec((tm, d), lambda i: (i, 0)),
        out_shape=jax.ShapeDtypeStruct((n, d), F32),
        compiler_params=_cparams("parallel"),
    )(x2d, g.reshape(1, d), b.reshape(1, d))


def _in_proj_kernel(x_ref, w_ref, wsh_ref, wsl_ref, o_ref, os_ref, xb_ref):
    @pl.when(pl.program_id(1) == 0)
    def _():
        x = x_ref[...]
        xb_ref[...] = x.astype(BF16)
        os_ref[...] = _dot_x3(x, wsh_ref[...], wsl_ref[...])

    o_ref[...] = _dot(xb_ref[...], w_ref[...]).astype(BF16)


def _in_proj(h, w_main, ws_hi, ws_lo):
    tp, d = h.shape
    n = w_main.shape[1]
    ns = ws_hi.shape[1]
    tm = _divisor_tile(tp, 1280, LANE)
    tn = _divisor_tile(n, 1024, LANE)
    return pl.pallas_call(
        _in_proj_kernel,
        grid=(tp // tm, n // tn),
        in_specs=[pl.BlockSpec((tm, d), lambda i, j: (i, 0)),
                  pl.BlockSpec((d, tn), lambda i, j: (0, j)),
                  pl.BlockSpec((d, ns), lambda i, j: (0, 0)),
                  pl.BlockSpec((d, ns), lambda i, j: (0, 0))],
        out_specs=[pl.BlockSpec((tm, tn), lambda i, j: (i, j)),
                   pl.BlockSpec((tm, ns), lambda i, j: (i, 0))],
        out_shape=[jax.ShapeDtypeStruct((tp, n), BF16),
                   jax.ShapeDtypeStruct((tp, ns), F32)],
        scratch_shapes=[pltpu.VMEM((tm, d), BF16)],
        compiler_params=_cparams("parallel", "arbitrary"),
    )(h, w_main, ws_hi, ws_lo)


def _attn_prep_kernel(q_ref, k_ref, f_ref, bf_ref, tri_ref, pq_ref, pk_ref, qo_ref, ko_ref, carry_ref,
                      *, tm, tiles_per_seq):
    t = pl.program_id(0) % tiles_per_seq

    @pl.when(t == 0)
    def _():
        carry_ref[...] = jnp.zeros_like(carry_ref)

    x = f_ref[...] + bf_ref[...]
    log_f = -_softplus(-x)
    row = t * tm + lax.broadcasted_iota(I32, (tm, LANE), 0)
    lane = lax.broadcasted_iota(I32, (tm, LANE), 1)
    log_f = jnp.where((row >= PAD) & (lane < ATT_HEADS), log_f, 0.0)
    tri = tri_ref[...]
    hi, mid, lo = _split3(log_f)
    c = _dot(tri, hi) + _dot(tri, mid) + _dot(tri, lo) + carry_ref[0:1, :]
    carry_ref[0:1, :] = c[tm - 1:tm, :]
    hi, mid, lo = _split3(c)
    packed = jnp.where(lane < 8, hi.astype(F32),
             jnp.where(lane < 16, pltpu.roll(mid.astype(F32), 8, 1),
             jnp.where(lane < 24, pltpu.roll(lo.astype(F32), 16, 1),
             jnp.where(lane == 24, 1.0, 0.0)))).astype(BF16)
    qo_ref[...] = (q_ref[...].astype(F32) + _dot(packed, pq_ref[...])).astype(BF16)
    ko_ref[...] = (k_ref[...].astype(F32) + _dot(packed, pk_ref[...])).astype(BF16)


def _bias_placement():
    hw = 2 * ATT_HEAD_DIM
    rows_q, cols_q, vals_q, rows_k, cols_k, vals_k = [], [], [], [], [], []
    for h in range(ATT_HEADS):
        base = h * hw + ATT_HEAD_DIM
        for piece in range(3):
            rows_q.append(8 * piece + h); cols_q.append(base + piece); vals_q.append(1.0)
            rows_q.append(24); cols_q.append(base + 3 + piece); vals_q.append(1.0)
            rows_k.append(24); cols_k.append(base + piece); vals_k.append(1.0)
            rows_k.append(8 * piece + h); cols_k.append(base + 3 + piece); vals_k.append(-1.0)
    width = ATT_HEADS * hw
    pq = jnp.zeros((LANE, width), F32).at[jnp.array(rows_q), jnp.array(cols_q)].set(jnp.array(vals_q))
    pk = jnp.zeros((LANE, width), F32).at[jnp.array(rows_k), jnp.array(cols_k)].set(jnp.array(vals_k))
    return pq.astype(BF16), pk.astype(BF16)


def _attn_prep(proj, small, b_forget, lp):
    tp = proj.shape[0]
    width = ATT_HEADS * 2 * ATT_HEAD_DIM
    tm = _divisor_tile(lp, 640, LANE)
    tiles_per_seq = lp // tm
    tri = jnp.tril(jnp.ones((tm, tm), F32)).astype(BF16)
    pq, pk = _bias_placement()
    bf = jnp.zeros((1, LANE), F32).at[0, :ATT_HEADS].set(b_forget)
    kern = functools.partial(_attn_prep_kernel, tm=tm, tiles_per_seq=tiles_per_seq)
    return pl.pallas_call(
        kern,
        grid=(tp // tm,),
        in_specs=[pl.BlockSpec((tm, width), lambda i: (i, 0)),
                  pl.BlockSpec((tm, width), lambda i: (i, 1)),
                  pl.BlockSpec((tm, LANE), lambda i: (i, 0)),
                  pl.BlockSpec((1, LANE), lambda i: (0, 0)),
                  pl.BlockSpec((tm, tm), lambda i: (0, 0)),
                  pl.BlockSpec((LANE, width), lambda i: (0, 0)),
                  pl.BlockSpec((LANE, width), lambda i: (0, 0))],
        out_specs=[pl.BlockSpec((tm, width), lambda i: (i, 0)),
                   pl.BlockSpec((tm, width), lambda i: (i, 0))],
        out_shape=[jax.ShapeDtypeStruct((tp, width), BF16),
                   jax.ShapeDtypeStruct((tp, width), BF16)],
        scratch_shapes=[pltpu.VMEM((SUBLANE, LANE), F32)],
        compiler_params=_cparams("arbitrary"),
    )(proj, proj, small, bf, tri, pq, pk)


def _attn_kernel(q_ref, k_ref, v_ref, o_ref, m_ref, l_ref, acc_ref, *, t):
    i = pl.program_id(2)
    m_ref[...] = jnp.full_like(m_ref, NEG_INF)
    l_ref[...] = jnp.zeros_like(l_ref)
    acc_ref[...] = jnp.zeros_like(acc_ref)
    q = q_ref[...]

    def step(j, masked):
        start = pl.multiple_of(j * t, t)
        k = k_ref[pl.ds(start, t), :]
        v = v_ref[pl.ds(start, t), :]
        s = _dot_nt(q, k)
        if masked:
            qpos = i * t + lax.broadcasted_iota(I32, (t, t), 0)
            kpos = j * t + lax.broadcasted_iota(I32, (t, t), 1)
            s = jnp.where((kpos <= qpos) & (kpos >= PAD), s, NEG_INF)
        m_old = m_ref[...]
        m_new = jnp.maximum(m_old, jnp.max(s, axis=1, keepdims=True))
        p = jnp.exp(s - m_new)
        alpha = jnp.exp(m_old - m_new)
        l_ref[...] = alpha * l_ref[...] + jnp.sum(p, axis=1, keepdims=True)
        acc_ref[...] = alpha * acc_ref[...] + _dot(p.astype(BF16), v)
        m_ref[...] = m_new

    step(0, True)

    def body(j, c):
        step(j, False)
        return c

    lax.fori_loop(1, i, body, 0)

    @pl.when(i > 0)
    def _():
        step(i, True)

    o_ref[...] = (acc_ref[...] / l_ref[...]).astype(BF16)


def _attention(qa, ka, proj, v_col_block, batch, lp):
    hw = 2 * ATT_HEAD_DIM
    tp = qa.shape[0]
    t = _divisor_tile(lp, 640, LANE)
    nq = lp // t
    kern = functools.partial(_attn_kernel, t=t)
    return pl.pallas_call(
        kern,
        grid=(batch, ATT_HEADS, nq),
        in_specs=[pl.BlockSpec((t, hw), lambda b, h, i: (b * nq + i, h)),
                  pl.BlockSpec((lp, hw), lambda b, h, i: (b, h)),
                  pl.BlockSpec((lp, hw), lambda b, h, i: (b, v_col_block + h))],
        out_specs=pl.BlockSpec((t, hw), lambda b, h, i: (b * nq + i, h)),
        out_shape=jax.ShapeDtypeStruct((tp, ATT_HEADS * hw), BF16),
        scratch_shapes=[pltpu.VMEM((t, 1), F32), pltpu.VMEM((t, 1), F32), pltpu.VMEM((t, hw), F32)],
        compiler_params=_cparams("parallel", "parallel", "arbitrary"),
    )(qa, ka, proj)


def _ssd_kernel(x_ref, b_ref, c_ref, z_ref, dt_ref, cw_ref, cb_ref, dtb_ref, alog_ref, dexp_ref, nw_ref,
                tri_ref, e_ref, o_ref, ubuf_ref, state_ref, ybuf_ref):
    chunk = pl.program_id(1)
    inner = SSD_HEADS * SSD_HEAD_DIM
    gs = SSD_GROUPS * SSD_STATE
    gw = SSD_HPG * SSD_HEAD_DIM

    @pl.when(chunk == 0)
    def _():
        ubuf_ref[0:SUBLANE, :] = jnp.zeros((SUBLANE, ubuf_ref.shape[1]), F32)
        state_ref[...] = jnp.zeros_like(state_ref)

    ubuf_ref[SUBLANE:SUBLANE + BLOCK, 0:inner] = x_ref[...].astype(F32)
    ubuf_ref[SUBLANE:SUBLANE + BLOCK, inner:inner + gs] = b_ref[...].astype(F32)
    ubuf_ref[SUBLANE:SUBLANE + BLOCK, inner + gs:inner + 2 * gs] = c_ref[...].astype(F32)
    conv = cb_ref[...]
    for k in range(SSD_CONV):
        conv = conv + cw_ref[k:k + 1, :] * ubuf_ref[pl.ds(SUBLANE - (SSD_CONV - 1) + k, BLOCK), :]
    ubuf_ref[0:SUBLANE, :] = ubuf_ref[BLOCK:BLOCK + SUBLANE, :]
    xc = _silu(conv)
    xs = xc[:, 0:inner]
    bm = xc[:, inner:inner + gs]
    cm = xc[:, inner + gs:inner + 2 * gs]

    row = lax.broadcasted_iota(I32, (BLOCK, LANE), 0)
    lane = lax.broadcasted_iota(I32, (BLOCK, LANE), 1)
    dt = _softplus(dt_ref[...] + dtb_ref[...])
    dt = jnp.where((lane < SSD_HEADS) & (chunk * BLOCK + row >= PAD), dt, 0.0)
    a = -jnp.exp(alog_ref[...])
    tri = tri_ref[...]
    hi, mid, lo = _split3(dt * a)
    a_cum = _dot(tri, hi) + _dot(tri, mid) + _dot(tri, lo)
    a_last = a_cum[BLOCK - 1:BLOCK, :]
    ea = jnp.exp(a_cum)
    wgt = jnp.exp(a_last - a_cum) * dt

    def pieces(v, at):
        v_hi = v.astype(BF16).astype(F32)
        v_lo = v - v_hi
        return (pltpu.roll(v_hi, at, 1) if at else v_hi), pltpu.roll(v_lo, at + SSD_HEADS, 1)

    dt_hi, dt_lo = pieces(dt, 0)
    ea_hi, ea_lo = pieces(ea, 2 * SSD_HEADS)
    wg_hi, wg_lo = pieces(wgt, 4 * SSD_HEADS)
    h16 = SSD_HEADS
    packed = jnp.where(lane < h16, dt_hi,
             jnp.where(lane < 2 * h16, dt_lo,
             jnp.where(lane < 3 * h16, ea_hi,
             jnp.where(lane < 4 * h16, ea_lo,
             jnp.where(lane < 5 * h16, wg_hi,
             jnp.where(lane < 6 * h16, wg_lo, 0.0)))))).astype(BF16)
    expanded = _dot(packed, e_ref[...])
    dt_exp = expanded[:, 0:inner]
    ea_exp = expanded[:, inner:2 * inner]
    wg_exp = expanded[:, 2 * inner:3 * inner]
    xdt = (xs * dt_exp).astype(BF16)
    xw = (xs * wg_exp).astype(BF16)

    a_cum_t = a_cum.T
    causal = row >= lane
    for g in range(SSD_GROUPS):
        cg = cm[:, g * SSD_STATE:(g + 1) * SSD_STATE].astype(BF16)
        bg_f32 = bm[:, g * SSD_STATE:(g + 1) * SSD_STATE]
        cbg = _dot_nt(cg, bg_f32.astype(BF16))
        mats = []
        for hh in range(SSD_HPG):
            h = g * SSD_HPG + hh
            diff = a_cum[:, h:h + 1] - a_cum_t[h:h + 1, :]
            decay = jnp.exp(jnp.where(causal, diff, NEG_INF))
            mats.append((cbg * decay).astype(BF16))
        for j in range(SSD_HPG // 2):
            col = g * gw + j * LANE
            xp = xdt[:, col:col + LANE]
            y0 = _dot(mats[2 * j], xp)
            y1 = _dot(mats[2 * j + 1], xp)
            ybuf_ref[:, col:col + LANE] = jnp.where(lane < SSD_HEAD_DIM, y0, y1)
        st = state_ref[g]
        y_off = _dot(cg, st.astype(BF16))
        ybuf_ref[:, g * gw:(g + 1) * gw] += y_off * ea_exp[:, g * gw:(g + 1) * gw]
        chunk_decay = ea_exp[BLOCK - 1:BLOCK, g * gw:(g + 1) * gw]
        state_ref[g] = st * chunk_decay + _dot(bg_f32.T.astype(BF16), xw[:, g * gw:(g + 1) * gw])

    y = ybuf_ref[...] + dexp_ref[...] * xs
    u = y * _silu(z_ref[...].astype(F32))
    for g in range(SSD_GROUPS):
        ug = u[:, g * gw:(g + 1) * gw]
        ug = ug * lax.rsqrt(jnp.mean(ug * ug, axis=-1, keepdims=True) + RMS_EPS)
        o_ref[:, g * gw:(g + 1) * gw] = (ug * nw_ref[:, g * gw:(g + 1) * gw]).astype(BF16)


def _ssd_expand_matrix():
    inner = SSD_HEADS * SSD_HEAD_DIM
    rows = jnp.arange(LANE)
    cols = jnp.arange(3 * inner)
    seg = cols // inner
    head = (cols % inner) // SSD_HEAD_DIM
    hit = (rows[:, None] // (2 * SSD_HEADS) == seg[None, :]) & (rows[:, None] % SSD_HEADS == head[None, :]) \
          & (rows[:, None] < 6 * SSD_HEADS)
    return hit.astype(BF16)


def _ssd(proj, small, cols, conv_w, conv_b, dt_bias, a_log, d, norm_w, batch, lp):
    tp = proj.shape[0]
    inner = SSD_HEADS * SSD_HEAD_DIM
    gs = SSD_GROUPS * SSD_STATE
    nc = lp // BLOCK
    cdim = inner + 2 * gs
    cw = jnp.zeros((SUBLANE, cdim), F32).at[:SSD_CONV].set(conv_w)
    pad_lane = lambda v: jnp.zeros((1, LANE), F32).at[0, :v.shape[0]].set(v)
    dexp = jnp.repeat(d, SSD_HEAD_DIM).reshape(1, inner)
    tri = jnp.tril(jnp.ones((BLOCK, BLOCK), F32)).astype(BF16)
    rowblk = lambda b, c: b * nc + c
    const = lambda shape: pl.BlockSpec(shape, lambda b, c: (0,) * len(shape))
    return pl.pallas_call(
        _ssd_kernel,
        grid=(batch, nc),
        in_specs=[pl.BlockSpec((BLOCK, inner), lambda b, c: (rowblk(b, c), cols["x"] // inner)),
                  pl.BlockSpec((BLOCK, gs), lambda b, c: (rowblk(b, c), cols["B"] // gs)),
                  pl.BlockSpec((BLOCK, gs), lambda b, c: (rowblk(b, c), cols["C"] // gs)),
                  pl.BlockSpec((BLOCK, inner), lambda b, c: (rowblk(b, c), cols["z"] // inner)),
                  pl.BlockSpec((BLOCK, LANE), lambda b, c: (rowblk(b, c), 1)),
                  const((SUBLANE, cdim)), const((1, cdim)), const((1, LANE)), const((1, LANE)),
                  const((1, inner)), const((1, inner)), const((BLOCK, BLOCK)), const((LANE, 3 * inner))],
        out_specs=pl.BlockSpec((BLOCK, inner), lambda b, c: (rowblk(b, c), 0)),
        out_shape=jax.ShapeDtypeStruct((tp, inner), BF16),
        scratch_shapes=[pltpu.VMEM((BLOCK + SUBLANE, cdim), F32),
                        pltpu.VMEM((SSD_GROUPS, SSD_STATE, SSD_HPG * SSD_HEAD_DIM), F32),
                        pltpu.VMEM((BLOCK, inner), F32)],
        compiler_params=_cparams("parallel", "arbitrary"),
    )(proj, proj, proj, proj, small, cw, conv_b.reshape(1, cdim), pad_lane(dt_bias), pad_lane(a_log),
      dexp, norm_w.reshape(1, inner), tri, _ssd_expand_matrix())


def _merge_kernel(ya_ref, ys_ref, sb_ref, sc_ref, sh_ref, sch_ref, shh_ref, ga_ref, gs_ref, gc_ref, h_ref,
                  wpa_ref, wps_ref, wpc_ref, wo_ref, scw_ref, g_ref, b_ref, *rest,
                  tm, tiles_per_seq, alpha, with_router):
    if with_router:
        rwh_ref, rwl_ref, rb_ref, o_ref, ri_ref, rw_ref, vbuf_ref = rest
    else:
        o_ref, vbuf_ref = rest
    vbuf_ref[0:SUBLANE, :] = sch_ref[...].astype(F32) * shh_ref[...].astype(F32)
    vbuf_ref[SUBLANE:SUBLANE + tm, :] = sc_ref[...].astype(F32) * sh_ref[...].astype(F32)
    conv = jnp.zeros((tm, vbuf_ref.shape[1]), F32)
    for k in range(SC_CONV):
        conv = conv + scw_ref[k:k + 1, :] * vbuf_ref[pl.ds(SUBLANE - (SC_CONV - 1) + k, tm), :]
    y_conv = (sb_ref[...].astype(F32) * conv).astype(BF16)

    merged = jax.nn.sigmoid(ga_ref[...].astype(F32)) * _dot(ya_ref[...], wpa_ref[...])
    merged += jax.nn.sigmoid(gs_ref[...].astype(F32)) * _dot(ys_ref[...], wps_ref[...])
    merged += jax.nn.sigmoid(gc_ref[...].astype(F32)) * _dot(y_conv, wpc_ref[...])
    mix = _dot(merged.astype(BF16), wo_ref[...])
    hn = _ln(alpha * h_ref[...] + mix, g_ref[...], b_ref[...])
    row = (pl.program_id(0) % tiles_per_seq) * tm + lax.broadcasted_iota(I32, (tm, 1), 0)
    hn = jnp.where(row >= PAD, hn, 0.0)
    o_ref[...] = hn

    if with_router:
        lane = lax.broadcasted_iota(I32, (tm, LANE), 1)
        logits = _dot_x3(hn, rwh_ref[...], rwl_ref[...]) + rb_ref[...]
        lg = jnp.where(lane < N_EXPERTS, logits, -jnp.inf)
        lane_f = lane.astype(F32)
        v1 = jnp.max(lg, axis=1, keepdims=True)
        i1 = jnp.min(jnp.where(lg == v1, lane_f, float(LANE)), axis=1, keepdims=True).astype(I32)
        lg2 = jnp.where(lane == i1, -jnp.inf, lg)
        v2 = jnp.max(lg2, axis=1, keepdims=True)
        i2 = jnp.min(jnp.where(lg2 == v2, lane_f, float(LANE)), axis=1, keepdims=True).astype(I32)
        e = jnp.exp(v2 - v1)
        w1 = 1.0 / (1.0 + e)
        w2 = e / (1.0 + e)
        ri_ref[...] = jnp.where(lane == 0, i1, jnp.where(lane == 1, i2, 0))
        rw_ref[...] = jnp.where(lane == 0, w1, jnp.where(lane == 1, w2, 0.0))


def _merge(y_att, y_ssd, proj, cols, h, wpa, wps, wpc, wo, sc_w, g, b, lp, alpha, router=None):
    tp, d = h.shape
    scw = proj_sc = wpc.shape[0]
    tm = _divisor_tile(lp, 640, LANE)
    nt = tp // tm
    halo = lambda i: jnp.maximum(i * (tm // SUBLANE) - 1, 0)
    const = lambda shape: pl.BlockSpec(shape, lambda i: (0,) * len(shape))
    in_specs = [pl.BlockSpec((tm, y_att.shape[1]), lambda i: (i, 0)),
                pl.BlockSpec((tm, y_ssd.shape[1]), lambda i: (i, 0)),
                pl.BlockSpec((tm, scw), lambda i: (i, cols["sc_b"] // scw)),
                pl.BlockSpec((tm, scw), lambda i: (i, cols["sc_c"] // scw)),
                pl.BlockSpec((tm, scw), lambda i: (i, cols["sc_h"] // scw)),
                pl.BlockSpec((SUBLANE, scw), lambda i: (halo(i), cols["sc_c"] // scw)),
                pl.BlockSpec((SUBLANE, scw), lambda i: (halo(i), cols["sc_h"] // scw)),
                pl.BlockSpec((tm, d), lambda i: (i, cols["g_att"] // d)),
                pl.BlockSpec((tm, d), lambda i: (i, cols["g_ssd"] // d)),
                pl.BlockSpec((tm, d), lambda i: (i, cols["g_conv"] // d)),
                pl.BlockSpec((tm, d), lambda i: (i, 0)),
                const(wpa.shape), const(wps.shape), const(wpc.shape), const(wo.shape),
                const((SUBLANE, scw)), const((1, d)), const((1, d))]
    args = [y_att, y_ssd, proj, proj, proj, proj, proj, proj, proj, proj, h, wpa, wps, wpc, wo,
            jnp.zeros((SUBLANE, scw), F32).at[:SC_CONV].set(sc_w), g.reshape(1, d), b.reshape(1, d)]
    out_specs = [pl.BlockSpec((tm, d), lambda i: (i, 0))]
    out_shape = [jax.ShapeDtypeStruct((tp, d), F32)]
    if router is not None:
        rw_hi, rw_lo, rb = router
        in_specs += [const(rw_hi.shape), const(rw_lo.shape), const((1, LANE))]
        args += [rw_hi, rw_lo, rb]
        out_specs += [pl.BlockSpec((tm, LANE), lambda i: (i, 0)), pl.BlockSpec((tm, LANE), lambda i: (i, 0))]
        out_shape += [jax.ShapeDtypeStruct((tp, LANE), I32), jax.ShapeDtypeStruct((tp, LANE), F32)]
    kern = functools.partial(_merge_kernel, tm=tm, tiles_per_seq=lp // tm, alpha=alpha,
                             with_router=router is not None)
    return pl.pallas_call(
        kern, grid=(nt,), in_specs=in_specs, out_specs=out_specs, out_shape=out_shape,
        scratch_shapes=[pltpu.VMEM((tm + SUBLANE, proj_sc), F32)],
        compiler_params=_cparams("parallel"),
    )(*args)


def _ffn_kernel(h_ref, wg_ref, wu_ref, wd_ref, g_ref, b_ref, o_ref, xb_ref, acc_ref,
                *, tm, tiles_per_seq, alpha):
    f = pl.program_id(1)

    @pl.when(f == 0)
    def _():
        xb_ref[...] = h_ref[...].astype(BF16)
        acc_ref[...] = jnp.zeros_like(acc_ref)

    xb = xb_ref[...]
    act = (_silu(_dot(xb, wg_ref[...])) * _dot(xb, wu_ref[...])).astype(BF16)
    acc_ref[...] += _dot(act, wd_ref[...])

    @pl.when(f == pl.num_programs(1) - 1)
    def _():
        hn = _ln(alpha * h_ref[...] + acc_ref[...], g_ref[...], b_ref[...])
        row = (pl.program_id(0) % tiles_per_seq) * tm + lax.broadcasted_iota(I32, (tm, 1), 0)
        o_ref[...] = jnp.where(row >= PAD, hn, 0.0)


def _ffn(h, w_gu, w_down, g, b, lp, alpha):
    tp, d = h.shape
    ff = w_down.shape[0]
    tm = _divisor_tile(lp, 640, LANE)
    tf = _divisor_tile(ff, 1408, LANE)
    nf = ff // tf
    kern = functools.partial(_ffn_kernel, tm=tm, tiles_per_seq=lp // tm, alpha=alpha)
    return pl.pallas_call(
        kern,
        grid=(tp // tm, nf),
        in_specs=[pl.BlockSpec((tm, d), lambda i, f: (i, 0)),
                  pl.BlockSpec((d, tf), lambda i, f: (0, f)),
                  pl.BlockSpec((d, tf), lambda i, f: (0, nf + f)),
                  pl.BlockSpec((tf, d), lambda i, f: (f, 0)),
                  pl.BlockSpec((1, d), lambda i, f: (0, 0)),
                  pl.BlockSpec((1, d), lambda i, f: (0, 0))],
        out_specs=pl.BlockSpec((tm, d), lambda i, f: (i, 0)),
        out_shape=jax.ShapeDtypeStruct((tp, d), F32),
        scratch_shapes=[pltpu.VMEM((tm, d), BF16), pltpu.VMEM((tm, d), F32)],
        compiler_params=_cparams("parallel", "arbitrary"),
    )(h, w_gu, w_gu, w_down, g.reshape(1, d), b.reshape(1, d))


def _rowcopy_kernel(sidx_ref, didx_ref, src_ref, *rest, chunk):
    dst_ref, sem = rest[-2], rest[-1]

    def row_copy(s, d):
        return pltpu.make_async_copy(src_ref.at[pl.ds(s, 1)], dst_ref.at[pl.ds(d, 1)], sem)

    def issue(r, c):
        row_copy(sidx_ref[0, 0, r], didx_ref[0, 0, r]).start()
        return c

    lax.fori_loop(0, chunk, issue, 0)

    def drain(r, c):
        row_copy(0, 0).wait()
        return c

    lax.fori_loop(0, chunk, drain, 0)


def _rowcopy(src, sidx, didx, n_out, init=None):
    n = sidx.shape[0]
    d = src.shape[1]
    chunk = _divisor_tile(n, 2048, 1)
    nchunk = n // chunk
    idx_spec = pl.BlockSpec((1, 1, chunk), lambda i: (i, 0, 0), memory_space=pltpu.SMEM)
    in_specs = [idx_spec, idx_spec, pl.BlockSpec(memory_space=pl.ANY)]
    args = [sidx.reshape(nchunk, 1, chunk), didx.reshape(nchunk, 1, chunk), src]
    aliases = {}
    if init is not None:
        in_specs.append(pl.BlockSpec(memory_space=pl.ANY))
        args.append(init)
        aliases = {3: 0}
    return pl.pallas_call(
        functools.partial(_rowcopy_kernel, chunk=chunk),
        grid=(nchunk,),
        in_specs=in_specs,
        out_specs=pl.BlockSpec(memory_space=pl.ANY),
        out_shape=jax.ShapeDtypeStruct((n_out, d), src.dtype),
        scratch_shapes=[pltpu.SemaphoreType.DMA(())],
        input_output_aliases=aliases,
        compiler_params=_cparams("arbitrary"),
    )(*args)


def _moe_kernel(te_ref, tv_ref, x_ref, wg_ref, wu_ref, wd_ref, o_ref, xb_ref, acc_ref):
    n = pl.program_id(0)
    f = pl.program_id(1)

    @pl.when(f == 0)
    def _():
        xb_ref[...] = x_ref[...].astype(BF16)
        acc_ref[...] = jnp.zeros_like(acc_ref)

    @pl.when(tv_ref[n] == 1)
    def _():
        xb = xb_ref[...]
        act = (_silu(_dot(xb, wg_ref[0])) * _dot(xb, wu_ref[0])).astype(BF16)
        acc_ref[...] += _dot(act, wd_ref[0])

    @pl.when(f == pl.num_programs(1) - 1)
    def _():
        o_ref[...] = acc_ref[...]


def _moe_experts(x_sorted, tile_expert, tile_valid, w_gu, w_down, tm):
    p, d = x_sorted.shape
    ff = w_down.shape[1]
    tf = _divisor_tile(ff, 896, LANE)
    nf = ff // tf
    fblk = lambda f, n, tv: f * tv[n] + (nf - 1) * (1 - tv[n])
    grid_spec = pltpu.PrefetchScalarGridSpec(
        num_scalar_prefetch=2,
        grid=(p // tm, nf),
        in_specs=[pl.BlockSpec((tm, d), lambda n, f, te, tv: (n, 0)),
                  pl.BlockSpec((1, d, tf), lambda n, f, te, tv: (te[n], 0, fblk(f, n, tv))),
                  pl.BlockSpec((1, d, tf), lambda n, f, te, tv: (te[n], 0, nf + fblk(f, n, tv))),
                  pl.BlockSpec((1, tf, d), lambda n, f, te, tv: (te[n], fblk(f, n, tv), 0))],
        out_specs=pl.BlockSpec((tm, d), lambda n, f, te, tv: (n, 0)),
        scratch_shapes=[pltpu.VMEM((tm, d), BF16), pltpu.VMEM((tm, d), F32)])
    return pl.pallas_call(
        _moe_kernel,
        grid_spec=grid_spec,
        out_shape=jax.ShapeDtypeStruct((p, d), F32),
        compiler_params=_cparams("parallel", "arbitrary"),
    )(tile_expert, tile_valid, x_sorted, w_gu, w_gu, w_down)


def _moe_combine_kernel(y0_ref, y1_ref, rw_ref, h_ref, g_ref, b_ref, o_ref, *, tm, tiles_per_seq, alpha):
    rw = rw_ref[...]
    ff = rw[:, 0:1] * y0_ref[...] + rw[:, 1:2] * y1_ref[...]
    hn = _ln(alpha * h_ref[...] + ff, g_ref[...], b_ref[...])
    row = (pl.program_id(0) % tiles_per_seq) * tm + lax.broadcasted_iota(I32, (tm, 1), 0)
    o_ref[...] = jnp.where(row >= PAD, hn, 0.0)


def _moe_combine(picked, route_w, h, g, b, lp, alpha):
    tp, d = h.shape
    tm = _divisor_tile(lp, 640, LANE)
    nt = tp // tm
    kern = functools.partial(_moe_combine_kernel, tm=tm, tiles_per_seq=lp // tm, alpha=alpha)
    return pl.pallas_call(
        kern,
        grid=(nt,),
        in_specs=[pl.BlockSpec((tm, d), lambda i: (i, 0)),
                  pl.BlockSpec((tm, d), lambda i: (nt + i, 0)),
                  pl.BlockSpec((tm, LANE), lambda i: (i, 0)),
                  pl.BlockSpec((tm, d), lambda i: (i, 0)),
                  pl.BlockSpec((1, d), lambda i: (0, 0)),
                  pl.BlockSpec((1, d), lambda i: (0, 0))],
        out_specs=pl.BlockSpec((tm, d), lambda i: (i, 0)),
        out_shape=jax.ShapeDtypeStruct((tp, d), F32),
        compiler_params=_cparams("parallel"),
    )(picked, picked, route_w, h, g.reshape(1, d), b.reshape(1, d))


def _moe(h, route_idx, route_w, w_gu, w_down, g, b, batch, lp, alpha):
    tp, d = h.shape
    seq_real = lp - PAD
    n_real = batch * seq_real
    tm = 640
    real_rows = (jnp.arange(batch, dtype=I32)[:, None] * lp + PAD + jnp.arange(seq_real, dtype=I32)[None, :])
    real_rows = real_rows.reshape(n_real)
    experts = route_idx[:, :2].reshape(batch, lp, 2)[:, PAD:].reshape(n_real * 2)
    onehot = (experts[:, None] == jnp.arange(N_EXPERTS, dtype=I32)[None, :]).astype(I32)
    csum = jnp.cumsum(onehot, axis=0)
    counts = csum[-1]
    rank = jnp.sum(onehot * csum, axis=1) - 1
    tiles = (counts + tm - 1) // tm
    tile_end = jnp.cumsum(tiles)
    slot = ((tile_end - tiles) * tm)[experts] + rank
    n_tiles = (2 * n_real) // tm + N_EXPERTS
    tile_ids = jnp.arange(n_tiles, dtype=I32)
    tile_expert = jnp.minimum(jnp.sum((tile_ids[:, None] >= tile_end[None, :]).astype(I32), axis=1),
                              N_EXPERTS - 1).astype(I32)
    tile_valid = (tile_ids < tile_end[-1]).astype(I32)

    x_sorted = _rowcopy(h, jnp.repeat(real_rows, 2), slot.astype(I32), n_tiles * tm,
                        init=jnp.zeros((n_tiles * tm, d), F32))
    y_sorted = _moe_experts(x_sorted, tile_expert, tile_valid, w_gu, w_down, tm)
    slot_padded = jnp.zeros((batch, lp, 2), I32).at[:, PAD:].set(slot.reshape(batch, seq_real, 2))
    pick = slot_padded.reshape(tp, 2).T.reshape(2 * tp)
    picked = _rowcopy(y_sorted, pick, jnp.arange(2 * tp, dtype=I32), 2 * tp)
    return _moe_combine(picked, route_w, h, g, b, lp, alpha)


def _prepare_in_proj(w_in):
    d = w_in.shape[0]
    aw = ATT_HEADS * ATT_HEAD_DIM
    inner = SSD_HEADS * SSD_HEAD_DIM
    gs = SSD_GROUPS * SSD_STATE
    sizes = [aw, aw, aw, ATT_HEADS, inner, inner + 2 * gs, SSD_HEADS]
    sc3 = w_in.shape[1] - sum(sizes) - 3 * d
    sizes += [sc3, 3 * d]
    offs = [0]
    for s in sizes:
        offs.append(offs[-1] + s)
    part = lambda k: w_in[:, offs[k]:offs[k + 1]]

    def pad_heads(w, scale=1.0):
        w = (w * scale).reshape(d, ATT_HEADS, ATT_HEAD_DIM)
        return jnp.pad(w, ((0, 0), (0, 0), (0, ATT_HEAD_DIM))).reshape(d, 2 * aw)

    pieces = [("q", pad_heads(part(0), ATT_HEAD_DIM ** -0.5)), ("k", pad_heads(part(1))), ("v", pad_heads(part(2))),
              ("z", part(4)), ("x", part(5)[:, :inner]), ("B", part(5)[:, inner:inner + gs]),
              ("C", part(5)[:, inner + gs:]), ("sc_b", part(7)[:, :sc3 // 3]),
              ("sc_c", part(7)[:, sc3 // 3:2 * sc3 // 3]), ("sc_h", part(7)[:, 2 * sc3 // 3:]),
              ("g_att", part(8)[:, :d]), ("g_ssd", part(8)[:, d:2 * d]), ("g_conv", part(8)[:, 2 * d:])]
    cols, off = {}, 0
    for name, w in pieces:
        cols[name] = off
        off += w.shape[1]
    w_main = jnp.concatenate([w for _, w in pieces], axis=1).astype(BF16)
    w_small = jnp.zeros((d, 2 * LANE), F32).at[:, :ATT_HEADS].set(part(3)).at[:, LANE:LANE + SSD_HEADS].set(part(6))
    ws_hi = w_small.astype(BF16)
    ws_lo = (w_small - ws_hi.astype(F32)).astype(BF16)
    return w_main, ws_hi, ws_lo, cols


def kernel(x, meta_tokens, ln_in_g, ln_in_b, w_in, b_forget, ssd_conv_w, ssd_conv_b, ssd_dt_bias, ssd_a_log,
           ssd_d, ssd_norm_w, sc_conv_w, w_proj_attn, w_proj_ssd, w_proj_conv, w_out, ln_mix_g, ln_mix_b,
           dense_w_gu, dense_w_down, router_w, router_b, moe_w_gu, moe_w_down, ln_ffn_g, ln_ffn_b):
    batch, seq, d = x.shape
    depth = w_in.shape[0]
    alpha = (2 * depth) ** 0.25
    lp = PAD + N_META + seq
    tp = batch * lp
    hw = 2 * ATT_HEAD_DIM

    ln_x = _ln_rows(x.reshape(batch * seq, d), ln_in_g, ln_in_b).reshape(batch, seq, d)
    ln_meta = _ln_rows(meta_tokens.astype(x.dtype), ln_in_g, ln_in_b)
    h = jnp.concatenate([jnp.zeros((batch, PAD, d), F32),
                         jnp.broadcast_to(ln_meta[None], (batch, N_META, d)), ln_x], axis=1).reshape(tp, d)

    for layer in range(depth):
        w_main, ws_hi, ws_lo, cols = _prepare_in_proj(w_in[layer])
        proj, small = _in_proj(h, w_main, ws_hi, ws_lo)
        qa, ka = _attn_prep(proj, small, b_forget[layer], lp)
        y_att = _attention(qa, ka, proj, cols["v"] // hw, batch, lp)
        y_ssd = _ssd(proj, small, cols, ssd_conv_w[layer], ssd_conv_b[layer], ssd_dt_bias[layer],
                     ssd_a_log[layer], ssd_d[layer], ssd_norm_w[layer], batch, lp)
        wpa = jnp.pad(w_proj_attn[layer].reshape(ATT_HEADS, ATT_HEAD_DIM, d),
                      ((0, 0), (0, ATT_HEAD_DIM), (0, 0))).reshape(ATT_HEADS * hw, d).astype(BF16)
        j = layer // 2
        router = None
        if layer % 2 == 1:
            rw = jnp.zeros((d, LANE), F32).at[:, :N_EXPERTS].set(router_w[j])
            rw_hi = rw.astype(BF16)
            rw_lo = (rw - rw_hi.astype(F32)).astype(BF16)
            router = (rw_hi, rw_lo, jnp.zeros((1, LANE), F32).at[0, :N_EXPERTS].set(router_b[j]))
        outs = _merge(y_att, y_ssd, proj, cols, h, wpa, w_proj_ssd[layer].astype(BF16),
                      w_proj_conv[layer].astype(BF16), w_out[layer].astype(BF16), sc_conv_w[layer],
                      ln_mix_g[layer], ln_mix_b[layer], lp, alpha, router)
        if layer % 2 == 0:
            h = _ffn(outs[0], dense_w_gu[j].astype(BF16), dense_w_down[j].astype(BF16),
                     ln_ffn_g[layer], ln_ffn_b[layer], lp, alpha)
        else:
            h_mid, route_idx, route_w = outs
            h = _moe(h_mid, route_idx, route_w, moe_w_gu[j].astype(BF16), moe_w_down[j].astype(BF16),
                     ln_ffn_g[layer], ln_ffn_b[layer], batch, lp, alpha)
    return h.reshape(batch, lp, d)[:, PAD + N_META:]
```

```python
import functools

import jax
import jax.numpy as jnp
from jax import lax
from jax.experimental import pallas as pl
from jax.experimental.pallas import tpu as pltpu

F32, BF16, I32 = jnp.float32, jnp.bfloat16, jnp.int32

N_META = 16
ATT_HEADS = 8
ATT_HEAD_DIM = 64
SSD_HEADS = 16
SSD_HEAD_DIM = 64
SSD_GROUPS = 2
SSD_HPG = SSD_HEADS // SSD_GROUPS
SSD_STATE = 128
SSD_CONV = 4
SSD_CHUNK = 128
SC_CONV = 3
N_EXPERTS = 8
LN_EPS = 1e-5
RMS_EPS = 1e-5
NEG_INF = -1e30
LOG2E = 1.4426950408889634

LANE = 128
SUBLANE = 8
MXU_DIM = 256
VMEM_LIMIT = 56 * 1024 * 1024
TOKEN_TILE_ROWS = SUBLANE


def _cparams(*sem):
    return pltpu.CompilerParams(dimension_semantics=tuple(sem), vmem_limit_bytes=VMEM_LIMIT)


def _divisor_tile(n, target, quantum):
    best = None
    t = quantum
    while t <= min(n, target):
        if n % t == 0:
            best = t
        t += quantum
    assert best is not None, (n, target, quantum)
    return best


def _dot(a, b):
    return jnp.dot(a, b, preferred_element_type=F32)


def _dot_nt(a, b):
    return lax.dot_general(a, b, (((1,), (1,)), ((), ())), preferred_element_type=F32)


def _split2(x):
    hi = x.astype(BF16)
    lo = (x - hi.astype(F32)).astype(BF16)
    return hi, lo


def _split3(x):
    hi = x.astype(BF16)
    r = x - hi.astype(F32)
    mid = r.astype(BF16)
    lo = (r - mid.astype(F32)).astype(BF16)
    return hi, mid, lo


def _dot_x3(a_f32, w_hi, w_lo):
    a_hi, a_lo = _split2(a_f32)
    return _dot(a_hi, w_hi) + _dot(a_hi, w_lo) + _dot(a_lo, w_hi)


def _ln(v, g, b):
    mu = jnp.mean(v, axis=-1, keepdims=True)
    c = v - mu
    var = jnp.mean(c * c, axis=-1, keepdims=True)
    return c * lax.rsqrt(var + LN_EPS) * g + b


def _silu(v):
    return v * jax.nn.sigmoid(v)


def _softplus(v):
    return jnp.maximum(v, 0.0) + jnp.log(1.0 + jnp.exp(-jnp.abs(v)))


def _zero_pad_rows(v, tile_index, tm, tiles_per_seq, pad):
    row = (tile_index % tiles_per_seq) * tm + lax.broadcasted_iota(I32, (tm, 1), 0)
    return jnp.where(row >= pad, v, 0.0)


def _ln_kernel(x_ref, g_ref, b_ref, o_ref):
    o_ref[...] = _ln(x_ref[...].astype(F32), g_ref[...], b_ref[...])


def _ln_rows(x2d, g, b):
    n, d = x2d.shape
    tm = _divisor_tile(n, 1024, SUBLANE)
    return pl.pallas_call(
        _ln_kernel,
        grid=(n // tm,),
        in_specs=[pl.BlockSpec((tm, d), lambda i: (i, 0)),
                  pl.BlockSpec((1, d), lambda i: (0, 0)),
                  pl.BlockSpec((1, d), lambda i: (0, 0))],
        out_specs=pl.BlockSpec((tm, d), lambda i: (i, 0)),
        out_shape=jax.ShapeDtypeStruct((n, d), F32),
        compiler_params=_cparams("parallel"),
        name="ln_in",
    )(x2d, g.reshape(1, d), b.reshape(1, d))


def _in_proj_kernel(x_ref, w_ref, wsh_ref, wsl_ref, o_ref, os_ref, xb_ref):
    @pl.when(pl.program_id(1) == 0)
    def _():
        x = x_ref[...]
        xb_ref[...] = x.astype(BF16)
        os_ref[...] = _dot_x3(x, wsh_ref[...], wsl_ref[...])

    o_ref[...] = _dot(xb_ref[...], w_ref[...]).astype(BF16)


def _in_proj(h, w_main, ws_hi, ws_lo):
    tp, d = h.shape
    n = w_main.shape[1]
    ns = ws_hi.shape[1]
    tm = _divisor_tile(tp, 1280, LANE)
    tn = _divisor_tile(n, 1024, LANE)
    return pl.pallas_call(
        _in_proj_kernel,
        grid=(tp // tm, n // tn),
        in_specs=[pl.BlockSpec((tm, d), lambda i, j: (i, 0)),
                  pl.BlockSpec((d, tn), lambda i, j: (0, j)),
                  pl.BlockSpec((d, ns), lambda i, j: (0, 0)),
                  pl.BlockSpec((d, ns), lambda i, j: (0, 0))],
        out_specs=[pl.BlockSpec((tm, tn), lambda i, j: (i, j)),
                   pl.BlockSpec((tm, ns), lambda i, j: (i, 0))],
        out_shape=[jax.ShapeDtypeStruct((tp, n), BF16),
                   jax.ShapeDtypeStruct((tp, ns), F32)],
        scratch_shapes=[pltpu.VMEM((tm, d), BF16)],
        compiler_params=_cparams("parallel", "arbitrary"),
        name="in_proj",
    )(h, w_main, ws_hi, ws_lo)


def _attn_prep_kernel(k_ref, q_ref, v_ref, f_ref, bf_ref, tri_ref, pk_ref, ko_ref, qo_ref, vo_ref, carry_ref,
                      *, tm, tiles_per_seq, pad):
    t = pl.program_id(0) % tiles_per_seq
    hd = ATT_HEAD_DIM
    hw = 2 * hd

    @pl.when(t == 0)
    def _():
        carry_ref[...] = jnp.zeros_like(carry_ref)

    x = f_ref[...] + bf_ref[...]
    log_f = -_softplus(-x) * LOG2E
    row = t * tm + lax.broadcasted_iota(I32, (tm, LANE), 0)
    lane = lax.broadcasted_iota(I32, (tm, LANE), 1)
    log_f = jnp.where((row >= pad) & (lane < ATT_HEADS), log_f, 0.0)
    tri = tri_ref[...]
    hi, mid, lo = _split3(log_f)
    c = _dot(tri, hi) + _dot(tri, mid) + _dot(tri, lo) + carry_ref[0:1, :]
    carry_ref[0:1, :] = c[tm - 1:tm, :]

    hi, mid, lo = _split3(c)
    packed = jnp.where(lane < 8, hi.astype(F32),
             jnp.where(lane < 16, pltpu.roll(mid.astype(F32), 8, 1),
             jnp.where(lane < 24, pltpu.roll(lo.astype(F32), 16, 1),
             jnp.where(lane == 24, 1.0, 0.0)))).astype(BF16)
    ko_ref[...] = (k_ref[...].astype(F32) + _dot(packed, pk_ref[...])).astype(BF16)

    q_t = q_ref[...].astype(F32).T
    hi_t, mid_t, lo_t = (p.astype(F32) for p in _split3(c.T))
    sub = lax.broadcasted_iota(I32, (SUBLANE, tm), 0)
    for h in range(ATT_HEADS):
        bias = jnp.where(sub == 0, hi_t[h:h + 1, :],
               jnp.where(sub == 1, mid_t[h:h + 1, :],
               jnp.where(sub == 2, lo_t[h:h + 1, :],
               jnp.where(sub < 6, 1.0, 0.0))))
        blk = jnp.concatenate([q_t[h * hd:(h + 1) * hd, :], bias, jnp.zeros((hw - hd - SUBLANE, tm), F32)], axis=0)
        qo_ref[h * hw:(h + 1) * hw, :] = blk.astype(BF16)

    v_t = v_ref[...].astype(F32).T
    for h in range(ATT_HEADS):
        vo_ref[h, 0] = v_t[h * hd:(h + 1) * hd, :].astype(BF16)


def _key_bias_placement():
    hw = 2 * ATT_HEAD_DIM
    rows, cols, vals = [], [], []
    for h in range(ATT_HEADS):
        base = h * hw + ATT_HEAD_DIM
        for piece in range(3):
            rows.append(24); cols.append(base + piece); vals.append(1.0)
            rows.append(8 * piece + h); cols.append(base + 3 + piece); vals.append(-1.0)
    pk = jnp.zeros((LANE, ATT_HEADS * hw), F32).at[jnp.array(rows), jnp.array(cols)].set(jnp.array(vals))
    return pk.astype(BF16)


def _attn_prep(proj, small, cols, b_forget, batch, lp, pad, tk):
    tp = proj.shape[0]
    hw = 2 * ATT_HEAD_DIM
    kw = ATT_HEADS * hw
    aw = ATT_HEADS * ATT_HEAD_DIM
    tm = tk
    tiles_per_seq = lp // tm
    tri = jnp.tril(jnp.ones((tm, tm), F32)).astype(BF16)
    bf = jnp.zeros((1, LANE), F32).at[0, :ATT_HEADS].set(b_forget)
    kern = functools.partial(_attn_prep_kernel, tm=tm, tiles_per_seq=tiles_per_seq, pad=pad)
    return pl.pallas_call(
        kern,
        grid=(tp // tm,),
        in_specs=[pl.BlockSpec((tm, kw), lambda i: (i, cols["k"] // kw)),
                  pl.BlockSpec((tm, aw), lambda i: (i, cols["q"] // aw)),
                  pl.BlockSpec((tm, aw), lambda i: (i, cols["v"] // aw)),
                  pl.BlockSpec((tm, LANE), lambda i: (i, 0)),
                  pl.BlockSpec((1, LANE), lambda i: (0, 0)),
                  pl.BlockSpec((tm, tm), lambda i: (0, 0)),
                  pl.BlockSpec((LANE, kw), lambda i: (0, 0))],
        out_specs=[pl.BlockSpec((tm, kw), lambda i: (i, 0)),
                   pl.BlockSpec((kw, tm), lambda i: (i // tiles_per_seq, i % tiles_per_seq)),
                   pl.BlockSpec((ATT_HEADS, 1, ATT_HEAD_DIM, tm),
                                lambda i: (i // tiles_per_seq, i % tiles_per_seq, 0, 0))],
        out_shape=[jax.ShapeDtypeStruct((tp, kw), BF16),
                   jax.ShapeDtypeStruct((batch * kw, lp), BF16),
                   jax.ShapeDtypeStruct((batch * ATT_HEADS, tiles_per_seq, ATT_HEAD_DIM, tm), BF16)],
        scratch_shapes=[pltpu.VMEM((SUBLANE, LANE), F32)],
        compiler_params=_cparams("arbitrary"),
        name="attn_prep",
    )(proj, proj, proj, small, bf, tri, _key_bias_placement())


def _attn_kernel(qt_ref, k_ref, vt_ref, o_ref, m_ref, l_ref, acc_ref, *, tq, tk, pad, heads):
    i = pl.program_id(2)
    r = tq // tk
    hd = ATT_HEAD_DIM
    hw = 2 * hd
    m_ref[...] = jnp.full_like(m_ref, NEG_INF)
    l_ref[...] = jnp.zeros_like(l_ref)
    acc_ref[...] = jnp.zeros_like(acc_ref)

    def step(j, masked):
        start = pl.multiple_of(j * tk, tk)
        if masked:
            kpos = j * tk + lax.broadcasted_iota(I32, (tk, tq), 0)
            qpos = i * tq + lax.broadcasted_iota(I32, (tk, tq), 1)
            visible = (kpos <= qpos) & (kpos >= pad)
        for hh in range(heads):
            k = k_ref[pl.ds(start, tk), hh * hw:(hh + 1) * hw]
            s = _dot(k, qt_ref[hh * hw:(hh + 1) * hw, :])
            if masked:
                s = jnp.where(visible, s, NEG_INF)
            m_old = m_ref[hh:hh + 1, :]
            m_new = jnp.maximum(m_old, jnp.max(s, axis=0, keepdims=True))
            p = jnp.exp2(s - m_new)
            alpha = jnp.exp2(m_old - m_new)
            l_ref[hh:hh + 1, :] = alpha * l_ref[hh:hh + 1, :] + jnp.sum(p, axis=0, keepdims=True)
            acc_ref[hh * hd:(hh + 1) * hd, :] = (alpha * acc_ref[hh * hd:(hh + 1) * hd, :]
                                                 + _dot(vt_ref[hh, j], p.astype(BF16)))
            m_ref[hh:hh + 1, :] = m_new

    def masked_body(j, c):
        step(j, True)
        return c

    def plain_body(j, c):
        step(j, False)
        return c

    step(0, True)
    lax.fori_loop(1, r * i, plain_body, 0)
    lax.fori_loop(jnp.maximum(r * i, 1), r * i + r, masked_body, 0)
    for hh in range(heads):
        o_ref[hh * hd:(hh + 1) * hd, :] = (acc_ref[hh * hd:(hh + 1) * hd, :] / l_ref[hh:hh + 1, :]).astype(BF16)


ATT_HEADS_PER_STEP = 4


def _attention(q_t, ka, v_t, batch, lp, pad, tq, tk):
    hd = ATT_HEAD_DIM
    hw = 2 * hd
    nh = ATT_HEADS_PER_STEP
    groups = ATT_HEADS // nh
    nq = lp // tq
    nkv = lp // tk
    kern = functools.partial(_attn_kernel, tq=tq, tk=tk, pad=pad, heads=nh)
    return pl.pallas_call(
        kern,
        grid=(batch, groups, nq),
        in_specs=[pl.BlockSpec((nh * hw, tq), lambda b, g, i: (b * groups + g, i)),
                  pl.BlockSpec((lp, nh * hw), lambda b, g, i: (b, g)),
                  pl.BlockSpec((nh, nkv, hd, tk), lambda b, g, i: (b * groups + g, 0, 0, 0))],
        out_specs=pl.BlockSpec((nh * hd, tq), lambda b, g, i: (b * groups + g, i)),
        out_shape=jax.ShapeDtypeStruct((batch * ATT_HEADS * hd, lp), BF16),
        scratch_shapes=[pltpu.VMEM((nh, tq), F32), pltpu.VMEM((nh, tq), F32), pltpu.VMEM((nh * hd, tq), F32)],
        compiler_params=_cparams("parallel", "parallel", "arbitrary"),
        name="fox_attention",
    )(q_t, ka, v_t)


def _ssd_kernel(x_ref, b_ref, c_ref, z_ref, dt_ref, cw_ref, cb_ref, dtb_ref, alog_ref, dexp_ref, nw_ref,
                tri_ref, e_ref, o_ref, ubuf_ref, state_ref, ybuf_ref, *, pad):
    chunk = pl.program_id(1)
    blk = SSD_CHUNK
    inner = SSD_HEADS * SSD_HEAD_DIM
    gs = SSD_GROUPS * SSD_STATE
    gw = SSD_HPG * SSD_HEAD_DIM

    @pl.when(chunk == 0)
    def _():
        ubuf_ref[0:SUBLANE, :] = jnp.zeros((SUBLANE, ubuf_ref.shape[1]), F32)
        state_ref[...] = jnp.zeros_like(state_ref)

    ubuf_ref[SUBLANE:SUBLANE + blk, 0:inner] = x_ref[...].astype(F32)
    ubuf_ref[SUBLANE:SUBLANE + blk, inner:inner + gs] = b_ref[...].astype(F32)
    ubuf_ref[SUBLANE:SUBLANE + blk, inner + gs:inner + 2 * gs] = c_ref[...].astype(F32)
    conv = cb_ref[...]
    for k in range(SSD_CONV):
        conv = conv + cw_ref[k:k + 1, :] * ubuf_ref[pl.ds(SUBLANE - (SSD_CONV - 1) + k, blk), :]
    ubuf_ref[0:SUBLANE, :] = ubuf_ref[blk:blk + SUBLANE, :]
    xc = _silu(conv)
    xs = xc[:, 0:inner]
    bm = xc[:, inner:inner + gs]
    cm = xc[:, inner + gs:inner + 2 * gs]

    row = lax.broadcasted_iota(I32, (blk, LANE), 0)
    lane = lax.broadcasted_iota(I32, (blk, LANE), 1)
    dt = _softplus(dt_ref[...] + dtb_ref[...])
    dt = jnp.where((lane < SSD_HEADS) & (chunk * blk + row >= pad), dt, 0.0)
    a = -jnp.exp(alog_ref[...])
    tri = tri_ref[...]
    hi, mid, lo = _split3(dt * a)
    a_cum = _dot(tri, hi) + _dot(tri, mid) + _dot(tri, lo)
    a_last = a_cum[blk - 1:blk, :]
    ea = jnp.exp(a_cum)
    wgt = jnp.exp(a_last - a_cum) * dt

    def pieces(v, at):
        v_hi = v.astype(BF16).astype(F32)
        v_lo = v - v_hi
        return (pltpu.roll(v_hi, at, 1) if at else v_hi), pltpu.roll(v_lo, at + SSD_HEADS, 1)

    dt_hi, dt_lo = pieces(dt, 0)
    ea_hi, ea_lo = pieces(ea, 2 * SSD_HEADS)
    wg_hi, wg_lo = pieces(wgt, 4 * SSD_HEADS)
    h16 = SSD_HEADS
    packed = jnp.where(lane < h16, dt_hi,
             jnp.where(lane < 2 * h16, dt_lo,
             jnp.where(lane < 3 * h16, ea_hi,
             jnp.where(lane < 4 * h16, ea_lo,
             jnp.where(lane < 5 * h16, wg_hi,
             jnp.where(lane < 6 * h16, wg_lo, 0.0)))))).astype(BF16)
    expanded = _dot(packed, e_ref[...])
    dt_exp = expanded[:, 0:inner]
    ea_exp = expanded[:, inner:2 * inner]
    wg_exp = expanded[:, 2 * inner:3 * inner]
    xdt = (xs * dt_exp).astype(BF16)
    xw = (xs * wg_exp).astype(BF16)

    a_cum_t = a_cum.T
    causal = row >= lane
    for g in range(SSD_GROUPS):
        cg = cm[:, g * SSD_STATE:(g + 1) * SSD_STATE].astype(BF16)
        bg_f32 = bm[:, g * SSD_STATE:(g + 1) * SSD_STATE]
        cbg = _dot_nt(cg, bg_f32.astype(BF16))
        mats = []
        for hh in range(SSD_HPG):
            h = g * SSD_HPG + hh
            diff = a_cum[:, h:h + 1] - a_cum_t[h:h + 1, :]
            decay = jnp.exp(jnp.where(causal, diff, NEG_INF))
            mats.append((cbg * decay).astype(BF16))
        for j in range(SSD_HPG // 2):
            col = g * gw + j * LANE
            xp = xdt[:, col:col + LANE]
            y0 = _dot(mats[2 * j], xp)
            y1 = _dot(mats[2 * j + 1], xp)
            ybuf_ref[:, col:col + LANE] = jnp.where(lane < SSD_HEAD_DIM, y0, y1)
        st = state_ref[g]
        y_off = _dot(cg, st.astype(BF16))
        ybuf_ref[:, g * gw:(g + 1) * gw] += y_off * ea_exp[:, g * gw:(g + 1) * gw]
        chunk_decay = ea_exp[blk - 1:blk, g * gw:(g + 1) * gw]
        state_ref[g] = st * chunk_decay + _dot(bg_f32.T.astype(BF16), xw[:, g * gw:(g + 1) * gw])

    y = ybuf_ref[...] + dexp_ref[...] * xs
    u = y * _silu(z_ref[...].astype(F32))
    for g in range(SSD_GROUPS):
        ug = u[:, g * gw:(g + 1) * gw]
        ug = ug * lax.rsqrt(jnp.mean(ug * ug, axis=-1, keepdims=True) + RMS_EPS)
        o_ref[:, g * gw:(g + 1) * gw] = (ug * nw_ref[:, g * gw:(g + 1) * gw]).astype(BF16)


def _ssd_expand_matrix():
    inner = SSD_HEADS * SSD_HEAD_DIM
    rows = jnp.arange(LANE)
    cols = jnp.arange(3 * inner)
    seg = cols // inner
    head = (cols % inner) // SSD_HEAD_DIM
    hit = (rows[:, None] // (2 * SSD_HEADS) == seg[None, :]) & (rows[:, None] % SSD_HEADS == head[None, :]) \
          & (rows[:, None] < 6 * SSD_HEADS)
    return hit.astype(BF16)


def _ssd(proj, small, cols, conv_w, conv_b, dt_bias, a_log, d, norm_w, batch, lp, pad):
    tp = proj.shape[0]
    blk = SSD_CHUNK
    inner = SSD_HEADS * SSD_HEAD_DIM
    gs = SSD_GROUPS * SSD_STATE
    nc = lp // blk
    cdim = inner + 2 * gs
    cw = jnp.zeros((SUBLANE, cdim), F32).at[:SSD_CONV].set(conv_w)
    pad_lane = lambda v: jnp.zeros((1, LANE), F32).at[0, :v.shape[0]].set(v)
    dexp = jnp.repeat(d, SSD_HEAD_DIM).reshape(1, inner)
    tri = jnp.tril(jnp.ones((blk, blk), F32)).astype(BF16)
    rowblk = lambda b, c: b * nc + c
    const = lambda shape: pl.BlockSpec(shape, lambda b, c: (0,) * len(shape))
    return pl.pallas_call(
        functools.partial(_ssd_kernel, pad=pad),
        grid=(batch, nc),
        in_specs=[pl.BlockSpec((blk, inner), lambda b, c: (rowblk(b, c), cols["x"] // inner)),
                  pl.BlockSpec((blk, gs), lambda b, c: (rowblk(b, c), cols["B"] // gs)),
                  pl.BlockSpec((blk, gs), lambda b, c: (rowblk(b, c), cols["C"] // gs)),
                  pl.BlockSpec((blk, inner), lambda b, c: (rowblk(b, c), cols["z"] // inner)),
                  pl.BlockSpec((blk, LANE), lambda b, c: (rowblk(b, c), 1)),
                  const((SUBLANE, cdim)), const((1, cdim)), const((1, LANE)), const((1, LANE)),
                  const((1, inner)), const((1, inner)), const((blk, blk)), const((LANE, 3 * inner))],
        out_specs=pl.BlockSpec((blk, inner), lambda b, c: (rowblk(b, c), 0)),
        out_shape=jax.ShapeDtypeStruct((tp, inner), BF16),
        scratch_shapes=[pltpu.VMEM((blk + SUBLANE, cdim), F32),
                        pltpu.VMEM((SSD_GROUPS, SSD_STATE, SSD_HPG * SSD_HEAD_DIM), F32),
                        pltpu.VMEM((blk, inner), F32)],
        compiler_params=_cparams("parallel", "arbitrary"),
        name="ssd",
    )(proj, proj, proj, proj, small, cw, conv_b.reshape(1, cdim), pad_lane(dt_bias), pad_lane(a_log),
      dexp, norm_w.reshape(1, inner), tri, _ssd_expand_matrix())


def _merge_kernel(yat_ref, ys_ref, sb_ref, sc_ref, sh_ref, sch_ref, shh_ref, ga_ref, gs_ref, gc_ref, h_ref,
                  wpa_ref, wps_ref, wpc_ref, wo_ref, scw_ref, g_ref, b_ref, *rest,
                  tm, tiles_per_seq, pad, alpha, with_router):
    if with_router:
        rwh_ref, rwl_ref, rb_ref, o_ref, ri_ref, rw_ref, vbuf_ref = rest
    else:
        o_ref, vbuf_ref = rest
    vbuf_ref[0:SUBLANE, :] = sch_ref[...].astype(F32) * shh_ref[...].astype(F32)
    vbuf_ref[SUBLANE:SUBLANE + tm, :] = sc_ref[...].astype(F32) * sh_ref[...].astype(F32)
    conv = jnp.zeros((tm, vbuf_ref.shape[1]), F32)
    for k in range(SC_CONV):
        conv = conv + scw_ref[k:k + 1, :] * vbuf_ref[pl.ds(SUBLANE - (SC_CONV - 1) + k, tm), :]
    y_conv = (sb_ref[...].astype(F32) * conv).astype(BF16)
    y_att = yat_ref[...].astype(F32).T.astype(BF16)

    merged = jax.nn.sigmoid(ga_ref[...].astype(F32)) * _dot(y_att, wpa_ref[...])
    merged += jax.nn.sigmoid(gs_ref[...].astype(F32)) * _dot(ys_ref[...], wps_ref[...])
    merged += jax.nn.sigmoid(gc_ref[...].astype(F32)) * _dot(y_conv, wpc_ref[...])
    mix = _dot(merged.astype(BF16), wo_ref[...])
    hn = _ln(alpha * h_ref[...] + mix, g_ref[...], b_ref[...])
    hn = _zero_pad_rows(hn, pl.program_id(0), tm, tiles_per_seq, pad)
    o_ref[...] = hn

    if with_router:
        lane = lax.broadcasted_iota(I32, (tm, LANE), 1)
        lane_f = lane.astype(F32)
        logits = _dot_x3(hn, rwh_ref[...], rwl_ref[...]) + rb_ref[...]
        lg = jnp.where(lane < N_EXPERTS, logits, -jnp.inf)
        v1 = jnp.max(lg, axis=1, keepdims=True)
        i1 = jnp.min(jnp.where(lg == v1, lane_f, float(LANE)), axis=1, keepdims=True).astype(I32)
        lg2 = jnp.where(lane == i1, -jnp.inf, lg)
        v2 = jnp.max(lg2, axis=1, keepdims=True)
        i2 = jnp.min(jnp.where(lg2 == v2, lane_f, float(LANE)), axis=1, keepdims=True).astype(I32)
        e = jnp.exp(v2 - v1)
        w1 = 1.0 / (1.0 + e)
        w2 = e / (1.0 + e)
        ri_ref[...] = jnp.where(lane == 0, i1, jnp.where(lane == 1, i2, 0))
        rw_ref[...] = jnp.where(lane == 0, w1, jnp.where(lane == 1, w2, 0.0))


def _merge(y_att_t, y_ssd, proj, cols, h, wpa, wps, wpc, wo, sc_w, g, b, lp, pad, alpha, router=None):
    tp, d = h.shape
    scw = wpc.shape[0]
    aw = wpa.shape[0]
    tm = _divisor_tile(lp, 768, LANE)
    tiles_per_seq = lp // tm
    nt = tp // tm
    halo = lambda i: jnp.maximum(i * (tm // SUBLANE) - 1, 0)
    const = lambda shape: pl.BlockSpec(shape, lambda i: (0,) * len(shape))
    in_specs = [pl.BlockSpec((aw, tm), lambda i: (i // tiles_per_seq, i % tiles_per_seq)),
                pl.BlockSpec((tm, y_ssd.shape[1]), lambda i: (i, 0)),
                pl.BlockSpec((tm, scw), lambda i: (i, cols["sc_b"] // scw)),
                pl.BlockSpec((tm, scw), lambda i: (i, cols["sc_c"] // scw)),
                pl.BlockSpec((tm, scw), lambda i: (i, cols["sc_h"] // scw)),
                pl.BlockSpec((SUBLANE, scw), lambda i: (halo(i), cols["sc_c"] // scw)),
                pl.BlockSpec((SUBLANE, scw), lambda i: (halo(i), cols["sc_h"] // scw)),
                pl.BlockSpec((tm, d), lambda i: (i, cols["g_att"] // d)),
                pl.BlockSpec((tm, d), lambda i: (i, cols["g_ssd"] // d)),
                pl.BlockSpec((tm, d), lambda i: (i, cols["g_conv"] // d)),
                pl.BlockSpec((tm, d), lambda i: (i, 0)),
                const(wpa.shape), const(wps.shape), const(wpc.shape), const(wo.shape),
                const((SUBLANE, scw)), const((1, d)), const((1, d))]
    args = [y_att_t, y_ssd, proj, proj, proj, proj, proj, proj, proj, proj, h, wpa, wps, wpc, wo,
            jnp.zeros((SUBLANE, scw), F32).at[:SC_CONV].set(sc_w), g.reshape(1, d), b.reshape(1, d)]
    out_specs = [pl.BlockSpec((tm, d), lambda i: (i, 0))]
    out_shape = [jax.ShapeDtypeStruct((tp, d), F32)]
    if router is not None:
        rw_hi, rw_lo, rb = router
        in_specs += [const(rw_hi.shape), const(rw_lo.shape), const((1, LANE))]
        args += [rw_hi, rw_lo, rb]
        out_specs += [pl.BlockSpec((tm, LANE), lambda i: (i, 0)), pl.BlockSpec((tm, LANE), lambda i: (i, 0))]
        out_shape += [jax.ShapeDtypeStruct((tp, LANE), I32), jax.ShapeDtypeStruct((tp, LANE), F32)]
    kern = functools.partial(_merge_kernel, tm=tm, tiles_per_seq=tiles_per_seq, pad=pad, alpha=alpha,
                             with_router=router is not None)
    return pl.pallas_call(
        kern, grid=(nt,), in_specs=in_specs, out_specs=out_specs, out_shape=out_shape,
        scratch_shapes=[pltpu.VMEM((tm + SUBLANE, scw), F32)],
        compiler_params=_cparams("parallel"),
        name="merge_out",
    )(*args)


def _ffn_kernel(h_ref, wg_ref, wu_ref, wd_ref, g_ref, b_ref, o_ref, xb_ref, acc_ref,
                *, tm, tiles_per_seq, pad, alpha):
    f = pl.program_id(1)

    @pl.when(f == 0)
    def _():
        xb_ref[...] = h_ref[...].astype(BF16)
        acc_ref[...] = jnp.zeros_like(acc_ref)

    xb = xb_ref[...]
    act = (_silu(_dot(xb, wg_ref[...])) * _dot(xb, wu_ref[...])).astype(BF16)
    acc_ref[...] += _dot(act, wd_ref[...])

    @pl.when(f == pl.num_programs(1) - 1)
    def _():
        hn = _ln(alpha * h_ref[...] + acc_ref[...], g_ref[...], b_ref[...])
        o_ref[...] = _zero_pad_rows(hn, pl.program_id(0), tm, tiles_per_seq, pad)


def _ffn(h, w_gu, w_down, g, b, lp, pad, alpha):
    tp, d = h.shape
    ff = w_down.shape[0]
    tm = _divisor_tile(lp, 768, LANE)
    tf = _divisor_tile(ff, 1408, LANE)
    nf = ff // tf
    kern = functools.partial(_ffn_kernel, tm=tm, tiles_per_seq=lp // tm, pad=pad, alpha=alpha)
    return pl.pallas_call(
        kern,
        grid=(tp // tm, nf),
        in_specs=[pl.BlockSpec((tm, d), lambda i, f: (i, 0)),
                  pl.BlockSpec((d, tf), lambda i, f: (0, f)),
                  pl.BlockSpec((d, tf), lambda i, f: (0, nf + f)),
                  pl.BlockSpec((tf, d), lambda i, f: (f, 0)),
                  pl.BlockSpec((1, d), lambda i, f: (0, 0)),
                  pl.BlockSpec((1, d), lambda i, f: (0, 0))],
        out_specs=pl.BlockSpec((tm, d), lambda i, f: (i, 0)),
        out_shape=jax.ShapeDtypeStruct((tp, d), F32),
        scratch_shapes=[pltpu.VMEM((tm, d), BF16), pltpu.VMEM((tm, d), F32)],
        compiler_params=_cparams("parallel", "arbitrary"),
        name="dense_ffn",
    )(h, w_gu, w_gu, w_down, g.reshape(1, d), b.reshape(1, d))


def _moe_scatter_kernel(slot_ref, h_ref, init_ref, o_ref, buf_ref, sem, *, tm):
    del init_ref
    for s in range(TOKEN_TILE_ROWS):
        buf_ref[pl.ds(s, tm, stride=TOKEN_TILE_ROWS), :] = h_ref[:, s * LANE:(s + 1) * LANE]

    def issue(r, c):
        src = buf_ref.at[pl.ds(pl.multiple_of(r * TOKEN_TILE_ROWS, TOKEN_TILE_ROWS), TOKEN_TILE_ROWS)]
        for k in range(2):
            slot = slot_ref[0, 0, 2 * r + k]
            dst = o_ref.at[pl.ds(pl.multiple_of(slot * TOKEN_TILE_ROWS, TOKEN_TILE_ROWS), TOKEN_TILE_ROWS)]
            pltpu.make_async_copy(src, dst, sem).start()
        return c

    lax.fori_loop(0, tm, issue, 0)
    for _ in range(2):
        pltpu.make_async_copy(buf_ref, o_ref.at[pl.ds(0, tm * TOKEN_TILE_ROWS)], sem).wait()


def _moe_scatter(h, slots, n_rows, lp):
    tp, d = h.shape
    assert d == TOKEN_TILE_ROWS * LANE
    tm = _divisor_tile(lp, 768, LANE)
    nt = tp // tm
    init = jnp.zeros((n_rows * TOKEN_TILE_ROWS, LANE), F32)
    return pl.pallas_call(
        functools.partial(_moe_scatter_kernel, tm=tm),
        grid=(nt,),
        in_specs=[pl.BlockSpec((1, 1, 2 * tm), lambda i: (i, 0, 0), memory_space=pltpu.SMEM),
                  pl.BlockSpec((tm, d), lambda i: (i, 0)),
                  pl.BlockSpec(memory_space=pl.ANY)],
        out_specs=pl.BlockSpec(memory_space=pl.ANY),
        out_shape=jax.ShapeDtypeStruct(init.shape, F32),
        scratch_shapes=[pltpu.VMEM((tm * TOKEN_TILE_ROWS, LANE), F32), pltpu.SemaphoreType.DMA(())],
        input_output_aliases={2: 0},
        compiler_params=_cparams("arbitrary"),
        name="moe_scatter",
    )(slots.reshape(nt, 1, 2 * tm), h, init)


def _moe_kernel(te_ref, tv_ref, x_ref, wg_ref, wu_ref, wd_ref, o_ref, xb_ref, acc_ref, *, tm):
    n = pl.program_id(0)
    f = pl.program_id(1)

    @pl.when(f == 0)
    def _():
        for s in range(TOKEN_TILE_ROWS):
            xb_ref[:, s * LANE:(s + 1) * LANE] = x_ref[pl.ds(s, tm, stride=TOKEN_TILE_ROWS), :].astype(BF16)
        acc_ref[...] = jnp.zeros_like(acc_ref)

    @pl.when(tv_ref[n] == 1)
    def _():
        xb = xb_ref[...]
        act = (_silu(_dot(xb, wg_ref[0])) * _dot(xb, wu_ref[0])).astype(BF16)
        acc_ref[...] += _dot(act, wd_ref[0])

    @pl.when(f == pl.num_programs(1) - 1)
    def _():
        for s in range(TOKEN_TILE_ROWS):
            o_ref[pl.ds(s, tm, stride=TOKEN_TILE_ROWS), :] = acc_ref[:, s * LANE:(s + 1) * LANE]


def _moe_experts(x_sorted, tile_expert, tile_valid, w_gu, w_down, tm, n_tiles):
    d = w_down.shape[2]
    ff = w_down.shape[1]
    tf = _divisor_tile(ff, 896, LANE)
    nf = ff // tf
    fblk = lambda f, n, tv: f * tv[n] + (nf - 1) * (1 - tv[n])
    rows = tm * TOKEN_TILE_ROWS
    grid_spec = pltpu.PrefetchScalarGridSpec(
        num_scalar_prefetch=2,
        grid=(n_tiles, nf),
        in_specs=[pl.BlockSpec((rows, LANE), lambda n, f, te, tv: (n, 0)),
                  pl.BlockSpec((1, d, tf), lambda n, f, te, tv: (te[n], 0, fblk(f, n, tv))),
                  pl.BlockSpec((1, d, tf), lambda n, f, te, tv: (te[n], 0, nf + fblk(f, n, tv))),
                  pl.BlockSpec((1, tf, d), lambda n, f, te, tv: (te[n], fblk(f, n, tv), 0))],
        out_specs=pl.BlockSpec((rows, LANE), lambda n, f, te, tv: (n, 0)),
        scratch_shapes=[pltpu.VMEM((tm, d), BF16), pltpu.VMEM((tm, d), F32)])
    return pl.pallas_call(
        functools.partial(_moe_kernel, tm=tm),
        grid_spec=grid_spec,
        out_shape=jax.ShapeDtypeStruct((n_tiles * rows, LANE), F32),
        compiler_params=_cparams("parallel", "arbitrary"),
        name="moe_experts",
    )(tile_expert, tile_valid, x_sorted, w_gu, w_gu, w_down)


def _moe_combine_kernel(slot_ref, y_ref, rw_ref, h_ref, g_ref, b_ref, o_ref, gbuf_ref, sem,
                        *, tm, tiles_per_seq, pad, alpha):
    def issue(r, c):
        for k in range(2):
            slot = slot_ref[0, 0, 2 * r + k]
            src = y_ref.at[pl.ds(pl.multiple_of(slot * TOKEN_TILE_ROWS, TOKEN_TILE_ROWS), TOKEN_TILE_ROWS)]
            dst = gbuf_ref.at[pl.ds(pl.multiple_of((k * tm + r) * TOKEN_TILE_ROWS, TOKEN_TILE_ROWS),
                                    TOKEN_TILE_ROWS)]
            pltpu.make_async_copy(src, dst, sem).start()
        return c

    lax.fori_loop(0, tm, issue, 0)
    pltpu.make_async_copy(y_ref.at[pl.ds(0, 2 * tm * TOKEN_TILE_ROWS)], gbuf_ref, sem).wait()

    rw = rw_ref[...]
    w0, w1 = rw[:, 0:1], rw[:, 1:2]
    ff = jnp.concatenate(
        [w0 * gbuf_ref[pl.ds(s, tm, stride=TOKEN_TILE_ROWS), :]
         + w1 * gbuf_ref[pl.ds(tm * TOKEN_TILE_ROWS + s, tm, stride=TOKEN_TILE_ROWS), :]
         for s in range(TOKEN_TILE_ROWS)], axis=1)
    hn = _ln(alpha * h_ref[...] + ff, g_ref[...], b_ref[...])
    o_ref[...] = _zero_pad_rows(hn, pl.program_id(0), tm, tiles_per_seq, pad)


def _moe_combine(y_sorted, slots, route_w, h, g, b, lp, pad, alpha):
    tp, d = h.shape
    tm = _divisor_tile(lp, 768, LANE)
    nt = tp // tm
    kern = functools.partial(_moe_combine_kernel, tm=tm, tiles_per_seq=lp // tm, pad=pad, alpha=alpha)
    return pl.pallas_call(
        kern,
        grid=(nt,),
        in_specs=[pl.BlockSpec((1, 1, 2 * tm), lambda i: (i, 0, 0), memory_space=pltpu.SMEM),
                  pl.BlockSpec(memory_space=pl.ANY),
                  pl.BlockSpec((tm, LANE), lambda i: (i, 0)),
                  pl.BlockSpec((tm, d), lambda i: (i, 0)),
                  pl.BlockSpec((1, d), lambda i: (0, 0)),
                  pl.BlockSpec((1, d), lambda i: (0, 0))],
        out_specs=pl.BlockSpec((tm, d), lambda i: (i, 0)),
        out_shape=jax.ShapeDtypeStruct((tp, d), F32),
        scratch_shapes=[pltpu.VMEM((2 * tm * TOKEN_TILE_ROWS, LANE), F32), pltpu.SemaphoreType.DMA(())],
        compiler_params=_cparams("arbitrary"),
        name="moe_combine",
    )(slots.reshape(nt, 1, 2 * tm), y_sorted, route_w, h, g.reshape(1, d), b.reshape(1, d))


def _moe(h, route_idx, route_w, w_gu, w_down, g, b, batch, lp, pad, alpha):
    tp, d = h.shape
    n_real = batch * (lp - pad)
    tm = 640
    experts = route_idx[:, :2].reshape(tp * 2)
    seq_row = jnp.arange(tp, dtype=I32) % lp
    real = jnp.repeat(seq_row >= pad, 2)
    onehot = ((experts[:, None] == jnp.arange(N_EXPERTS, dtype=I32)[None, :]) & real[:, None]).astype(I32)
    csum = jnp.cumsum(onehot, axis=0)
    counts = csum[-1]
    rank = jnp.sum(onehot * csum, axis=1) - 1
    tiles = (counts + tm - 1) // tm
    tile_end = jnp.cumsum(tiles)
    slot = ((tile_end - tiles) * tm)[experts] + rank
    n_tiles = (2 * n_real) // tm + N_EXPERTS
    tile_ids = jnp.arange(n_tiles, dtype=I32)
    tile_expert = jnp.minimum(jnp.sum((tile_ids[:, None] >= tile_end[None, :]).astype(I32), axis=1),
                              N_EXPERTS - 1).astype(I32)
    tile_valid = (tile_ids < tile_end[-1]).astype(I32)
    n_pad = batch * pad * 2
    spare_tiles = -(-n_pad // tm)
    pad_id = jnp.repeat((jnp.arange(tp, dtype=I32) // lp) * pad + seq_row, 2) * 2 + jnp.tile(jnp.arange(2, dtype=I32), tp)
    slot_scatter = jnp.where(real, slot, n_tiles * tm + pad_id).astype(I32)
    slot_gather = jnp.where(real, slot, 0).astype(I32)

    x_sorted = _moe_scatter(h, slot_scatter, (n_tiles + spare_tiles) * tm, lp)
    y_sorted = _moe_experts(x_sorted, tile_expert, tile_valid, w_gu, w_down, tm, n_tiles)
    return _moe_combine(y_sorted, slot_gather, route_w, h, g, b, lp, pad, alpha)


def _prepare_in_proj(w_in):
    d = w_in.shape[0]
    aw = ATT_HEADS * ATT_HEAD_DIM
    inner = SSD_HEADS * SSD_HEAD_DIM
    gs = SSD_GROUPS * SSD_STATE
    sizes = [aw, aw, aw, ATT_HEADS, inner, inner + 2 * gs, SSD_HEADS]
    sc3 = w_in.shape[1] - sum(sizes) - 3 * d
    sizes += [sc3, 3 * d]
    offs = [0]
    for s in sizes:
        offs.append(offs[-1] + s)
    part = lambda k: w_in[:, offs[k]:offs[k + 1]]

    def pad_heads(w):
        w = w.reshape(d, ATT_HEADS, ATT_HEAD_DIM)
        return jnp.pad(w, ((0, 0), (0, 0), (0, ATT_HEAD_DIM))).reshape(d, 2 * aw)

    pieces = [("k", pad_heads(part(1))), ("q", part(0) * (ATT_HEAD_DIM ** -0.5 * LOG2E)), ("v", part(2)),
              ("z", part(4)), ("x", part(5)[:, :inner]), ("B", part(5)[:, inner:inner + gs]),
              ("C", part(5)[:, inner + gs:]), ("sc_b", part(7)[:, :sc3 // 3]),
              ("sc_c", part(7)[:, sc3 // 3:2 * sc3 // 3]), ("sc_h", part(7)[:, 2 * sc3 // 3:]),
              ("g_att", part(8)[:, :d]), ("g_ssd", part(8)[:, d:2 * d]), ("g_conv", part(8)[:, 2 * d:])]
    cols, off = {}, 0
    for name, w in pieces:
        cols[name] = off
        off += w.shape[1]
    w_main = jnp.concatenate([w for _, w in pieces], axis=1).astype(BF16)
    w_small = jnp.zeros((d, 2 * LANE), F32).at[:, :ATT_HEADS].set(part(3)).at[:, LANE:LANE + SSD_HEADS].set(part(6))
    ws_hi = w_small.astype(BF16)
    ws_lo = (w_small - ws_hi.astype(F32)).astype(BF16)
    return w_main, ws_hi, ws_lo, cols


def kernel(x, meta_tokens, ln_in_g, ln_in_b, w_in, b_forget, ssd_conv_w, ssd_conv_b, ssd_dt_bias, ssd_a_log,
           ssd_d, ssd_norm_w, sc_conv_w, w_proj_attn, w_proj_ssd, w_proj_conv, w_out, ln_mix_g, ln_mix_b,
           dense_w_gu, dense_w_down, router_w, router_b, moe_w_gu, moe_w_down, ln_ffn_g, ln_ffn_b):
    batch, seq, d = x.shape
    depth = w_in.shape[0]
    alpha = (2 * depth) ** 0.25
    pad = (-(N_META + seq)) % MXU_DIM
    lp = pad + N_META + seq
    tp = batch * lp
    tk = MXU_DIM
    tq = _divisor_tile(lp, 3 * MXU_DIM, MXU_DIM)

    ln_x = _ln_rows(x.reshape(batch * seq, d), ln_in_g, ln_in_b).reshape(batch, seq, d)
    ln_meta = _ln_rows(meta_tokens.astype(x.dtype), ln_in_g, ln_in_b)
    h = jnp.concatenate([jnp.zeros((batch, pad, d), F32),
                         jnp.broadcast_to(ln_meta[None], (batch, N_META, d)), ln_x], axis=1).reshape(tp, d)

    for layer in range(depth):
        w_main, ws_hi, ws_lo, cols = _prepare_in_proj(w_in[layer])
        proj, small = _in_proj(h, w_main, ws_hi, ws_lo)
        ka, q_t, v_t = _attn_prep(proj, small, cols, b_forget[layer], batch, lp, pad, tk)
        y_att_t = _attention(q_t, ka, v_t, batch, lp, pad, tq, tk)
        y_ssd = _ssd(proj, small, cols, ssd_conv_w[layer], ssd_conv_b[layer], ssd_dt_bias[layer],
                     ssd_a_log[layer], ssd_d[layer], ssd_norm_w[layer], batch, lp, pad)
        j = layer // 2
        router = None
        if layer % 2 == 1:
            rw = jnp.zeros((d, LANE), F32).at[:, :N_EXPERTS].set(router_w[j])
            rw_hi = rw.astype(BF16)
            rw_lo = (rw - rw_hi.astype(F32)).astype(BF16)
            router = (rw_hi, rw_lo, jnp.zeros((1, LANE), F32).at[0, :N_EXPERTS].set(router_b[j]))
        outs = _merge(y_att_t, y_ssd, proj, cols, h, w_proj_attn[layer].astype(BF16),
                      w_proj_ssd[layer].astype(BF16), w_proj_conv[layer].astype(BF16),
                      w_out[layer].astype(BF16), sc_conv_w[layer], ln_mix_g[layer], ln_mix_b[layer],
                      lp, pad, alpha, router)
        if layer % 2 == 0:
            h = _ffn(outs[0], dense_w_gu[j].astype(BF16), dense_w_down[j].astype(BF16),
                     ln_ffn_g[layer], ln_ffn_b[layer], lp, pad, alpha)
        else:
            h_mid, route_idx, route_w = outs
            h = _moe(h_mid, route_idx, route_w, moe_w_gu[j].astype(BF16), moe_w_down[j].astype(BF16),
                     ln_ffn_g[layer], ln_ffn_b[layer], batch, lp, pad, alpha)
    return h.reshape(batch, lp, d)[:, pad + N_META:]
```

```python
import functools

import jax
import jax.numpy as jnp
from jax import lax
from jax.experimental import pallas as pl
from jax.experimental.pallas import tpu as pltpu

F32, BF16, I32 = jnp.float32, jnp.bfloat16, jnp.int32

N_META = 16
ATT_HEADS = 8
ATT_HEAD_DIM = 64
SSD_HEADS = 16
SSD_HEAD_DIM = 64
SSD_GROUPS = 2
SSD_HPG = SSD_HEADS // SSD_GROUPS
SSD_STATE = 128
SSD_CONV = 4
SSD_CHUNK = 128
SC_CONV = 3
N_EXPERTS = 8
LN_EPS = 1e-5
RMS_EPS = 1e-5
NEG_INF = -1e30
LOG2E = 1.4426950408889634

LANE = 128
SUBLANE = 8
MXU_DIM = 256
VMEM_LIMIT = 56 * 1024 * 1024
TOKEN_TILE_ROWS = SUBLANE


def _cparams(*sem):
    return pltpu.CompilerParams(dimension_semantics=tuple(sem), vmem_limit_bytes=VMEM_LIMIT)


def _divisor_tile(n, target, quantum):
    best = None
    t = quantum
    while t <= min(n, target):
        if n % t == 0:
            best = t
        t += quantum
    assert best is not None, (n, target, quantum)
    return best


def _dot(a, b):
    return jnp.dot(a, b, preferred_element_type=F32)


def _dot_nt(a, b):
    return lax.dot_general(a, b, (((1,), (1,)), ((), ())), preferred_element_type=F32)


def _split2(x):
    hi = x.astype(BF16)
    lo = (x - hi.astype(F32)).astype(BF16)
    return hi, lo


def _split3(x):
    hi = x.astype(BF16)
    r = x - hi.astype(F32)
    mid = r.astype(BF16)
    lo = (r - mid.astype(F32)).astype(BF16)
    return hi, mid, lo


def _dot_x3(a_f32, w_hi, w_lo):
    a_hi, a_lo = _split2(a_f32)
    return _dot(a_hi, w_hi) + _dot(a_hi, w_lo) + _dot(a_lo, w_hi)


def _ln(v, g, b):
    mu = jnp.mean(v, axis=-1, keepdims=True)
    c = v - mu
    var = jnp.mean(c * c, axis=-1, keepdims=True)
    return c * lax.rsqrt(var + LN_EPS) * g + b


def _silu(v):
    return v * jax.nn.sigmoid(v)


def _softplus(v):
    return jnp.maximum(v, 0.0) + jnp.log(1.0 + jnp.exp(-jnp.abs(v)))


def _zero_pad_rows(v, tile_index, tm, tiles_per_seq, pad):
    row = (tile_index % tiles_per_seq) * tm + lax.broadcasted_iota(I32, (tm, 1), 0)
    return jnp.where(row >= pad, v, 0.0)


def _ln_kernel(x_ref, g_ref, b_ref, o_ref):
    o_ref[...] = _ln(x_ref[...].astype(F32), g_ref[...], b_ref[...])


def _ln_rows(x2d, g, b):
    n, d = x2d.shape
    tm = _divisor_tile(n, 1024, SUBLANE)
    return pl.pallas_call(
        _ln_kernel,
        grid=(n // tm,),
        in_specs=[pl.BlockSpec((tm, d), lambda i: (i, 0)),
                  pl.BlockSpec((1, d), lambda i: (0, 0)),
                  pl.BlockSpec((1, d), lambda i: (0, 0))],
        out_specs=pl.BlockSpec((tm, d), lambda i: (i, 0)),
        out_shape=jax.ShapeDtypeStruct((n, d), F32),
        compiler_params=_cparams("parallel"),
        name="ln_in",
    )(x2d, g.reshape(1, d), b.reshape(1, d))


def _in_proj_kernel(x_ref, w_ref, wsh_ref, wsl_ref, o_ref, os_ref, xb_ref):
    @pl.when(pl.program_id(1) == 0)
    def _():
        x = x_ref[...]
        xb_ref[...] = x.astype(BF16)
        os_ref[...] = _dot_x3(x, wsh_ref[...], wsl_ref[...])

    o_ref[...] = _dot(xb_ref[...], w_ref[...]).astype(BF16)


def _in_proj(h, w_main, ws_hi, ws_lo):
    tp, d = h.shape
    n = w_main.shape[1]
    ns = ws_hi.shape[1]
    tm = _divisor_tile(tp, 1280, LANE)
    tn = _divisor_tile(n, 1024, LANE)
    return pl.pallas_call(
        _in_proj_kernel,
        grid=(tp // tm, n // tn),
        in_specs=[pl.BlockSpec((tm, d), lambda i, j: (i, 0)),
                  pl.BlockSpec((d, tn), lambda i, j: (0, j)),
                  pl.BlockSpec((d, ns), lambda i, j: (0, 0)),
                  pl.BlockSpec((d, ns), lambda i, j: (0, 0))],
        out_specs=[pl.BlockSpec((tm, tn), lambda i, j: (i, j)),
                   pl.BlockSpec((tm, ns), lambda i, j: (i, 0))],
        out_shape=[jax.ShapeDtypeStruct((tp, n), BF16),
                   jax.ShapeDtypeStruct((tp, ns), F32)],
        scratch_shapes=[pltpu.VMEM((tm, d), BF16)],
        compiler_params=_cparams("parallel", "arbitrary"),
        name="in_proj",
    )(h, w_main, ws_hi, ws_lo)


ATT_BIAS_ROWS = 2 * SUBLANE
ATT_V_ROWS = ATT_HEAD_DIM + 2 * SUBLANE
ATT_BOUND_SLACK = (1.02, 1.0)


def _attn_prep_kernel(k_ref, q_ref, v_ref, f_ref, bf_ref, tri_ref, pk_ref, gk_ref, gq_ref,
                      ko_ref, qo_ref, vo_ref, carry_ref, *, tm, tiles_per_seq, pad):
    t = pl.program_id(0) % tiles_per_seq
    hd = ATT_HEAD_DIM
    hw = 2 * hd

    @pl.when(t == 0)
    def _():
        carry_ref[...] = jnp.zeros_like(carry_ref)

    x = f_ref[...] + bf_ref[...]
    log_f = -_softplus(-x) * LOG2E
    row = t * tm + lax.broadcasted_iota(I32, (tm, LANE), 0)
    lane = lax.broadcasted_iota(I32, (tm, LANE), 1)
    log_f = jnp.where((row >= pad) & (lane < ATT_HEADS), log_f, 0.0)
    tri = tri_ref[...]
    hi, mid, lo = _split3(log_f)
    c = _dot(tri, hi) + _dot(tri, mid) + _dot(tri, lo) + carry_ref[0:1, :]
    carry_ref[0:1, :] = c[tm - 1:tm, :]

    kf = k_ref[...].astype(F32)
    qf = q_ref[...].astype(F32)
    k_max2 = jnp.maximum(carry_ref[1:2, :], jnp.max(_dot((kf * kf).astype(BF16), gk_ref[...]), axis=0, keepdims=True))
    carry_ref[1:2, :] = k_max2
    q_norm2 = _dot((qf * qf).astype(BF16), gq_ref[...])
    m = jnp.sqrt(q_norm2 * k_max2) * ATT_BOUND_SLACK[0] + ATT_BOUND_SLACK[1]

    hi, mid, lo = _split3(c)
    packed = jnp.where(lane < 8, hi.astype(F32),
             jnp.where(lane < 16, pltpu.roll(mid.astype(F32), 8, 1),
             jnp.where(lane < 24, pltpu.roll(lo.astype(F32), 16, 1),
             jnp.where(lane == 24, 1.0, 0.0)))).astype(BF16)
    ko_ref[...] = (kf + _dot(packed, pk_ref[...])).astype(BF16)

    q_t = qf.T
    c_hi, c_mid, c_lo = (p.astype(F32) for p in _split3(c.T))
    m_hi, m_mid, m_lo = (p.astype(F32) for p in _split3(m.T))
    sub = lax.broadcasted_iota(I32, (SUBLANE, tm), 0)
    for h in range(ATT_HEADS):
        bias_a = jnp.where(sub == 0, c_hi[h:h + 1, :],
                 jnp.where(sub == 1, c_mid[h:h + 1, :],
                 jnp.where(sub == 2, c_lo[h:h + 1, :],
                 jnp.where(sub < 6, 1.0,
                 jnp.where(sub == 6, -m_hi[h:h + 1, :], -m_mid[h:h + 1, :])))))
        bias_b = jnp.where(sub == 0, -m_lo[h:h + 1, :], 0.0)
        blk = jnp.concatenate([q_t[h * hd:(h + 1) * hd, :], bias_a, bias_b,
                               jnp.zeros((hw - hd - ATT_BIAS_ROWS, tm), F32)], axis=0)
        qo_ref[h * hw:(h + 1) * hw, :] = blk.astype(BF16)

    v_t = v_ref[...].astype(F32).T
    ones_row = jnp.where(sub == 0, 1.0, 0.0)
    for h in range(ATT_HEADS):
        blk = jnp.concatenate([v_t[h * hd:(h + 1) * hd, :], ones_row, jnp.zeros((SUBLANE, tm), F32)], axis=0)
        vo_ref[h, 0] = blk.astype(BF16)


def _key_bias_placement():
    hw = 2 * ATT_HEAD_DIM
    rows, cols, vals = [], [], []
    for h in range(ATT_HEADS):
        base = h * hw + ATT_HEAD_DIM
        for piece in range(3):
            rows.append(24); cols.append(base + piece); vals.append(1.0)
            rows.append(8 * piece + h); cols.append(base + 3 + piece); vals.append(-1.0)
            rows.append(24); cols.append(base + 6 + piece); vals.append(1.0)
    pk = jnp.zeros((LANE, ATT_HEADS * hw), F32).at[jnp.array(rows), jnp.array(cols)].set(jnp.array(vals))
    return pk.astype(BF16)


def _head_grouping(width, per_head):
    return (jnp.arange(width)[:, None] // per_head == jnp.arange(LANE)[None, :]).astype(BF16)


def _attn_prep(proj, small, cols, b_forget, batch, lp, pad, tk):
    tp = proj.shape[0]
    hw = 2 * ATT_HEAD_DIM
    kw = ATT_HEADS * hw
    aw = ATT_HEADS * ATT_HEAD_DIM
    tm = tk
    tiles_per_seq = lp // tm
    tri = jnp.tril(jnp.ones((tm, tm), F32)).astype(BF16)
    bf = jnp.zeros((1, LANE), F32).at[0, :ATT_HEADS].set(b_forget)
    kern = functools.partial(_attn_prep_kernel, tm=tm, tiles_per_seq=tiles_per_seq, pad=pad)
    return pl.pallas_call(
        kern,
        grid=(tp // tm,),
        in_specs=[pl.BlockSpec((tm, kw), lambda i: (i, cols["k"] // kw)),
                  pl.BlockSpec((tm, aw), lambda i: (i, cols["q"] // aw)),
                  pl.BlockSpec((tm, aw), lambda i: (i, cols["v"] // aw)),
                  pl.BlockSpec((tm, LANE), lambda i: (i, 0)),
                  pl.BlockSpec((1, LANE), lambda i: (0, 0)),
                  pl.BlockSpec((tm, tm), lambda i: (0, 0)),
                  pl.BlockSpec((LANE, kw), lambda i: (0, 0)),
                  pl.BlockSpec((kw, LANE), lambda i: (0, 0)),
                  pl.BlockSpec((aw, LANE), lambda i: (0, 0))],
        out_specs=[pl.BlockSpec((tm, kw), lambda i: (i, 0)),
                   pl.BlockSpec((kw, tm), lambda i: (i // tiles_per_seq, i % tiles_per_seq)),
                   pl.BlockSpec((ATT_HEADS, 1, ATT_V_ROWS, tm),
                                lambda i: (i // tiles_per_seq, i % tiles_per_seq, 0, 0))],
        out_shape=[jax.ShapeDtypeStruct((tp, kw), BF16),
                   jax.ShapeDtypeStruct((batch * kw, lp), BF16),
                   jax.ShapeDtypeStruct((batch * ATT_HEADS, tiles_per_seq, ATT_V_ROWS, tm), BF16)],
        scratch_shapes=[pltpu.VMEM((SUBLANE, LANE), F32)],
        compiler_params=_cparams("arbitrary"),
        name="attn_prep",
    )(proj, proj, proj, small, bf, tri, _key_bias_placement(), _head_grouping(kw, hw),
      _head_grouping(aw, ATT_HEAD_DIM))


ATT_HEADS_PER_STEP = 4
ATT_MIN_ROW_SUM = 2.0 ** -80


def _attn_kernel(qt_ref, k_ref, vt_ref, o_ref, m_ref, acc_ref, *, tq, tk, pad, heads):
    i = pl.program_id(2)
    r = tq // tk
    hd = ATT_HEAD_DIM
    hw = 2 * hd
    vr = ATT_V_ROWS

    def scores(j, hh, masked):
        k = k_ref[pl.ds(pl.multiple_of(j * tk, tk), tk), hh * hw:(hh + 1) * hw]
        s = _dot(k, qt_ref[hh * hw:(hh + 1) * hw, :])
        if masked:
            kpos = j * tk + lax.broadcasted_iota(I32, (tk, tq), 0)
            qpos = i * tq + lax.broadcasted_iota(I32, (tk, tq), 1)
            s = jnp.where((kpos <= qpos) & (kpos >= pad), s, NEG_INF)
        return s

    def fast_step(j, masked):
        s_next = scores(j, 0, masked)
        for hh in range(heads):
            s = s_next
            if hh + 1 < heads:
                s_next = scores(j, hh + 1, masked)
            p = jnp.exp2(s).astype(BF16)
            acc_ref[hh * vr:(hh + 1) * vr, :] += _dot(vt_ref[hh, j], p)

    def exact_step(j, masked):
        for hh in range(heads):
            s = scores(j, hh, masked)
            m_old = m_ref[hh:hh + 1, :]
            m_new = jnp.maximum(m_old, jnp.max(s, axis=0, keepdims=True))
            p = jnp.exp2(s - m_new).astype(BF16)
            acc_ref[hh * vr:(hh + 1) * vr, :] = (jnp.exp2(m_old - m_new) * acc_ref[hh * vr:(hh + 1) * vr, :]
                                                 + _dot(vt_ref[hh, j], p))
            m_ref[hh:hh + 1, :] = m_new

    def sweep(step):
        def masked_body(j, c):
            step(j, True)
            return c

        def plain_body(j, c):
            step(j, False)
            return c

        acc_ref[...] = jnp.zeros_like(acc_ref)
        step(0, True)
        lax.fori_loop(1, r * i, plain_body, 0)
        lax.fori_loop(jnp.maximum(r * i, 1), r * i + r, masked_body, 0)

    sweep(fast_step)
    real = i * tq + lax.broadcasted_iota(I32, (1, tq), 1) >= pad
    smallest = jnp.min(jnp.concatenate(
        [jnp.where(real, acc_ref[hh * vr + hd:hh * vr + hd + 1, :], 1.0) for hh in range(heads)], axis=0))

    @pl.when(smallest < ATT_MIN_ROW_SUM)
    def _():
        m_ref[...] = jnp.full_like(m_ref, NEG_INF)
        sweep(exact_step)

    for hh in range(heads):
        l = acc_ref[hh * vr + hd:hh * vr + hd + 1, :]
        l = jnp.where(l > 0.0, l, 1.0)
        o_ref[hh * hd:(hh + 1) * hd, :] = (acc_ref[hh * vr:hh * vr + hd, :] / l).astype(BF16)


def _attention(q_t, ka, v_t, batch, lp, pad, tq, tk):
    hd = ATT_HEAD_DIM
    hw = 2 * hd
    nh = ATT_HEADS_PER_STEP
    groups = ATT_HEADS // nh
    nq = lp // tq
    nkv = lp // tk
    kern = functools.partial(_attn_kernel, tq=tq, tk=tk, pad=pad, heads=nh)
    return pl.pallas_call(
        kern,
        grid=(batch, groups, nq),
        in_specs=[pl.BlockSpec((nh * hw, tq), lambda b, g, i: (b * groups + g, i)),
                  pl.BlockSpec((lp, nh * hw), lambda b, g, i: (b, g)),
                  pl.BlockSpec((nh, nkv, ATT_V_ROWS, tk), lambda b, g, i: (b * groups + g, 0, 0, 0))],
        out_specs=pl.BlockSpec((nh * hd, tq), lambda b, g, i: (b * groups + g, i)),
        out_shape=jax.ShapeDtypeStruct((batch * ATT_HEADS * hd, lp), BF16),
        scratch_shapes=[pltpu.VMEM((nh, tq), F32), pltpu.VMEM((nh * ATT_V_ROWS, tq), F32)],
        compiler_params=_cparams("parallel", "parallel", "arbitrary"),
        name="fox_attention",
    )(q_t, ka, v_t)


def _ssd_kernel(x_ref, b_ref, c_ref, z_ref, dt_ref, cw_ref, cb_ref, dtb_ref, alog_ref, dexp_ref, nw_ref,
                tri_ref, e_ref, o_ref, ubuf_ref, state_ref, ybuf_ref, *, pad):
    chunk = pl.program_id(1)
    blk = SSD_CHUNK
    inner = SSD_HEADS * SSD_HEAD_DIM
    gs = SSD_GROUPS * SSD_STATE
    gw = SSD_HPG * SSD_HEAD_DIM

    @pl.when(chunk == 0)
    def _():
        ubuf_ref[0:SUBLANE, :] = jnp.zeros((SUBLANE, ubuf_ref.shape[1]), F32)
        state_ref[...] = jnp.zeros_like(state_ref)

    ubuf_ref[SUBLANE:SUBLANE + blk, 0:inner] = x_ref[...].astype(F32)
    ubuf_ref[SUBLANE:SUBLANE + blk, inner:inner + gs] = b_ref[...].astype(F32)
    ubuf_ref[SUBLANE:SUBLANE + blk, inner + gs:inner + 2 * gs] = c_ref[...].astype(F32)
    conv = cb_ref[...]
    for k in range(SSD_CONV):
        conv = conv + cw_ref[k:k + 1, :] * ubuf_ref[pl.ds(SUBLANE - (SSD_CONV - 1) + k, blk), :]
    ubuf_ref[0:SUBLANE, :] = ubuf_ref[blk:blk + SUBLANE, :]
    xc = _silu(conv)
    xs = xc[:, 0:inner]
    bm = xc[:, inner:inner + gs]
    cm = xc[:, inner + gs:inner + 2 * gs]

    row = lax.broadcasted_iota(I32, (blk, LANE), 0)
    lane = lax.broadcasted_iota(I32, (blk, LANE), 1)
    dt = _softplus(dt_ref[...] + dtb_ref[...])
    dt = jnp.where((lane < SSD_HEADS) & (chunk * blk + row >= pad), dt, 0.0)
    a = -jnp.exp(alog_ref[...])
    tri = tri_ref[...]
    hi, mid, lo = _split3(dt * a)
    a_cum = _dot(tri, hi) + _dot(tri, mid) + _dot(tri, lo)
    a_last = a_cum[blk - 1:blk, :]
    ea = jnp.exp(a_cum)
    wgt = jnp.exp(a_last - a_cum) * dt

    def pieces(v, at):
        v_hi = v.astype(BF16).astype(F32)
        v_lo = v - v_hi
        return (pltpu.roll(v_hi, at, 1) if at else v_hi), pltpu.roll(v_lo, at + SSD_HEADS, 1)

    dt_hi, dt_lo = pieces(dt, 0)
    ea_hi, ea_lo = pieces(ea, 2 * SSD_HEADS)
    wg_hi, wg_lo = pieces(wgt, 4 * SSD_HEADS)
    h16 = SSD_HEADS
    packed = jnp.where(lane < h16, dt_hi,
             jnp.where(lane < 2 * h16, dt_lo,
             jnp.where(lane < 3 * h16, ea_hi,
             jnp.where(lane < 4 * h16, ea_lo,
             jnp.where(lane < 5 * h16, wg_hi,
             jnp.where(lane < 6 * h16, wg_lo, 0.0)))))).astype(BF16)
    expanded = _dot(packed, e_ref[...])
    dt_exp = expanded[:, 0:inner]
    ea_exp = expanded[:, inner:2 * inner]
    wg_exp = expanded[:, 2 * inner:3 * inner]
    xdt = (xs * dt_exp).astype(BF16)
    xw = (xs * wg_exp).astype(BF16)

    a_cum_t = a_cum.T
    causal = row >= lane
    for g in range(SSD_GROUPS):
        cg = cm[:, g * SSD_STATE:(g + 1) * SSD_STATE].astype(BF16)
        bg_f32 = bm[:, g * SSD_STATE:(g + 1) * SSD_STATE]
        cbg = _dot_nt(cg, bg_f32.astype(BF16))
        mats = []
        for hh in range(SSD_HPG):
            h = g * SSD_HPG + hh
            diff = a_cum[:, h:h + 1] - a_cum_t[h:h + 1, :]
            decay = jnp.exp(jnp.where(causal, diff, NEG_INF))
            mats.append((cbg * decay).astype(BF16))
        for j in range(SSD_HPG // 2):
            col = g * gw + j * LANE
            xp = xdt[:, col:col + LANE]
            y0 = _dot(mats[2 * j], xp)
            y1 = _dot(mats[2 * j + 1], xp)
            ybuf_ref[:, col:col + LANE] = jnp.where(lane < SSD_HEAD_DIM, y0, y1)
        st = state_ref[g]
        y_off = _dot(cg, st.astype(BF16))
        ybuf_ref[:, g * gw:(g + 1) * gw] += y_off * ea_exp[:, g * gw:(g + 1) * gw]
        chunk_decay = ea_exp[blk - 1:blk, g * gw:(g + 1) * gw]
        state_ref[g] = st * chunk_decay + _dot(bg_f32.T.astype(BF16), xw[:, g * gw:(g + 1) * gw])

    y = ybuf_ref[...] + dexp_ref[...] * xs
    u = y * _silu(z_ref[...].astype(F32))
    for g in range(SSD_GROUPS):
        ug = u[:, g * gw:(g + 1) * gw]
        ug = ug * lax.rsqrt(jnp.mean(ug * ug, axis=-1, keepdims=True) + RMS_EPS)
        o_ref[:, g * gw:(g + 1) * gw] = (ug * nw_ref[:, g * gw:(g + 1) * gw]).astype(BF16)


def _ssd_expand_matrix():
    inner = SSD_HEADS * SSD_HEAD_DIM
    rows = jnp.arange(LANE)
    cols = jnp.arange(3 * inner)
    seg = cols // inner
    head = (cols % inner) // SSD_HEAD_DIM
    hit = (rows[:, None] // (2 * SSD_HEADS) == seg[None, :]) & (rows[:, None] % SSD_HEADS == head[None, :]) \
          & (rows[:, None] < 6 * SSD_HEADS)
    return hit.astype(BF16)


def _ssd(proj, small, cols, conv_w, conv_b, dt_bias, a_log, d, norm_w, batch, lp, pad):
    tp = proj.shape[0]
    blk = SSD_CHUNK
    inner = SSD_HEADS * SSD_HEAD_DIM
    gs = SSD_GROUPS * SSD_STATE
    nc = lp // blk
    cdim = inner + 2 * gs
    cw = jnp.zeros((SUBLANE, cdim), F32).at[:SSD_CONV].set(conv_w)
    pad_lane = lambda v: jnp.zeros((1, LANE), F32).at[0, :v.shape[0]].set(v)
    dexp = jnp.repeat(d, SSD_HEAD_DIM).reshape(1, inner)
    tri = jnp.tril(jnp.ones((blk, blk), F32)).astype(BF16)
    rowblk = lambda b, c: b * nc + c
    const = lambda shape: pl.BlockSpec(shape, lambda b, c: (0,) * len(shape))
    return pl.pallas_call(
        functools.partial(_ssd_kernel, pad=pad),
        grid=(batch, nc),
        in_specs=[pl.BlockSpec((blk, inner), lambda b, c: (rowblk(b, c), cols["x"] // inner)),
                  pl.BlockSpec((blk, gs), lambda b, c: (rowblk(b, c), cols["B"] // gs)),
                  pl.BlockSpec((blk, gs), lambda b, c: (rowblk(b, c), cols["C"] // gs)),
                  pl.BlockSpec((blk, inner), lambda b, c: (rowblk(b, c), cols["z"] // inner)),
                  pl.BlockSpec((blk, LANE), lambda b, c: (rowblk(b, c), 1)),
                  const((SUBLANE, cdim)), const((1, cdim)), const((1, LANE)), const((1, LANE)),
                  const((1, inner)), const((1, inner)), const((blk, blk)), const((LANE, 3 * inner))],
        out_specs=pl.BlockSpec((blk, inner), lambda b, c: (rowblk(b, c), 0)),
        out_shape=jax.ShapeDtypeStruct((tp, inner), BF16),
        scratch_shapes=[pltpu.VMEM((blk + SUBLANE, cdim), F32),
                        pltpu.VMEM((SSD_GROUPS, SSD_STATE, SSD_HPG * SSD_HEAD_DIM), F32),
                        pltpu.VMEM((blk, inner), F32)],
        compiler_params=_cparams("parallel", "arbitrary"),
        name="ssd",
    )(proj, proj, proj, proj, small, cw, conv_b.reshape(1, cdim), pad_lane(dt_bias), pad_lane(a_log),
      dexp, norm_w.reshape(1, inner), tri, _ssd_expand_matrix())


def _merge_kernel(yat_ref, ys_ref, sb_ref, sc_ref, sh_ref, sch_ref, shh_ref, ga_ref, gs_ref, gc_ref, h_ref,
                  wpa_ref, wps_ref, wpc_ref, wo_ref, scw_ref, g_ref, b_ref, *rest,
                  tm, tiles_per_seq, pad, alpha, with_router):
    if with_router:
        rwh_ref, rwl_ref, rb_ref, o_ref, ri_ref, rw_ref, vbuf_ref = rest
    else:
        o_ref, vbuf_ref = rest
    vbuf_ref[0:SUBLANE, :] = sch_ref[...].astype(F32) * shh_ref[...].astype(F32)
    vbuf_ref[SUBLANE:SUBLANE + tm, :] = sc_ref[...].astype(F32) * sh_ref[...].astype(F32)
    conv = jnp.zeros((tm, vbuf_ref.shape[1]), F32)
    for k in range(SC_CONV):
        conv = conv + scw_ref[k:k + 1, :] * vbuf_ref[pl.ds(SUBLANE - (SC_CONV - 1) + k, tm), :]
    y_conv = (sb_ref[...].astype(F32) * conv).astype(BF16)
    y_att = yat_ref[...].astype(F32).T.astype(BF16)

    merged = jax.nn.sigmoid(ga_ref[...].astype(F32)) * _dot(y_att, wpa_ref[...])
    merged += jax.nn.sigmoid(gs_ref[...].astype(F32)) * _dot(ys_ref[...], wps_ref[...])
    merged += jax.nn.sigmoid(gc_ref[...].astype(F32)) * _dot(y_conv, wpc_ref[...])
    mix = _dot(merged.astype(BF16), wo_ref[...])
    hn = _ln(alpha * h_ref[...] + mix, g_ref[...], b_ref[...])
    hn = _zero_pad_rows(hn, pl.program_id(0), tm, tiles_per_seq, pad)
    o_ref[...] = hn

    if with_router:
        lane = lax.broadcasted_iota(I32, (tm, LANE), 1)
        lane_f = lane.astype(F32)
        logits = _dot_x3(hn, rwh_ref[...], rwl_ref[...]) + rb_ref[...]
        lg = jnp.where(lane < N_EXPERTS, logits, -jnp.inf)
        v1 = jnp.max(lg, axis=1, keepdims=True)
        i1 = jnp.min(jnp.where(lg == v1, lane_f, float(LANE)), axis=1, keepdims=True).astype(I32)
        lg2 = jnp.where(lane == i1, -jnp.inf, lg)
        v2 = jnp.max(lg2, axis=1, keepdims=True)
        i2 = jnp.min(jnp.where(lg2 == v2, lane_f, float(LANE)), axis=1, keepdims=True).astype(I32)
        e = jnp.exp(v2 - v1)
        w1 = 1.0 / (1.0 + e)
        w2 = e / (1.0 + e)
        ri_ref[...] = jnp.where(lane == 0, i1, jnp.where(lane == 1, i2, 0))
        rw_ref[...] = jnp.where(lane == 0, w1, jnp.where(lane == 1, w2, 0.0))


def _merge(y_att_t, y_ssd, proj, cols, h, wpa, wps, wpc, wo, sc_w, g, b, lp, pad, alpha, router=None):
    tp, d = h.shape
    scw = wpc.shape[0]
    aw = wpa.shape[0]
    tm = _divisor_tile(lp, 768, LANE)
    tiles_per_seq = lp // tm
    nt = tp // tm
    halo = lambda i: jnp.maximum(i * (tm // SUBLANE) - 1, 0)
    const = lambda shape: pl.BlockSpec(shape, lambda i: (0,) * len(shape))
    in_specs = [pl.BlockSpec((aw, tm), lambda i: (i // tiles_per_seq, i % tiles_per_seq)),
                pl.BlockSpec((tm, y_ssd.shape[1]), lambda i: (i, 0)),
                pl.BlockSpec((tm, scw), lambda i: (i, cols["sc_b"] // scw)),
                pl.BlockSpec((tm, scw), lambda i: (i, cols["sc_c"] // scw)),
                pl.BlockSpec((tm, scw), lambda i: (i, cols["sc_h"] // scw)),
                pl.BlockSpec((SUBLANE, scw), lambda i: (halo(i), cols["sc_c"] // scw)),
                pl.BlockSpec((SUBLANE, scw), lambda i: (halo(i), cols["sc_h"] // scw)),
                pl.BlockSpec((tm, d), lambda i: (i, cols["g_att"] // d)),
                pl.BlockSpec((tm, d), lambda i: (i, cols["g_ssd"] // d)),
                pl.BlockSpec((tm, d), lambda i: (i, cols["g_conv"] // d)),
                pl.BlockSpec((tm, d), lambda i: (i, 0)),
                const(wpa.shape), const(wps.shape), const(wpc.shape), const(wo.shape),
                const((SUBLANE, scw)), const((1, d)), const((1, d))]
    args = [y_att_t, y_ssd, proj, proj, proj, proj, proj, proj, proj, proj, h, wpa, wps, wpc, wo,
            jnp.zeros((SUBLANE, scw), F32).at[:SC_CONV].set(sc_w), g.reshape(1, d), b.reshape(1, d)]
    out_specs = [pl.BlockSpec((tm, d), lambda i: (i, 0))]
    out_shape = [jax.ShapeDtypeStruct((tp, d), F32)]
    if router is not None:
        rw_hi, rw_lo, rb = router
        in_specs += [const(rw_hi.shape), const(rw_lo.shape), const((1, LANE))]
        args += [rw_hi, rw_lo, rb]
        out_specs += [pl.BlockSpec((tm, LANE), lambda i: (i, 0)), pl.BlockSpec((tm, LANE), lambda i: (i, 0))]
        out_shape += [jax.ShapeDtypeStruct((tp, LANE), I32), jax.ShapeDtypeStruct((tp, LANE), F32)]
    kern = functools.partial(_merge_kernel, tm=tm, tiles_per_seq=tiles_per_seq, pad=pad, alpha=alpha,
                             with_router=router is not None)
    return pl.pallas_call(
        kern, grid=(nt,), in_specs=in_specs, out_specs=out_specs, out_shape=out_shape,
        scratch_shapes=[pltpu.VMEM((tm + SUBLANE, scw), F32)],
        compiler_params=_cparams("parallel"),
        name="merge_out",
    )(*args)


def _ffn_kernel(h_ref, wg_ref, wu_ref, wd_ref, g_ref, b_ref, o_ref, *, tm, tf, tiles_per_seq, pad, alpha):
    h = h_ref[...]
    xb = h.astype(BF16)
    acc = jnp.zeros_like(h)
    for c in range(wd_ref.shape[0] // tf):
        cols = slice(c * tf, (c + 1) * tf)
        act = (_silu(_dot(xb, wg_ref[:, cols])) * _dot(xb, wu_ref[:, cols])).astype(BF16)
        acc = acc + _dot(act, wd_ref[cols, :])
    hn = _ln(alpha * h + acc, g_ref[...], b_ref[...])
    o_ref[...] = _zero_pad_rows(hn, pl.program_id(0), tm, tiles_per_seq, pad)


def _ffn(h, w_gu, w_down, g, b, lp, pad, alpha):
    tp, d = h.shape
    ff = w_down.shape[0]
    tm = _divisor_tile(lp, 768, LANE)
    tf = _divisor_tile(ff, 2 * MXU_DIM, MXU_DIM)
    kern = functools.partial(_ffn_kernel, tm=tm, tf=tf, tiles_per_seq=lp // tm, pad=pad, alpha=alpha)
    resident = dict(pipeline_mode=pl.Buffered(1))
    return pl.pallas_call(
        kern,
        grid=(tp // tm,),
        in_specs=[pl.BlockSpec((tm, d), lambda i: (i, 0)),
                  pl.BlockSpec((d, ff), lambda i: (0, 0), **resident),
                  pl.BlockSpec((d, ff), lambda i: (0, 1), **resident),
                  pl.BlockSpec((ff, d), lambda i: (0, 0), **resident),
                  pl.BlockSpec((1, d), lambda i: (0, 0)),
                  pl.BlockSpec((1, d), lambda i: (0, 0))],
        out_specs=pl.BlockSpec((tm, d), lambda i: (i, 0)),
        out_shape=jax.ShapeDtypeStruct((tp, d), F32),
        compiler_params=_cparams("parallel"),
        name="dense_ffn",
    )(h, w_gu, w_gu, w_down, g.reshape(1, d), b.reshape(1, d))


def _moe_scatter_kernel(slot_ref, h_ref, init_ref, o_ref, buf_ref, sem, *, tm):
    del init_ref
    for s in range(TOKEN_TILE_ROWS):
        buf_ref[pl.ds(s, tm, stride=TOKEN_TILE_ROWS), :] = h_ref[:, s * LANE:(s + 1) * LANE]

    def issue(r, c):
        src = buf_ref.at[pl.ds(pl.multiple_of(r * TOKEN_TILE_ROWS, TOKEN_TILE_ROWS), TOKEN_TILE_ROWS)]
        for k in range(2):
            slot = slot_ref[0, 0, 2 * r + k]
            dst = o_ref.at[pl.ds(pl.multiple_of(slot * TOKEN_TILE_ROWS, TOKEN_TILE_ROWS), TOKEN_TILE_ROWS)]
            pltpu.make_async_copy(src, dst, sem).start()
        return c

    lax.fori_loop(0, tm, issue, 0)
    for _ in range(2):
        pltpu.make_async_copy(buf_ref, o_ref.at[pl.ds(0, tm * TOKEN_TILE_ROWS)], sem).wait()


def _moe_scatter(h, slots, n_rows, lp):
    tp, d = h.shape
    assert d == TOKEN_TILE_ROWS * LANE
    tm = _divisor_tile(lp, 768, LANE)
    nt = tp // tm
    init = jnp.zeros((n_rows * TOKEN_TILE_ROWS, LANE), F32)
    return pl.pallas_call(
        functools.partial(_moe_scatter_kernel, tm=tm),
        grid=(nt,),
        in_specs=[pl.BlockSpec((1, 1, 2 * tm), lambda i: (i, 0, 0), memory_space=pltpu.SMEM),
                  pl.BlockSpec((tm, d), lambda i: (i, 0)),
                  pl.BlockSpec(memory_space=pl.ANY)],
        out_specs=pl.BlockSpec(memory_space=pl.ANY),
        out_shape=jax.ShapeDtypeStruct(init.shape, F32),
        scratch_shapes=[pltpu.VMEM((tm * TOKEN_TILE_ROWS, LANE), F32), pltpu.SemaphoreType.DMA(())],
        input_output_aliases={2: 0},
        compiler_params=_cparams("arbitrary"),
        name="moe_scatter",
    )(slots.reshape(nt, 1, 2 * tm), h, init)


def _moe_kernel(te_ref, tv_ref, x_ref, wg_ref, wu_ref, wd_ref, o_ref, xb_ref, acc_ref, *, tm):
    n = pl.program_id(0)
    f = pl.program_id(1)

    @pl.when(f == 0)
    def _():
        for s in range(TOKEN_TILE_ROWS):
            xb_ref[:, s * LANE:(s + 1) * LANE] = x_ref[pl.ds(s, tm, stride=TOKEN_TILE_ROWS), :].astype(BF16)
        acc_ref[...] = jnp.zeros_like(acc_ref)

    @pl.when(tv_ref[n] == 1)
    def _():
        xb = xb_ref[...]
        act = (_silu(_dot(xb, wg_ref[0])) * _dot(xb, wu_ref[0])).astype(BF16)
        acc_ref[...] += _dot(act, wd_ref[0])

    @pl.when(f == pl.num_programs(1) - 1)
    def _():
        for s in range(TOKEN_TILE_ROWS):
            o_ref[pl.ds(s, tm, stride=TOKEN_TILE_ROWS), :] = acc_ref[:, s * LANE:(s + 1) * LANE]


def _moe_experts(x_sorted, tile_expert, tile_valid, w_gu, w_down, tm, n_tiles):
    d = w_down.shape[2]
    ff = w_down.shape[1]
    tf = _divisor_tile(ff, 7 * MXU_DIM, MXU_DIM)
    nf = ff // tf
    fblk = lambda f, n, tv: f * tv[n] + (nf - 1) * (1 - tv[n])
    rows = tm * TOKEN_TILE_ROWS
    grid_spec = pltpu.PrefetchScalarGridSpec(
        num_scalar_prefetch=2,
        grid=(n_tiles, nf),
        in_specs=[pl.BlockSpec((rows, LANE), lambda n, f, te, tv: (n, 0)),
                  pl.BlockSpec((1, d, tf), lambda n, f, te, tv: (te[n], 0, fblk(f, n, tv))),
                  pl.BlockSpec((1, d, tf), lambda n, f, te, tv: (te[n], 0, nf + fblk(f, n, tv))),
                  pl.BlockSpec((1, tf, d), lambda n, f, te, tv: (te[n], fblk(f, n, tv), 0))],
        out_specs=pl.BlockSpec((rows, LANE), lambda n, f, te, tv: (n, 0)),
        scratch_shapes=[pltpu.VMEM((tm, d), BF16), pltpu.VMEM((tm, d), F32)])
    return pl.pallas_call(
        functools.partial(_moe_kernel, tm=tm),
        grid_spec=grid_spec,
        out_shape=jax.ShapeDtypeStruct((n_tiles * rows, LANE), F32),
        compiler_params=_cparams("parallel", "arbitrary"),
        name="moe_experts",
    )(tile_expert, tile_valid, x_sorted, w_gu, w_gu, w_down)


def _moe_combine_kernel(slot_ref, y_ref, rw_ref, h_ref, g_ref, b_ref, o_ref, gbuf_ref, sem,
                        *, tm, tiles_per_seq, pad, alpha):
    def issue(r, c):
        for k in range(2):
            slot = slot_ref[0, 0, 2 * r + k]
            src = y_ref.at[pl.ds(pl.multiple_of(slot * TOKEN_TILE_ROWS, TOKEN_TILE_ROWS), TOKEN_TILE_ROWS)]
            dst = gbuf_ref.at[pl.ds(pl.multiple_of((k * tm + r) * TOKEN_TILE_ROWS, TOKEN_TILE_ROWS),
                                    TOKEN_TILE_ROWS)]
            pltpu.make_async_copy(src, dst, sem).start()
        return c

    lax.fori_loop(0, tm, issue, 0)
    pltpu.make_async_copy(y_ref.at[pl.ds(0, 2 * tm * TOKEN_TILE_ROWS)], gbuf_ref, sem).wait()

    rw = rw_ref[...]
    w0, w1 = rw[:, 0:1], rw[:, 1:2]
    ff = jnp.concatenate(
        [w0 * gbuf_ref[pl.ds(s, tm, stride=TOKEN_TILE_ROWS), :]
         + w1 * gbuf_ref[pl.ds(tm * TOKEN_TILE_ROWS + s, tm, stride=TOKEN_TILE_ROWS), :]
         for s in range(TOKEN_TILE_ROWS)], axis=1)
    hn = _ln(alpha * h_ref[...] + ff, g_ref[...], b_ref[...])
    o_ref[...] = _zero_pad_rows(hn, pl.program_id(0), tm, tiles_per_seq, pad)


def _moe_combine(y_sorted, slots, route_w, h, g, b, lp, pad, alpha):
    tp, d = h.shape
    tm = _divisor_tile(lp, 768, LANE)
    nt = tp // tm
    kern = functools.partial(_moe_combine_kernel, tm=tm, tiles_per_seq=lp // tm, pad=pad, alpha=alpha)
    return pl.pallas_call(
        kern,
        grid=(nt,),
        in_specs=[pl.BlockSpec((1, 1, 2 * tm), lambda i: (i, 0, 0), memory_space=pltpu.SMEM),
                  pl.BlockSpec(memory_space=pl.ANY),
                  pl.BlockSpec((tm, LANE), lambda i: (i, 0)),
                  pl.BlockSpec((tm, d), lambda i: (i, 0)),
                  pl.BlockSpec((1, d), lambda i: (0, 0)),
                  pl.BlockSpec((1, d), lambda i: (0, 0))],
        out_specs=pl.BlockSpec((tm, d), lambda i: (i, 0)),
        out_shape=jax.ShapeDtypeStruct((tp, d), F32),
        scratch_shapes=[pltpu.VMEM((2 * tm * TOKEN_TILE_ROWS, LANE), F32), pltpu.SemaphoreType.DMA(())],
        compiler_params=_cparams("arbitrary"),
        name="moe_combine",
    )(slots.reshape(nt, 1, 2 * tm), y_sorted, route_w, h, g.reshape(1, d), b.reshape(1, d))


def _moe(h, route_idx, route_w, w_gu, w_down, g, b, batch, lp, pad, alpha):
    tp, d = h.shape
    n_real = batch * (lp - pad)
    tm = 640
    experts = route_idx[:, :2].reshape(tp * 2)
    seq_row = jnp.arange(tp, dtype=I32) % lp
    real = jnp.repeat(seq_row >= pad, 2)
    onehot = ((experts[:, None] == jnp.arange(N_EXPERTS, dtype=I32)[None, :]) & real[:, None]).astype(I32)
    csum = jnp.cumsum(onehot, axis=0)
    counts = csum[-1]
    rank = jnp.sum(onehot * csum, axis=1) - 1
    tiles = (counts + tm - 1) // tm
    tile_end = jnp.cumsum(tiles)
    slot = ((tile_end - tiles) * tm)[experts] + rank
    n_tiles = (2 * n_real) // tm + N_EXPERTS
    tile_ids = jnp.arange(n_tiles, dtype=I32)
    tile_expert = jnp.minimum(jnp.sum((tile_ids[:, None] >= tile_end[None, :]).astype(I32), axis=1),
                              N_EXPERTS - 1).astype(I32)
    tile_valid = (tile_ids < tile_end[-1]).astype(I32)
    n_pad = batch * pad * 2
    spare_tiles = -(-n_pad // tm)
    pad_id = jnp.repeat((jnp.arange(tp, dtype=I32) // lp) * pad + seq_row, 2) * 2 + jnp.tile(jnp.arange(2, dtype=I32), tp)
    slot_scatter = jnp.where(real, slot, n_tiles * tm + pad_id).astype(I32)
    slot_gather = jnp.where(real, slot, 0).astype(I32)

    x_sorted = _moe_scatter(h, slot_scatter, (n_tiles + spare_tiles) * tm, lp)
    y_sorted = _moe_experts(x_sorted, tile_expert, tile_valid, w_gu, w_down, tm, n_tiles)
    return _moe_combine(y_sorted, slot_gather, route_w, h, g, b, lp, pad, alpha)


def _prepare_in_proj(w_in):
    d = w_in.shape[0]
    aw = ATT_HEADS * ATT_HEAD_DIM
    inner = SSD_HEADS * SSD_HEAD_DIM
    gs = SSD_GROUPS * SSD_STATE
    sizes = [aw, aw, aw, ATT_HEADS, inner, inner + 2 * gs, SSD_HEADS]
    sc3 = w_in.shape[1] - sum(sizes) - 3 * d
    sizes += [sc3, 3 * d]
    offs = [0]
    for s in sizes:
        offs.append(offs[-1] + s)
    part = lambda k: w_in[:, offs[k]:offs[k + 1]]

    def pad_heads(w):
        w = w.reshape(d, ATT_HEADS, ATT_HEAD_DIM)
        return jnp.pad(w, ((0, 0), (0, 0), (0, ATT_HEAD_DIM))).reshape(d, 2 * aw)

    pieces = [("k", pad_heads(part(1))), ("q", part(0) * (ATT_HEAD_DIM ** -0.5 * LOG2E)), ("v", part(2)),
              ("z", part(4)), ("x", part(5)[:, :inner]), ("B", part(5)[:, inner:inner + gs]),
              ("C", part(5)[:, inner + gs:]), ("sc_b", part(7)[:, :sc3 // 3]),
              ("sc_c", part(7)[:, sc3 // 3:2 * sc3 // 3]), ("sc_h", part(7)[:, 2 * sc3 // 3:]),
              ("g_att", part(8)[:, :d]), ("g_ssd", part(8)[:, d:2 * d]), ("g_conv", part(8)[:, 2 * d:])]
    cols, off = {}, 0
    for name, w in pieces:
        cols[name] = off
        off += w.shape[1]
    w_main = jnp.concatenate([w for _, w in pieces], axis=1).astype(BF16)
    w_small = jnp.zeros((d, 2 * LANE), F32).at[:, :ATT_HEADS].set(part(3)).at[:, LANE:LANE + SSD_HEADS].set(part(6))
    ws_hi = w_small.astype(BF16)
    ws_lo = (w_small - ws_hi.astype(F32)).astype(BF16)
    return w_main, ws_hi, ws_lo, cols


def kernel(x, meta_tokens, ln_in_g, ln_in_b, w_in, b_forget, ssd_conv_w, ssd_conv_b, ssd_dt_bias, ssd_a_log,
           ssd_d, ssd_norm_w, sc_conv_w, w_proj_attn, w_proj_ssd, w_proj_conv, w_out, ln_mix_g, ln_mix_b,
           dense_w_gu, dense_w_down, router_w, router_b, moe_w_gu, moe_w_down, ln_ffn_g, ln_ffn_b):
    batch, seq, d = x.shape
    depth = w_in.shape[0]
    alpha = (2 * depth) ** 0.25
    pad = (-(N_META + seq)) % MXU_DIM
    lp = pad + N_META + seq
    tp = batch * lp
    tk = MXU_DIM
    tq = _divisor_tile(lp, 3 * MXU_DIM, MXU_DIM)

    ln_x = _ln_rows(x.reshape(batch * seq, d), ln_in_g, ln_in_b).reshape(batch, seq, d)
    ln_meta = _ln_rows(meta_tokens.astype(x.dtype), ln_in_g, ln_in_b)
    h = jnp.concatenate([jnp.zeros((batch, pad, d), F32),
                         jnp.broadcast_to(ln_meta[None], (batch, N_META, d)), ln_x], axis=1).reshape(tp, d)

    for layer in range(depth):
        w_main, ws_hi, ws_lo, cols = _prepare_in_proj(w_in[layer])
        proj, small = _in_proj(h, w_main, ws_hi, ws_lo)
        ka, q_t, v_t = _attn_prep(proj, small, cols, b_forget[layer], batch, lp, pad, tk)
        y_att_t = _attention(q_t, ka, v_t, batch, lp, pad, tq, tk)
        y_ssd = _ssd(proj, small, cols, ssd_conv_w[layer], ssd_conv_b[layer], ssd_dt_bias[layer],
                     ssd_a_log[layer], ssd_d[layer], ssd_norm_w[layer], batch, lp, pad)
        j = layer // 2
        router = None
        if layer % 2 == 1:
            rw = jnp.zeros((d, LANE), F32).at[:, :N_EXPERTS].set(router_w[j])
            rw_hi = rw.astype(BF16)
            rw_lo = (rw - rw_hi.astype(F32)).astype(BF16)
            router = (rw_hi, rw_lo, jnp.zeros((1, LANE), F32).at[0, :N_EXPERTS].set(router_b[j]))
        outs = _merge(y_att_t, y_ssd, proj, cols, h, w_proj_attn[layer].astype(BF16),
                      w_proj_ssd[layer].astype(BF16), w_proj_conv[layer].astype(BF16),
                      w_out[layer].astype(BF16), sc_conv_w[layer], ln_mix_g[layer], ln_mix_b[layer],
                      lp, pad, alpha, router)
        if layer % 2 == 0:
            h = _ffn(outs[0], dense_w_gu[j].astype(BF16), dense_w_down[j].astype(BF16),
                     ln_ffn_g[layer], ln_ffn_b[layer], lp, pad, alpha)
        else:
            h_mid, route_idx, route_w = outs
            h = _moe(h_mid, route_idx, route_w, moe_w_gu[j].astype(BF16), moe_w_down[j].astype(BF16),
                     ln_ffn_g[layer], ln_ffn_b[layer], batch, lp, pad, alpha)
    return h.reshape(batch, lp, d)[:, pad + N_META:]
```

```python
import functools

import jax
import jax.numpy as jnp
from jax import lax
from jax.experimental import pallas as pl
from jax.experimental.pallas import tpu as pltpu

F32, BF16, I32 = jnp.float32, jnp.bfloat16, jnp.int32

N_META = 16
ATT_HEADS = 8
ATT_HEAD_DIM = 64
SSD_HEADS = 16
SSD_HEAD_DIM = 64
SSD_GROUPS = 2
SSD_HPG = SSD_HEADS // SSD_GROUPS
SSD_STATE = 128
SSD_CONV = 4
SSD_CHUNK = 128
SC_CONV = 3
N_EXPERTS = 8
LN_EPS = 1e-5
RMS_EPS = 1e-5
NEG_INF = -1e30
LOG2E = 1.4426950408889634

LANE = 128
SUBLANE = 8
MXU_DIM = 256
VMEM_LIMIT = 56 * 1024 * 1024
TOKEN_TILE_ROWS = SUBLANE


def _cparams(*sem):
    return pltpu.CompilerParams(dimension_semantics=tuple(sem), vmem_limit_bytes=VMEM_LIMIT)


def _divisor_tile(n, target, quantum):
    best = None
    t = quantum
    while t <= min(n, target):
        if n % t == 0:
            best = t
        t += quantum
    assert best is not None, (n, target, quantum)
    return best


def _dot(a, b):
    return jnp.dot(a, b, preferred_element_type=F32)


def _dot_nt(a, b):
    return lax.dot_general(a, b, (((1,), (1,)), ((), ())), preferred_element_type=F32)


def _split2(x):
    hi = x.astype(BF16)
    lo = (x - hi.astype(F32)).astype(BF16)
    return hi, lo


def _split3(x):
    hi = x.astype(BF16)
    r = x - hi.astype(F32)
    mid = r.astype(BF16)
    lo = (r - mid.astype(F32)).astype(BF16)
    return hi, mid, lo


def _dot_x3(a_f32, w_hi, w_lo):
    a_hi, a_lo = _split2(a_f32)
    return _dot(a_hi, w_hi) + _dot(a_hi, w_lo) + _dot(a_lo, w_hi)


def _ln(v, g, b):
    mu = jnp.mean(v, axis=-1, keepdims=True)
    c = v - mu
    var = jnp.mean(c * c, axis=-1, keepdims=True)
    return c * lax.rsqrt(var + LN_EPS) * g + b


def _silu(v):
    return v * jax.nn.sigmoid(v)


def _softplus(v):
    return jnp.maximum(v, 0.0) + jnp.log(1.0 + jnp.exp(-jnp.abs(v)))


def _zero_pad_rows(v, tile_index, tm, tiles_per_seq, pad):
    row = (tile_index % tiles_per_seq) * tm + lax.broadcasted_iota(I32, (tm, 1), 0)
    return jnp.where(row >= pad, v, 0.0)


def _embed_ln_kernel(x_ref, meta_ref, g_ref, b_ref, o_ref, *, pad):
    j = pl.program_id(1)

    @pl.when(j == 0)
    def _():
        meta = _ln(meta_ref[...].astype(F32), g_ref[...], b_ref[...])
        o_ref[0] = jnp.concatenate([jnp.zeros((pad, meta.shape[1]), F32), meta], axis=0)

    @pl.when(j > 0)
    def _():
        o_ref[0] = _ln(x_ref[0].astype(F32), g_ref[...], b_ref[...])


def _embed_ln(x, meta, g, b, pad):
    batch, seq, d = x.shape
    head = pad + N_META
    assert pad % SUBLANE == 0 and seq % head == 0, (pad, seq)
    return pl.pallas_call(
        functools.partial(_embed_ln_kernel, pad=pad),
        grid=(batch, 1 + seq // head),
        in_specs=[pl.BlockSpec((1, head, d), lambda bi, j: (bi, jnp.maximum(j - 1, 0), 0)),
                  pl.BlockSpec((N_META, d), lambda bi, j: (0, 0)),
                  pl.BlockSpec((1, d), lambda bi, j: (0, 0)),
                  pl.BlockSpec((1, d), lambda bi, j: (0, 0))],
        out_specs=pl.BlockSpec((1, head, d), lambda bi, j: (bi, j, 0)),
        out_shape=jax.ShapeDtypeStruct((batch, head + seq, d), F32),
        compiler_params=_cparams("parallel", "arbitrary"),
        name="embed_ln",
    )(x, meta.astype(x.dtype), g.reshape(1, d), b.reshape(1, d))


def _in_proj_kernel(x_ref, w_ref, wsh_ref, wsl_ref, o_ref, os_ref, xb_ref):
    @pl.when(pl.program_id(1) == 0)
    def _():
        x = x_ref[...]
        xb_ref[...] = x.astype(BF16)
        os_ref[...] = _dot_x3(x, wsh_ref[...], wsl_ref[...])

    o_ref[...] = _dot(xb_ref[...], w_ref[...]).astype(BF16)


def _in_proj(h, w_main, ws_hi, ws_lo):
    tp, d = h.shape
    n = w_main.shape[1]
    ns = ws_hi.shape[1]
    tm = _divisor_tile(tp, 1280, LANE)
    tn = _divisor_tile(n, 1024, LANE)
    return pl.pallas_call(
        _in_proj_kernel,
        grid=(tp // tm, n // tn),
        in_specs=[pl.BlockSpec((tm, d), lambda i, j: (i, 0)),
                  pl.BlockSpec((d, tn), lambda i, j: (0, j)),
                  pl.BlockSpec((d, ns), lambda i, j: (0, 0)),
                  pl.BlockSpec((d, ns), lambda i, j: (0, 0))],
        out_specs=[pl.BlockSpec((tm, tn), lambda i, j: (i, j)),
                   pl.BlockSpec((tm, ns), lambda i, j: (i, 0))],
        out_shape=[jax.ShapeDtypeStruct((tp, n), BF16),
                   jax.ShapeDtypeStruct((tp, ns), F32)],
        scratch_shapes=[pltpu.VMEM((tm, d), BF16)],
        compiler_params=_cparams("parallel", "arbitrary"),
        name="in_proj",
    )(h, w_main, ws_hi, ws_lo)


ATT_BIAS_ROWS = 2 * SUBLANE
ATT_V_ROWS = ATT_HEAD_DIM + 2 * SUBLANE
ATT_BOUND_SLACK = (1.02, 1.0)


def _attn_prep_kernel(k_ref, q_ref, v_ref, f_ref, bf_ref, tri_ref, pk_ref, gk_ref, gq_ref,
                      ko_ref, qo_ref, vo_ref, carry_ref, *, tm, tiles_per_seq, pad):
    t = pl.program_id(0) % tiles_per_seq
    hd = ATT_HEAD_DIM
    hw = 2 * hd

    @pl.when(t == 0)
    def _():
        carry_ref[...] = jnp.zeros_like(carry_ref)

    x = f_ref[...] + bf_ref[...]
    log_f = -_softplus(-x) * LOG2E
    row = t * tm + lax.broadcasted_iota(I32, (tm, LANE), 0)
    lane = lax.broadcasted_iota(I32, (tm, LANE), 1)
    log_f = jnp.where((row >= pad) & (lane < ATT_HEADS), log_f, 0.0)
    tri = tri_ref[...]
    hi, mid, lo = _split3(log_f)
    c = _dot(tri, hi) + _dot(tri, mid) + _dot(tri, lo) + carry_ref[0:1, :]
    carry_ref[0:1, :] = c[tm - 1:tm, :]

    kf = k_ref[...].astype(F32)
    qf = q_ref[...].astype(F32)
    k_max2 = jnp.maximum(carry_ref[1:2, :], jnp.max(_dot((kf * kf).astype(BF16), gk_ref[...]), axis=0, keepdims=True))
    carry_ref[1:2, :] = k_max2
    q_norm2 = _dot((qf * qf).astype(BF16), gq_ref[...])
    m = jnp.sqrt(q_norm2 * k_max2) * ATT_BOUND_SLACK[0] + ATT_BOUND_SLACK[1]

    hi, mid, lo = _split3(c)
    packed = jnp.where(lane < 8, hi.astype(F32),
             jnp.where(lane < 16, pltpu.roll(mid.astype(F32), 8, 1),
             jnp.where(lane < 24, pltpu.roll(lo.astype(F32), 16, 1),
             jnp.where(lane == 24, 1.0, 0.0)))).astype(BF16)
    ko_ref[...] = (kf + _dot(packed, pk_ref[...])).astype(BF16)

    q_t = qf.T
    c_hi, c_mid, c_lo = (p.astype(F32) for p in _split3(c.T))
    m_hi, m_mid, m_lo = (p.astype(F32) for p in _split3(m.T))
    sub = lax.broadcasted_iota(I32, (SUBLANE, tm), 0)
    for h in range(ATT_HEADS):
        bias_a = jnp.where(sub == 0, c_hi[h:h + 1, :],
                 jnp.where(sub == 1, c_mid[h:h + 1, :],
                 jnp.where(sub == 2, c_lo[h:h + 1, :],
                 jnp.where(sub < 6, 1.0,
                 jnp.where(sub == 6, -m_hi[h:h + 1, :], -m_mid[h:h + 1, :])))))
        bias_b = jnp.where(sub == 0, -m_lo[h:h + 1, :], 0.0)
        blk = jnp.concatenate([q_t[h * hd:(h + 1) * hd, :], bias_a, bias_b,
                               jnp.zeros((hw - hd - ATT_BIAS_ROWS, tm), F32)], axis=0)
        qo_ref[h * hw:(h + 1) * hw, :] = blk.astype(BF16)

    v_t = v_ref[...].astype(F32).T
    ones_row = jnp.where(sub == 0, 1.0, 0.0)
    for h in range(ATT_HEADS):
        blk = jnp.concatenate([v_t[h * hd:(h + 1) * hd, :], ones_row, jnp.zeros((SUBLANE, tm), F32)], axis=0)
        vo_ref[h, 0] = blk.astype(BF16)


def _key_bias_placement():
    hw = 2 * ATT_HEAD_DIM
    rows, cols, vals = [], [], []
    for h in range(ATT_HEADS):
        base = h * hw + ATT_HEAD_DIM
        for piece in range(3):
            rows.append(24); cols.append(base + piece); vals.append(1.0)
            rows.append(8 * piece + h); cols.append(base + 3 + piece); vals.append(-1.0)
            rows.append(24); cols.append(base + 6 + piece); vals.append(1.0)
    pk = jnp.zeros((LANE, ATT_HEADS * hw), F32).at[jnp.array(rows), jnp.array(cols)].set(jnp.array(vals))
    return pk.astype(BF16)


def _head_grouping(width, per_head):
    return (jnp.arange(width)[:, None] // per_head == jnp.arange(LANE)[None, :]).astype(BF16)


def _attn_prep(proj, small, cols, b_forget, batch, lp, pad, tk):
    tp = proj.shape[0]
    hw = 2 * ATT_HEAD_DIM
    kw = ATT_HEADS * hw
    aw = ATT_HEADS * ATT_HEAD_DIM
    tm = tk
    tiles_per_seq = lp // tm
    tri = jnp.tril(jnp.ones((tm, tm), F32)).astype(BF16)
    bf = jnp.zeros((1, LANE), F32).at[0, :ATT_HEADS].set(b_forget)
    kern = functools.partial(_attn_prep_kernel, tm=tm, tiles_per_seq=tiles_per_seq, pad=pad)
    return pl.pallas_call(
        kern,
        grid=(tp // tm,),
        in_specs=[pl.BlockSpec((tm, kw), lambda i: (i, cols["k"] // kw)),
                  pl.BlockSpec((tm, aw), lambda i: (i, cols["q"] // aw)),
                  pl.BlockSpec((tm, aw), lambda i: (i, cols["v"] // aw)),
                  pl.BlockSpec((tm, LANE), lambda i: (i, 0)),
                  pl.BlockSpec((1, LANE), lambda i: (0, 0)),
                  pl.BlockSpec((tm, tm), lambda i: (0, 0)),
                  pl.BlockSpec((LANE, kw), lambda i: (0, 0)),
                  pl.BlockSpec((kw, LANE), lambda i: (0, 0)),
                  pl.BlockSpec((aw, LANE), lambda i: (0, 0))],
        out_specs=[pl.BlockSpec((tm, kw), lambda i: (i, 0)),
                   pl.BlockSpec((kw, tm), lambda i: (i // tiles_per_seq, i % tiles_per_seq)),
                   pl.BlockSpec((ATT_HEADS, 1, ATT_V_ROWS, tm),
                                lambda i: (i // tiles_per_seq, i % tiles_per_seq, 0, 0))],
        out_shape=[jax.ShapeDtypeStruct((tp, kw), BF16),
                   jax.ShapeDtypeStruct((batch * kw, lp), BF16),
                   jax.ShapeDtypeStruct((batch * ATT_HEADS, tiles_per_seq, ATT_V_ROWS, tm), BF16)],
        scratch_shapes=[pltpu.VMEM((SUBLANE, LANE), F32)],
        compiler_params=_cparams("arbitrary"),
        name="attn_prep",
    )(proj, proj, proj, small, bf, tri, _key_bias_placement(), _head_grouping(kw, hw),
      _head_grouping(aw, ATT_HEAD_DIM))


ATT_HEADS_PER_STEP = 8
ATT_MIN_ROW_SUM = 2.0 ** -80


def _attn_kernel(qt_ref, k_ref, vt_ref, o_ref, m_ref, acc_ref, *, tq, tk, pad, heads):
    i = pl.program_id(2)
    r = tq // tk
    hd = ATT_HEAD_DIM
    hw = 2 * hd
    vr = ATT_V_ROWS

    def scores(j, hh, masked):
        k = k_ref[pl.ds(pl.multiple_of(j * tk, tk), tk), hh * hw:(hh + 1) * hw]
        s = _dot(k, qt_ref[hh * hw:(hh + 1) * hw, :])
        if masked:
            kpos = j * tk + lax.broadcasted_iota(I32, (tk, tq), 0)
            qpos = i * tq + lax.broadcasted_iota(I32, (tk, tq), 1)
            s = jnp.where((kpos <= qpos) & (kpos >= pad), s, NEG_INF)
        return s

    def fast_step(j, masked):
        s_next = scores(j, 0, masked)
        for hh in range(heads):
            s = s_next
            if hh + 1 < heads:
                s_next = scores(j, hh + 1, masked)
            p = jnp.exp2(s).astype(BF16)
            acc_ref[hh * vr:(hh + 1) * vr, :] += _dot(vt_ref[hh, j], p)

    def exact_step(j, masked):
        for hh in range(heads):
            s = scores(j, hh, masked)
            m_old = m_ref[hh:hh + 1, :]
            m_new = jnp.maximum(m_old, jnp.max(s, axis=0, keepdims=True))
            p = jnp.exp2(s - m_new).astype(BF16)
            acc_ref[hh * vr:(hh + 1) * vr, :] = (jnp.exp2(m_old - m_new) * acc_ref[hh * vr:(hh + 1) * vr, :]
                                                 + _dot(vt_ref[hh, j], p))
            m_ref[hh:hh + 1, :] = m_new

    def sweep(step):
        def masked_body(j, c):
            step(j, True)
            return c

        def plain_body(j, c):
            step(j, False)
            return c

        acc_ref[...] = jnp.zeros_like(acc_ref)
        step(0, True)
        lax.fori_loop(1, r * i, plain_body, 0)
        lax.fori_loop(jnp.maximum(r * i, 1), r * i + r, masked_body, 0)

    sweep(fast_step)
    real = i * tq + lax.broadcasted_iota(I32, (1, tq), 1) >= pad
    smallest = jnp.min(jnp.concatenate(
        [jnp.where(real, acc_ref[hh * vr + hd:hh * vr + hd + 1, :], 1.0) for hh in range(heads)], axis=0))

    @pl.when(smallest < ATT_MIN_ROW_SUM)
    def _():
        m_ref[...] = jnp.full_like(m_ref, NEG_INF)
        sweep(exact_step)

    for hh in range(heads):
        l = acc_ref[hh * vr + hd:hh * vr + hd + 1, :]
        l = jnp.where(l > 0.0, l, 1.0)
        o_ref[hh * hd:(hh + 1) * hd, :] = (acc_ref[hh * vr:hh * vr + hd, :] / l).astype(BF16)


def _attention(q_t, ka, v_t, batch, lp, pad, tq, tk):
    hd = ATT_HEAD_DIM
    hw = 2 * hd
    nh = ATT_HEADS_PER_STEP
    groups = ATT_HEADS // nh
    nq = lp // tq
    nkv = lp // tk
    kern = functools.partial(_attn_kernel, tq=tq, tk=tk, pad=pad, heads=nh)
    whole_seq = dict(pipeline_mode=pl.Buffered(1))
    return pl.pallas_call(
        kern,
        grid=(batch, groups, nq),
        in_specs=[pl.BlockSpec((nh * hw, tq), lambda b, g, i: (b * groups + g, i)),
                  pl.BlockSpec((lp, nh * hw), lambda b, g, i: (b, g), **whole_seq),
                  pl.BlockSpec((nh, nkv, ATT_V_ROWS, tk), lambda b, g, i: (b * groups + g, 0, 0, 0), **whole_seq)],
        out_specs=pl.BlockSpec((nh * hd, tq), lambda b, g, i: (b * groups + g, i)),
        out_shape=jax.ShapeDtypeStruct((batch * ATT_HEADS * hd, lp), BF16),
        scratch_shapes=[pltpu.VMEM((nh, tq), F32), pltpu.VMEM((nh * ATT_V_ROWS, tq), F32)],
        compiler_params=_cparams("parallel", "parallel", "arbitrary"),
        name="fox_attention",
    )(q_t, ka, v_t)


def _ssd_kernel(x_ref, b_ref, c_ref, z_ref, dt_ref, cw_ref, cb_ref, dtb_ref, alog_ref, dexp_ref, nw_ref,
                tri_ref, e_ref, o_ref, ubuf_ref, state_ref, ybuf_ref, *, pad):
    chunk = pl.program_id(1)
    blk = SSD_CHUNK
    inner = SSD_HEADS * SSD_HEAD_DIM
    gs = SSD_GROUPS * SSD_STATE
    gw = SSD_HPG * SSD_HEAD_DIM

    @pl.when(chunk == 0)
    def _():
        ubuf_ref[0:SUBLANE, :] = jnp.zeros((SUBLANE, ubuf_ref.shape[1]), F32)
        state_ref[...] = jnp.zeros_like(state_ref)

    ubuf_ref[SUBLANE:SUBLANE + blk, 0:inner] = x_ref[...].astype(F32)
    ubuf_ref[SUBLANE:SUBLANE + blk, inner:inner + gs] = b_ref[...].astype(F32)
    ubuf_ref[SUBLANE:SUBLANE + blk, inner + gs:inner + 2 * gs] = c_ref[...].astype(F32)
    conv = cb_ref[...]
    for k in range(SSD_CONV):
        conv = conv + cw_ref[k:k + 1, :] * ubuf_ref[pl.ds(SUBLANE - (SSD_CONV - 1) + k, blk), :]
    ubuf_ref[0:SUBLANE, :] = ubuf_ref[blk:blk + SUBLANE, :]
    xc = _silu(conv)
    xs = xc[:, 0:inner]
    bm = xc[:, inner:inner + gs]
    cm = xc[:, inner + gs:inner + 2 * gs]

    row = lax.broadcasted_iota(I32, (blk, LANE), 0)
    lane = lax.broadcasted_iota(I32, (blk, LANE), 1)
    dt = _softplus(dt_ref[...] + dtb_ref[...])
    dt = jnp.where((lane < SSD_HEADS) & (chunk * blk + row >= pad), dt, 0.0)
    a = -jnp.exp(alog_ref[...])
    tri = tri_ref[...]
    hi, mid, lo = _split3(dt * a)
    a_cum = _dot(tri, hi) + _dot(tri, mid) + _dot(tri, lo)
    a_last = a_cum[blk - 1:blk, :]
    ea = jnp.exp(a_cum)
    wgt = jnp.exp(a_last - a_cum) * dt

    def pieces(v, at):
        v_hi = v.astype(BF16).astype(F32)
        v_lo = v - v_hi
        return (pltpu.roll(v_hi, at, 1) if at else v_hi), pltpu.roll(v_lo, at + SSD_HEADS, 1)

    dt_hi, dt_lo = pieces(dt, 0)
    ea_hi, ea_lo = pieces(ea, 2 * SSD_HEADS)
    wg_hi, wg_lo = pieces(wgt, 4 * SSD_HEADS)
    h16 = SSD_HEADS
    packed = jnp.where(lane < h16, dt_hi,
             jnp.where(lane < 2 * h16, dt_lo,
             jnp.where(lane < 3 * h16, ea_hi,
             jnp.where(lane < 4 * h16, ea_lo,
             jnp.where(lane < 5 * h16, wg_hi,
             jnp.where(lane < 6 * h16, wg_lo, 0.0)))))).astype(BF16)
    expanded = _dot(packed, e_ref[...])
    dt_exp = expanded[:, 0:inner]
    ea_exp = expanded[:, inner:2 * inner]
    wg_exp = expanded[:, 2 * inner:3 * inner]
    xdt = (xs * dt_exp).astype(BF16)
    xw = (xs * wg_exp).astype(BF16)

    a_cum_t = a_cum.T
    causal = row >= lane
    for g in range(SSD_GROUPS):
        cg = cm[:, g * SSD_STATE:(g + 1) * SSD_STATE].astype(BF16)
        bg_f32 = bm[:, g * SSD_STATE:(g + 1) * SSD_STATE]
        cbg = _dot_nt(cg, bg_f32.astype(BF16))
        mats = []
        for hh in range(SSD_HPG):
            h = g * SSD_HPG + hh
            diff = a_cum[:, h:h + 1] - a_cum_t[h:h + 1, :]
            decay = jnp.exp(jnp.where(causal, diff, NEG_INF))
            mats.append((cbg * decay).astype(BF16))
        for j in range(SSD_HPG // 2):
            col = g * gw + j * LANE
            xp = xdt[:, col:col + LANE]
            y0 = _dot(mats[2 * j], xp)
            y1 = _dot(mats[2 * j + 1], xp)
            ybuf_ref[:, col:col + LANE] = jnp.where(lane < SSD_HEAD_DIM, y0, y1)
        st = state_ref[g]
        y_off = _dot(cg, st.astype(BF16))
        ybuf_ref[:, g * gw:(g + 1) * gw] += y_off * ea_exp[:, g * gw:(g + 1) * gw]
        chunk_decay = ea_exp[blk - 1:blk, g * gw:(g + 1) * gw]
        state_ref[g] = st * chunk_decay + _dot(bg_f32.T.astype(BF16), xw[:, g * gw:(g + 1) * gw])

    y = ybuf_ref[...] + dexp_ref[...] * xs
    u = y * _silu(z_ref[...].astype(F32))
    for g in range(SSD_GROUPS):
        ug = u[:, g * gw:(g + 1) * gw]
        ug = ug * lax.rsqrt(jnp.mean(ug * ug, axis=-1, keepdims=True) + RMS_EPS)
        o_ref[:, g * gw:(g + 1) * gw] = (ug * nw_ref[:, g * gw:(g + 1) * gw]).astype(BF16)


def _ssd_expand_matrix():
    inner = SSD_HEADS * SSD_HEAD_DIM
    rows = jnp.arange(LANE)
    cols = jnp.arange(3 * inner)
    seg = cols // inner
    head = (cols % inner) // SSD_HEAD_DIM
    hit = (rows[:, None] // (2 * SSD_HEADS) == seg[None, :]) & (rows[:, None] % SSD_HEADS == head[None, :]) \
          & (rows[:, None] < 6 * SSD_HEADS)
    return hit.astype(BF16)


def _ssd(proj, small, cols, conv_w, conv_b, dt_bias, a_log, d, norm_w, batch, lp, pad):
    tp = proj.shape[0]
    blk = SSD_CHUNK
    inner = SSD_HEADS * SSD_HEAD_DIM
    gs = SSD_GROUPS * SSD_STATE
    nc = lp // blk
    cdim = inner + 2 * gs
    cw = jnp.zeros((SUBLANE, cdim), F32).at[:SSD_CONV].set(conv_w)
    pad_lane = lambda v: jnp.zeros((1, LANE), F32).at[0, :v.shape[0]].set(v)
    dexp = jnp.repeat(d, SSD_HEAD_DIM).reshape(1, inner)
    tri = jnp.tril(jnp.ones((blk, blk), F32)).astype(BF16)
    rowblk = lambda b, c: b * nc + c
    const = lambda shape: pl.BlockSpec(shape, lambda b, c: (0,) * len(shape))
    return pl.pallas_call(
        functools.partial(_ssd_kernel, pad=pad),
        grid=(batch, nc),
        in_specs=[pl.BlockSpec((blk, inner), lambda b, c: (rowblk(b, c), cols["x"] // inner)),
                  pl.BlockSpec((blk, gs), lambda b, c: (rowblk(b, c), cols["B"] // gs)),
                  pl.BlockSpec((blk, gs), lambda b, c: (rowblk(b, c), cols["C"] // gs)),
                  pl.BlockSpec((blk, inner), lambda b, c: (rowblk(b, c), cols["z"] // inner)),
                  pl.BlockSpec((blk, LANE), lambda b, c: (rowblk(b, c), 1)),
                  const((SUBLANE, cdim)), const((1, cdim)), const((1, LANE)), const((1, LANE)),
                  const((1, inner)), const((1, inner)), const((blk, blk)), const((LANE, 3 * inner))],
        out_specs=pl.BlockSpec((blk, inner), lambda b, c: (rowblk(b, c), 0)),
        out_shape=jax.ShapeDtypeStruct((tp, inner), BF16),
        scratch_shapes=[pltpu.VMEM((blk + SUBLANE, cdim), F32),
                        pltpu.VMEM((SSD_GROUPS, SSD_STATE, SSD_HPG * SSD_HEAD_DIM), F32),
                        pltpu.VMEM((blk, inner), F32)],
        compiler_params=_cparams("parallel", "arbitrary"),
        name="ssd",
    )(proj, proj, proj, proj, small, cw, conv_b.reshape(1, cdim), pad_lane(dt_bias), pad_lane(a_log),
      dexp, norm_w.reshape(1, inner), tri, _ssd_expand_matrix())


def _merge_kernel(yat_ref, ys_ref, sb_ref, sc_ref, sh_ref, sch_ref, shh_ref, ga_ref, gs_ref, gc_ref, h_ref,
                  wpa_ref, wps_ref, wpc_ref, wo_ref, scw_ref, g_ref, b_ref, *rest,
                  tm, tiles_per_seq, pad, alpha, with_router):
    if with_router:
        rwh_ref, rwl_ref, rb_ref, o_ref, ri_ref, rw_ref, vbuf_ref = rest
    else:
        o_ref, vbuf_ref = rest
    vbuf_ref[0:SUBLANE, :] = sch_ref[...].astype(F32) * shh_ref[...].astype(F32)
    vbuf_ref[SUBLANE:SUBLANE + tm, :] = sc_ref[...].astype(F32) * sh_ref[...].astype(F32)
    conv = jnp.zeros((tm, vbuf_ref.shape[1]), F32)
    for k in range(SC_CONV):
        conv = conv + scw_ref[k:k + 1, :] * vbuf_ref[pl.ds(SUBLANE - (SC_CONV - 1) + k, tm), :]
    y_conv = (sb_ref[...].astype(F32) * conv).astype(BF16)
    y_att = yat_ref[...].astype(F32).T.astype(BF16)

    merged = jax.nn.sigmoid(ga_ref[...].astype(F32)) * _dot(y_att, wpa_ref[...])
    merged += jax.nn.sigmoid(gs_ref[...].astype(F32)) * _dot(ys_ref[...], wps_ref[...])
    merged += jax.nn.sigmoid(gc_ref[...].astype(F32)) * _dot(y_conv, wpc_ref[...])
    mix = _dot(merged.astype(BF16), wo_ref[...])
    hn = _ln(alpha * h_ref[...] + mix, g_ref[...], b_ref[...])
    hn = _zero_pad_rows(hn, pl.program_id(0), tm, tiles_per_seq, pad)
    o_ref[...] = hn

    if with_router:
        lane = lax.broadcasted_iota(I32, (tm, LANE), 1)
        lane_f = lane.astype(F32)
        logits = _dot_x3(hn, rwh_ref[...], rwl_ref[...]) + rb_ref[...]
        lg = jnp.where(lane < N_EXPERTS, logits, -jnp.inf)
        v1 = jnp.max(lg, axis=1, keepdims=True)
        i1 = jnp.min(jnp.where(lg == v1, lane_f, float(LANE)), axis=1, keepdims=True).astype(I32)
        lg2 = jnp.where(lane == i1, -jnp.inf, lg)
        v2 = jnp.max(lg2, axis=1, keepdims=True)
        i2 = jnp.min(jnp.where(lg2 == v2, lane_f, float(LANE)), axis=1, keepdims=True).astype(I32)
        e = jnp.exp(v2 - v1)
        w1 = 1.0 / (1.0 + e)
        w2 = e / (1.0 + e)
        ri_ref[...] = jnp.where(lane == 0, i1, jnp.where(lane == 1, i2, 0))
        rw_ref[...] = jnp.where(lane == 0, w1, jnp.where(lane == 1, w2, 0.0))


def _merge(y_att_t, y_ssd, proj, cols, h, wpa, wps, wpc, wo, sc_w, g, b, lp, pad, alpha, router=None):
    tp, d = h.shape
    scw = wpc.shape[0]
    aw = wpa.shape[0]
    tm = _divisor_tile(lp, 768, LANE)
    tiles_per_seq = lp // tm
    nt = tp // tm
    halo = lambda i: jnp.maximum(i * (tm // SUBLANE) - 1, 0)
    const = lambda shape: pl.BlockSpec(shape, lambda i: (0,) * len(shape))
    in_specs = [pl.BlockSpec((aw, tm), lambda i: (i // tiles_per_seq, i % tiles_per_seq)),
                pl.BlockSpec((tm, y_ssd.shape[1]), lambda i: (i, 0)),
                pl.BlockSpec((tm, scw), lambda i: (i, cols["sc_b"] // scw)),
                pl.BlockSpec((tm, scw), lambda i: (i, cols["sc_c"] // scw)),
                pl.BlockSpec((tm, scw), lambda i: (i, cols["sc_h"] // scw)),
                pl.BlockSpec((SUBLANE, scw), lambda i: (halo(i), cols["sc_c"] // scw)),
                pl.BlockSpec((SUBLANE, scw), lambda i: (halo(i), cols["sc_h"] // scw)),
                pl.BlockSpec((tm, d), lambda i: (i, cols["g_att"] // d)),
                pl.BlockSpec((tm, d), lambda i: (i, cols["g_ssd"] // d)),
                pl.BlockSpec((tm, d), lambda i: (i, cols["g_conv"] // d)),
                pl.BlockSpec((tm, d), lambda i: (i, 0)),
                const(wpa.shape), const(wps.shape), const(wpc.shape), const(wo.shape),
                const((SUBLANE, scw)), const((1, d)), const((1, d))]
    args = [y_att_t, y_ssd, proj, proj, proj, proj, proj, proj, proj, proj, h, wpa, wps, wpc, wo,
            jnp.zeros((SUBLANE, scw), F32).at[:SC_CONV].set(sc_w), g.reshape(1, d), b.reshape(1, d)]
    out_specs = [pl.BlockSpec((tm, d), lambda i: (i, 0))]
    out_shape = [jax.ShapeDtypeStruct((tp, d), F32)]
    if router is not None:
        rw_hi, rw_lo, rb = router
        in_specs += [const(rw_hi.shape), const(rw_lo.shape), const((1, LANE))]
        args += [rw_hi, rw_lo, rb]
        out_specs += [pl.BlockSpec((tm, LANE), lambda i: (i, 0)), pl.BlockSpec((tm, LANE), lambda i: (i, 0))]
        out_shape += [jax.ShapeDtypeStruct((tp, LANE), I32), jax.ShapeDtypeStruct((tp, LANE), F32)]
    kern = functools.partial(_merge_kernel, tm=tm, tiles_per_seq=tiles_per_seq, pad=pad, alpha=alpha,
                             with_router=router is not None)
    return pl.pallas_call(
        kern, grid=(nt,), in_specs=in_specs, out_specs=out_specs, out_shape=out_shape,
        scratch_shapes=[pltpu.VMEM((tm + SUBLANE, scw), F32)],
        compiler_params=_cparams("parallel"),
        name="merge_out",
    )(*args)


def _ffn_kernel(h_ref, wg_ref, wu_ref, wd_ref, g_ref, b_ref, o_ref, *, tm, tf, tiles_per_seq, pad, alpha):
    h = h_ref[...]
    xb = h.astype(BF16)
    acc = jnp.zeros_like(h)
    for c in range(wd_ref.shape[0] // tf):
        cols = slice(c * tf, (c + 1) * tf)
        act = (_silu(_dot(xb, wg_ref[:, cols])) * _dot(xb, wu_ref[:, cols])).astype(BF16)
        acc = acc + _dot(act, wd_ref[cols, :])
    hn = _ln(alpha * h + acc, g_ref[...], b_ref[...])
    o_ref[...] = _zero_pad_rows(hn, pl.program_id(0), tm, tiles_per_seq, pad)


def _ffn(h, w_gu, w_down, g, b, lp, pad, alpha):
    tp, d = h.shape
    ff = w_down.shape[0]
    tm = _divisor_tile(lp, 768, LANE)
    tf = _divisor_tile(ff, 2 * MXU_DIM, MXU_DIM)
    kern = functools.partial(_ffn_kernel, tm=tm, tf=tf, tiles_per_seq=lp // tm, pad=pad, alpha=alpha)
    resident = dict(pipeline_mode=pl.Buffered(1))
    return pl.pallas_call(
        kern,
        grid=(tp // tm,),
        in_specs=[pl.BlockSpec((tm, d), lambda i: (i, 0)),
                  pl.BlockSpec((d, ff), lambda i: (0, 0), **resident),
                  pl.BlockSpec((d, ff), lambda i: (0, 1), **resident),
                  pl.BlockSpec((ff, d), lambda i: (0, 0), **resident),
                  pl.BlockSpec((1, d), lambda i: (0, 0)),
                  pl.BlockSpec((1, d), lambda i: (0, 0))],
        out_specs=pl.BlockSpec((tm, d), lambda i: (i, 0)),
        out_shape=jax.ShapeDtypeStruct((tp, d), F32),
        compiler_params=_cparams("parallel"),
        name="dense_ffn",
    )(h, w_gu, w_gu, w_down, g.reshape(1, d), b.reshape(1, d))


def _moe_scatter_kernel(slot_ref, h_ref, init_ref, o_ref, buf_ref, sem, *, tm):
    del init_ref
    for s in range(TOKEN_TILE_ROWS):
        buf_ref[pl.ds(s, tm, stride=TOKEN_TILE_ROWS), :] = h_ref[:, s * LANE:(s + 1) * LANE]

    def issue(r, c):
        src = buf_ref.at[pl.ds(pl.multiple_of(r * TOKEN_TILE_ROWS, TOKEN_TILE_ROWS), TOKEN_TILE_ROWS)]
        for k in range(2):
            slot = slot_ref[0, 0, 2 * r + k]
            dst = o_ref.at[pl.ds(pl.multiple_of(slot * TOKEN_TILE_ROWS, TOKEN_TILE_ROWS), TOKEN_TILE_ROWS)]
            pltpu.make_async_copy(src, dst, sem).start()
        return c

    lax.fori_loop(0, tm, issue, 0)
    for _ in range(2):
        pltpu.make_async_copy(buf_ref, o_ref.at[pl.ds(0, tm * TOKEN_TILE_ROWS)], sem).wait()


def _moe_scatter(h, slots, n_rows, lp):
    tp, d = h.shape
    assert d == TOKEN_TILE_ROWS * LANE
    tm = _divisor_tile(lp, 768, LANE)
    nt = tp // tm
    init = jnp.zeros((n_rows * TOKEN_TILE_ROWS, LANE), F32)
    return pl.pallas_call(
        functools.partial(_moe_scatter_kernel, tm=tm),
        grid=(nt,),
        in_specs=[pl.BlockSpec((1, 1, 2 * tm), lambda i: (i, 0, 0), memory_space=pltpu.SMEM),
                  pl.BlockSpec((tm, d), lambda i: (i, 0)),
                  pl.BlockSpec(memory_space=pl.ANY)],
        out_specs=pl.BlockSpec(memory_space=pl.ANY),
        out_shape=jax.ShapeDtypeStruct(init.shape, F32),
        scratch_shapes=[pltpu.VMEM((tm * TOKEN_TILE_ROWS, LANE), F32), pltpu.SemaphoreType.DMA(())],
        input_output_aliases={2: 0},
        compiler_params=_cparams("arbitrary"),
        name="moe_scatter",
    )(slots.reshape(nt, 1, 2 * tm), h, init)


def _moe_kernel(te_ref, tv_ref, x_ref, wg_ref, wu_ref, wd_ref, o_ref, xb_ref, acc_ref, *, tm):
    n = pl.program_id(0)
    f = pl.program_id(1)

    @pl.when(f == 0)
    def _():
        for s in range(TOKEN_TILE_ROWS):
            xb_ref[:, s * LANE:(s + 1) * LANE] = x_ref[pl.ds(s, tm, stride=TOKEN_TILE_ROWS), :].astype(BF16)
        acc_ref[...] = jnp.zeros_like(acc_ref)

    @pl.when(tv_ref[n] == 1)
    def _():
        xb = xb_ref[...]
        act = (_silu(_dot(xb, wg_ref[0])) * _dot(xb, wu_ref[0])).astype(BF16)
        acc_ref[...] += _dot(act, wd_ref[0])

    @pl.when(f == pl.num_programs(1) - 1)
    def _():
        for s in range(TOKEN_TILE_ROWS):
            o_ref[pl.ds(s, tm, stride=TOKEN_TILE_ROWS), :] = acc_ref[:, s * LANE:(s + 1) * LANE]


def _moe_experts(x_sorted, tile_expert, tile_valid, w_gu, w_down, tm, n_tiles):
    d = w_down.shape[2]
    ff = w_down.shape[1]
    tf = _divisor_tile(ff, 7 * MXU_DIM, MXU_DIM)
    nf = ff // tf
    fblk = lambda f, n, tv: f * tv[n] + (nf - 1) * (1 - tv[n])
    rows = tm * TOKEN_TILE_ROWS
    grid_spec = pltpu.PrefetchScalarGridSpec(
        num_scalar_prefetch=2,
        grid=(n_tiles, nf),
        in_specs=[pl.BlockSpec((rows, LANE), lambda n, f, te, tv: (n, 0)),
                  pl.BlockSpec((1, d, tf), lambda n, f, te, tv: (te[n], 0, fblk(f, n, tv))),
                  pl.BlockSpec((1, d, tf), lambda n, f, te, tv: (te[n], 0, nf + fblk(f, n, tv))),
                  pl.BlockSpec((1, tf, d), lambda n, f, te, tv: (te[n], fblk(f, n, tv), 0))],
        out_specs=pl.BlockSpec((rows, LANE), lambda n, f, te, tv: (n, 0)),
        scratch_shapes=[pltpu.VMEM((tm, d), BF16), pltpu.VMEM((tm, d), F32)])
    return pl.pallas_call(
        functools.partial(_moe_kernel, tm=tm),
        grid_spec=grid_spec,
        out_shape=jax.ShapeDtypeStruct((n_tiles * rows, LANE), F32),
        compiler_params=_cparams("parallel", "arbitrary"),
        name="moe_experts",
    )(tile_expert, tile_valid, x_sorted, w_gu, w_gu, w_down)


def _moe_combine_kernel(slot_ref, y_ref, rw_ref, h_ref, g_ref, b_ref, o_ref, gbuf_ref, sem,
                        *, tm, tiles_per_seq, pad, alpha):
    def issue(r, c):
        for k in range(2):
            slot = slot_ref[0, 0, 2 * r + k]
            src = y_ref.at[pl.ds(pl.multiple_of(slot * TOKEN_TILE_ROWS, TOKEN_TILE_ROWS), TOKEN_TILE_ROWS)]
            dst = gbuf_ref.at[pl.ds(pl.multiple_of((k * tm + r) * TOKEN_TILE_ROWS, TOKEN_TILE_ROWS),
                                    TOKEN_TILE_ROWS)]
            pltpu.make_async_copy(src, dst, sem).start()
        return c

    lax.fori_loop(0, tm, issue, 0, unroll=4)
    pltpu.make_async_copy(y_ref.at[pl.ds(0, 2 * tm * TOKEN_TILE_ROWS)], gbuf_ref, sem).wait()

    rw = rw_ref[...]
    w0, w1 = rw[:, 0:1], rw[:, 1:2]
    ff = jnp.concatenate(
        [w0 * gbuf_ref[pl.ds(s, tm, stride=TOKEN_TILE_ROWS), :]
         + w1 * gbuf_ref[pl.ds(tm * TOKEN_TILE_ROWS + s, tm, stride=TOKEN_TILE_ROWS), :]
         for s in range(TOKEN_TILE_ROWS)], axis=1)
    hn = _ln(alpha * h_ref[...] + ff, g_ref[...], b_ref[...])
    o_ref[...] = _zero_pad_rows(hn, pl.program_id(0), tm, tiles_per_seq, pad)


def _moe_combine(y_sorted, slots, route_w, h, g, b, lp, pad, alpha):
    tp, d = h.shape
    tm = _divisor_tile(lp, 768, LANE)
    nt = tp // tm
    kern = functools.partial(_moe_combine_kernel, tm=tm, tiles_per_seq=lp // tm, pad=pad, alpha=alpha)
    return pl.pallas_call(
        kern,
        grid=(nt,),
        in_specs=[pl.BlockSpec((1, 1, 2 * tm), lambda i: (i, 0, 0), memory_space=pltpu.SMEM),
                  pl.BlockSpec(memory_space=pl.ANY),
                  pl.BlockSpec((tm, LANE), lambda i: (i, 0)),
                  pl.BlockSpec((tm, d), lambda i: (i, 0)),
                  pl.BlockSpec((1, d), lambda i: (0, 0)),
                  pl.BlockSpec((1, d), lambda i: (0, 0))],
        out_specs=pl.BlockSpec((tm, d), lambda i: (i, 0)),
        out_shape=jax.ShapeDtypeStruct((tp, d), F32),
        scratch_shapes=[pltpu.VMEM((2 * tm * TOKEN_TILE_ROWS, LANE), F32), pltpu.SemaphoreType.DMA(())],
        compiler_params=_cparams("arbitrary"),
        name="moe_combine",
    )(slots.reshape(nt, 1, 2 * tm), y_sorted, route_w, h, g.reshape(1, d), b.reshape(1, d))


def _moe(h, route_idx, route_w, w_gu, w_down, g, b, batch, lp, pad, alpha):
    tp, d = h.shape
    n_real = batch * (lp - pad)
    tm = 640
    experts = route_idx[:, :2].reshape(tp * 2)
    seq_row = jnp.arange(tp, dtype=I32) % lp
    real = jnp.repeat(seq_row >= pad, 2)
    onehot = ((experts[:, None] == jnp.arange(N_EXPERTS, dtype=I32)[None, :]) & real[:, None]).astype(I32)
    csum = jnp.cumsum(onehot, axis=0)
    counts = csum[-1]
    rank = jnp.sum(onehot * csum, axis=1) - 1
    tiles = (counts + tm - 1) // tm
    tile_end = jnp.cumsum(tiles)
    slot = ((tile_end - tiles) * tm)[experts] + rank
    n_tiles = (2 * n_real) // tm + N_EXPERTS
    tile_ids = jnp.arange(n_tiles, dtype=I32)
    tile_expert = jnp.minimum(jnp.sum((tile_ids[:, None] >= tile_end[None, :]).astype(I32), axis=1),
                              N_EXPERTS - 1).astype(I32)
    tile_valid = (tile_ids < tile_end[-1]).astype(I32)
    n_pad = batch * pad * 2
    spare_tiles = -(-n_pad // tm)
    pad_id = jnp.repeat((jnp.arange(tp, dtype=I32) // lp) * pad + seq_row, 2) * 2 + jnp.tile(jnp.arange(2, dtype=I32), tp)
    slot_scatter = jnp.where(real, slot, n_tiles * tm + pad_id).astype(I32)
    slot_gather = jnp.where(real, slot, 0).astype(I32)

    x_sorted = _moe_scatter(h, slot_scatter, (n_tiles + spare_tiles) * tm, lp)
    y_sorted = _moe_experts(x_sorted, tile_expert, tile_valid, w_gu, w_down, tm, n_tiles)
    return _moe_combine(y_sorted, slot_gather, route_w, h, g, b, lp, pad, alpha)


def _prepare_in_proj(w_in):
    d = w_in.shape[0]
    aw = ATT_HEADS * ATT_HEAD_DIM
    inner = SSD_HEADS * SSD_HEAD_DIM
    gs = SSD_GROUPS * SSD_STATE
    sizes = [aw, aw, aw, ATT_HEADS, inner, inner + 2 * gs, SSD_HEADS]
    sc3 = w_in.shape[1] - sum(sizes) - 3 * d
    sizes += [sc3, 3 * d]
    offs = [0]
    for s in sizes:
        offs.append(offs[-1] + s)
    part = lambda k: w_in[:, offs[k]:offs[k + 1]]

    def pad_heads(w):
        w = w.reshape(d, ATT_HEADS, ATT_HEAD_DIM)
        return jnp.pad(w, ((0, 0), (0, 0), (0, ATT_HEAD_DIM))).reshape(d, 2 * aw)

    pieces = [("k", pad_heads(part(1))), ("q", part(0) * (ATT_HEAD_DIM ** -0.5 * LOG2E)), ("v", part(2)),
              ("z", part(4)), ("x", part(5)[:, :inner]), ("B", part(5)[:, inner:inner + gs]),
              ("C", part(5)[:, inner + gs:]), ("sc_b", part(7)[:, :sc3 // 3]),
              ("sc_c", part(7)[:, sc3 // 3:2 * sc3 // 3]), ("sc_h", part(7)[:, 2 * sc3 // 3:]),
              ("g_att", part(8)[:, :d]), ("g_ssd", part(8)[:, d:2 * d]), ("g_conv", part(8)[:, 2 * d:])]
    cols, off = {}, 0
    for name, w in pieces:
        cols[name] = off
        off += w.shape[1]
    w_main = jnp.concatenate([w for _, w in pieces], axis=1).astype(BF16)
    w_small = jnp.zeros((d, 2 * LANE), F32).at[:, :ATT_HEADS].set(part(3)).at[:, LANE:LANE + SSD_HEADS].set(part(6))
    ws_hi = w_small.astype(BF16)
    ws_lo = (w_small - ws_hi.astype(F32)).astype(BF16)
    return w_main, ws_hi, ws_lo, cols


def kernel(x, meta_tokens, ln_in_g, ln_in_b, w_in, b_forget, ssd_conv_w, ssd_conv_b, ssd_dt_bias, ssd_a_log,
           ssd_d, ssd_norm_w, sc_conv_w, w_proj_attn, w_proj_ssd, w_proj_conv, w_out, ln_mix_g, ln_mix_b,
           dense_w_gu, dense_w_down, router_w, router_b, moe_w_gu, moe_w_down, ln_ffn_g, ln_ffn_b):
    batch, seq, d = x.shape
    depth = w_in.shape[0]
    alpha = (2 * depth) ** 0.25
    pad = (-(N_META + seq)) % MXU_DIM
    lp = pad + N_META + seq
    tp = batch * lp
    tk = MXU_DIM
    tq = _divisor_tile(lp, 3 * MXU_DIM, MXU_DIM)

    h = _embed_ln(x, meta_tokens, ln_in_g, ln_in_b, pad).reshape(tp, d)

    for layer in range(depth):
        w_main, ws_hi, ws_lo, cols = _prepare_in_proj(w_in[layer])
        proj, small = _in_proj(h, w_main, ws_hi, ws_lo)
        ka, q_t, v_t = _attn_prep(proj, small, cols, b_forget[layer], batch, lp, pad, tk)
        y_att_t = _attention(q_t, ka, v_t, batch, lp, pad, tq, tk)
        y_ssd = _ssd(proj, small, cols, ssd_conv_w[layer], ssd_conv_b[layer], ssd_dt_bias[layer],
                     ssd_a_log[layer], ssd_d[layer], ssd_norm_w[layer], batch, lp, pad)
        j = layer // 2
        router = None
        if layer % 2 == 1:
            rw = jnp.zeros((d, LANE), F32).at[:, :N_EXPERTS].set(router_w[j])
            rw_hi = rw.astype(BF16)
            rw_lo = (rw - rw_hi.astype(F32)).astype(BF16)
            router = (rw_hi, rw_lo, jnp.zeros((1, LANE), F32).at[0, :N_EXPERTS].set(router_b[j]))
        outs = _merge(y_att_t, y_ssd, proj, cols, h, w_proj_attn[layer].astype(BF16),
                      w_proj_ssd[layer].astype(BF16), w_proj_conv[layer].astype(BF16),
                      w_out[layer].astype(BF16), sc_conv_w[layer], ln_mix_g[layer], ln_mix_b[layer],
                      lp, pad, alpha, router)
        if layer % 2 == 0:
            h = _ffn(outs[0], dense_w_gu[j].astype(BF16), dense_w_down[j].astype(BF16),
                     ln_ffn_g[layer], ln_ffn_b[layer], lp, pad, alpha)
        else:
            h_mid, route_idx, route_w = outs
            h = _moe(h_mid, route_idx, route_w, moe_w_gu[j].astype(BF16), moe_w_down[j].astype(BF16),
                     ln_ffn_g[layer], ln_ffn_b[layer], batch, lp, pad, alpha)
    return h.reshape(batch, lp, d)[:, pad + N_META:]
```

```python
import functools

import jax
import jax.numpy as jnp
from jax import lax
from jax.experimental import pallas as pl
from jax.experimental.pallas import tpu as pltpu

F32, BF16, I32 = jnp.float32, jnp.bfloat16, jnp.int32

N_META = 16
ATT_HEADS = 8
ATT_HEAD_DIM = 64
SSD_HEADS = 16
SSD_HEAD_DIM = 64
SSD_GROUPS = 2
SSD_HPG = SSD_HEADS // SSD_GROUPS
SSD_STATE = 128
SSD_CONV = 4
SSD_CHUNK = 128
SC_CONV = 3
N_EXPERTS = 8
LN_EPS = 1e-5
RMS_EPS = 1e-5
NEG_INF = -1e30
LOG2E = 1.4426950408889634

LANE = 128
SUBLANE = 8
MXU_DIM = 256
VMEM_LIMIT = 56 * 1024 * 1024
TOKEN_TILE_ROWS = SUBLANE


def _cparams(*sem):
    return pltpu.CompilerParams(dimension_semantics=tuple(sem), vmem_limit_bytes=VMEM_LIMIT)


def _divisor_tile(n, target, quantum):
    best = None
    t = quantum
    while t <= min(n, target):
        if n % t == 0:
            best = t
        t += quantum
    assert best is not None, (n, target, quantum)
    return best


def _dot(a, b):
    return jnp.dot(a, b, preferred_element_type=F32)


def _dot_nt(a, b):
    return lax.dot_general(a, b, (((1,), (1,)), ((), ())), preferred_element_type=F32)


def _split2(x):
    hi = x.astype(BF16)
    lo = (x - hi.astype(F32)).astype(BF16)
    return hi, lo


def _split3(x):
    hi = x.astype(BF16)
    r = x - hi.astype(F32)
    mid = r.astype(BF16)
    lo = (r - mid.astype(F32)).astype(BF16)
    return hi, mid, lo


def _dot_x3(a_f32, w_hi, w_lo):
    a_hi, a_lo = _split2(a_f32)
    return _dot(a_hi, w_hi) + _dot(a_hi, w_lo) + _dot(a_lo, w_hi)


def _ln(v, g, b):
    mu = jnp.mean(v, axis=-1, keepdims=True)
    c = v - mu
    var = jnp.mean(c * c, axis=-1, keepdims=True)
    return c * lax.rsqrt(var + LN_EPS) * g + b


def _silu(v):
    return v * jax.nn.sigmoid(v)


def _softplus(v):
    return jnp.maximum(v, 0.0) + jnp.log(1.0 + jnp.exp(-jnp.abs(v)))


def _zero_pad_rows(v, tile_index, tm, tiles_per_seq, pad):
    row = (tile_index % tiles_per_seq) * tm + lax.broadcasted_iota(I32, (tm, 1), 0)
    return jnp.where(row >= pad, v, 0.0)


def _embed_ln_kernel(x_ref, meta_ref, g_ref, b_ref, o_ref, *, pad):
    j = pl.program_id(1)

    @pl.when(j == 0)
    def _():
        meta = _ln(meta_ref[...].astype(F32), g_ref[...], b_ref[...])
        o_ref[0] = jnp.concatenate([jnp.zeros((pad, meta.shape[1]), F32), meta], axis=0)

    @pl.when(j > 0)
    def _():
        o_ref[0] = _ln(x_ref[0].astype(F32), g_ref[...], b_ref[...])


def _embed_ln(x, meta, g, b, pad):
    batch, seq, d = x.shape
    head = pad + N_META
    assert pad % SUBLANE == 0 and seq % head == 0, (pad, seq)
    return pl.pallas_call(
        functools.partial(_embed_ln_kernel, pad=pad),
        grid=(batch, 1 + seq // head),
        in_specs=[pl.BlockSpec((1, head, d), lambda bi, j: (bi, jnp.maximum(j - 1, 0), 0)),
                  pl.BlockSpec((N_META, d), lambda bi, j: (0, 0)),
                  pl.BlockSpec((1, d), lambda bi, j: (0, 0)),
                  pl.BlockSpec((1, d), lambda bi, j: (0, 0))],
        out_specs=pl.BlockSpec((1, head, d), lambda bi, j: (bi, j, 0)),
        out_shape=jax.ShapeDtypeStruct((batch, head + seq, d), F32),
        compiler_params=_cparams("parallel", "arbitrary"),
        name="embed_ln",
    )(x, meta.astype(x.dtype), g.reshape(1, d), b.reshape(1, d))


def _in_proj_kernel(x_ref, w_ref, wsh_ref, wsl_ref, o_ref, os_ref, xb_ref):
    @pl.when(pl.program_id(1) == 0)
    def _():
        x = x_ref[...]
        xb_ref[...] = x.astype(BF16)
        os_ref[...] = _dot_x3(x, wsh_ref[...], wsl_ref[...])

    o_ref[...] = _dot(xb_ref[...], w_ref[...]).astype(BF16)


def _in_proj(h, w_main, ws_hi, ws_lo):
    tp, d = h.shape
    n = w_main.shape[1]
    ns = ws_hi.shape[1]
    tm = _divisor_tile(tp, 1280, LANE)
    tn = _divisor_tile(n, 9 * MXU_DIM, MXU_DIM)
    return pl.pallas_call(
        _in_proj_kernel,
        grid=(tp // tm, n // tn),
        in_specs=[pl.BlockSpec((tm, d), lambda i, j: (i, 0)),
                  pl.BlockSpec((d, tn), lambda i, j: (0, j)),
                  pl.BlockSpec((d, ns), lambda i, j: (0, 0)),
                  pl.BlockSpec((d, ns), lambda i, j: (0, 0))],
        out_specs=[pl.BlockSpec((tm, tn), lambda i, j: (i, j)),
                   pl.BlockSpec((tm, ns), lambda i, j: (i, 0))],
        out_shape=[jax.ShapeDtypeStruct((tp, n), BF16),
                   jax.ShapeDtypeStruct((tp, ns), F32)],
        scratch_shapes=[pltpu.VMEM((tm, d), BF16)],
        compiler_params=_cparams("parallel", "arbitrary"),
        name="in_proj",
    )(h, w_main, ws_hi, ws_lo)


ATT_BIAS_ROWS = 2 * SUBLANE
ATT_V_ROWS = ATT_HEAD_DIM + 2 * SUBLANE
ATT_BOUND_SLACK = (1.02, 1.0)


def _attn_prep_kernel(k_ref, q_ref, v_ref, f_ref, bf_ref, tri_ref, pk_ref, gk_ref, gq_ref,
                      ko_ref, qo_ref, vo_ref, carry_ref, *, tm, tiles_per_seq, pad):
    t = pl.program_id(0) % tiles_per_seq
    hd = ATT_HEAD_DIM
    hw = 2 * hd

    @pl.when(t == 0)
    def _():
        carry_ref[...] = jnp.zeros_like(carry_ref)

    x = f_ref[...] + bf_ref[...]
    log_f = -_softplus(-x) * LOG2E
    row = t * tm + lax.broadcasted_iota(I32, (tm, LANE), 0)
    lane = lax.broadcasted_iota(I32, (tm, LANE), 1)
    log_f = jnp.where((row >= pad) & (lane < ATT_HEADS), log_f, 0.0)
    tri = tri_ref[...]
    hi, mid, lo = _split3(log_f)
    c = _dot(tri, hi) + _dot(tri, mid) + _dot(tri, lo) + carry_ref[0:1, :]
    carry_ref[0:1, :] = c[tm - 1:tm, :]

    kf = k_ref[...].astype(F32)
    qf = q_ref[...].astype(F32)
    k_max2 = jnp.maximum(carry_ref[1:2, :], jnp.max(_dot((kf * kf).astype(BF16), gk_ref[...]), axis=0, keepdims=True))
    carry_ref[1:2, :] = k_max2
    q_norm2 = _dot((qf * qf).astype(BF16), gq_ref[...])
    m = jnp.sqrt(q_norm2 * k_max2) * ATT_BOUND_SLACK[0] + ATT_BOUND_SLACK[1]

    hi, mid, lo = _split3(c)
    packed = jnp.where(lane < 8, hi.astype(F32),
             jnp.where(lane < 16, pltpu.roll(mid.astype(F32), 8, 1),
             jnp.where(lane < 24, pltpu.roll(lo.astype(F32), 16, 1),
             jnp.where(lane == 24, 1.0, 0.0)))).astype(BF16)
    ko_ref[...] = (kf + _dot(packed, pk_ref[...])).astype(BF16)

    q_t = qf.T
    c_hi, c_mid, c_lo = (p.astype(F32) for p in _split3(c.T))
    m_hi, m_mid, m_lo = (p.astype(F32) for p in _split3(m.T))
    sub = lax.broadcasted_iota(I32, (SUBLANE, tm), 0)
    for h in range(ATT_HEADS):
        bias_a = jnp.where(sub == 0, c_hi[h:h + 1, :],
                 jnp.where(sub == 1, c_mid[h:h + 1, :],
                 jnp.where(sub == 2, c_lo[h:h + 1, :],
                 jnp.where(sub < 6, 1.0,
                 jnp.where(sub == 6, -m_hi[h:h + 1, :], -m_mid[h:h + 1, :])))))
        bias_b = jnp.where(sub == 0, -m_lo[h:h + 1, :], 0.0)
        blk = jnp.concatenate([q_t[h * hd:(h + 1) * hd, :], bias_a, bias_b,
                               jnp.zeros((hw - hd - ATT_BIAS_ROWS, tm), F32)], axis=0)
        qo_ref[h * hw:(h + 1) * hw, :] = blk.astype(BF16)

    v_t = v_ref[...].astype(F32).T
    ones_row = jnp.where(sub == 0, 1.0, 0.0)
    for h in range(ATT_HEADS):
        blk = jnp.concatenate([v_t[h * hd:(h + 1) * hd, :], ones_row, jnp.zeros((SUBLANE, tm), F32)], axis=0)
        vo_ref[h, 0] = blk.astype(BF16)


def _key_bias_placement():
    hw = 2 * ATT_HEAD_DIM
    rows, cols, vals = [], [], []
    for h in range(ATT_HEADS):
        base = h * hw + ATT_HEAD_DIM
        for piece in range(3):
            rows.append(24); cols.append(base + piece); vals.append(1.0)
            rows.append(8 * piece + h); cols.append(base + 3 + piece); vals.append(-1.0)
            rows.append(24); cols.append(base + 6 + piece); vals.append(1.0)
    pk = jnp.zeros((LANE, ATT_HEADS * hw), F32).at[jnp.array(rows), jnp.array(cols)].set(jnp.array(vals))
    return pk.astype(BF16)


def _head_grouping(width, per_head):
    return (jnp.arange(width)[:, None] // per_head == jnp.arange(LANE)[None, :]).astype(BF16)


def _attn_prep(proj, small, cols, b_forget, batch, lp, pad, tk):
    tp = proj.shape[0]
    hw = 2 * ATT_HEAD_DIM
    kw = ATT_HEADS * hw
    aw = ATT_HEADS * ATT_HEAD_DIM
    tm = tk
    tiles_per_seq = lp // tm
    tri = jnp.tril(jnp.ones((tm, tm), F32)).astype(BF16)
    bf = jnp.zeros((1, LANE), F32).at[0, :ATT_HEADS].set(b_forget)
    kern = functools.partial(_attn_prep_kernel, tm=tm, tiles_per_seq=tiles_per_seq, pad=pad)
    return pl.pallas_call(
        kern,
        grid=(tp // tm,),
        in_specs=[pl.BlockSpec((tm, kw), lambda i: (i, cols["k"] // kw)),
                  pl.BlockSpec((tm, aw), lambda i: (i, cols["q"] // aw)),
                  pl.BlockSpec((tm, aw), lambda i: (i, cols["v"] // aw)),
                  pl.BlockSpec((tm, LANE), lambda i: (i, 0)),
                  pl.BlockSpec((1, LANE), lambda i: (0, 0)),
                  pl.BlockSpec((tm, tm), lambda i: (0, 0)),
                  pl.BlockSpec((LANE, kw), lambda i: (0, 0)),
                  pl.BlockSpec((kw, LANE), lambda i: (0, 0)),
                  pl.BlockSpec((aw, LANE), lambda i: (0, 0))],
        out_specs=[pl.BlockSpec((tm, kw), lambda i: (i, 0)),
                   pl.BlockSpec((kw, tm), lambda i: (i // tiles_per_seq, i % tiles_per_seq)),
                   pl.BlockSpec((ATT_HEADS, 1, ATT_V_ROWS, tm),
                                lambda i: (i // tiles_per_seq, i % tiles_per_seq, 0, 0))],
        out_shape=[jax.ShapeDtypeStruct((tp, kw), BF16),
                   jax.ShapeDtypeStruct((batch * kw, lp), BF16),
                   jax.ShapeDtypeStruct((batch * ATT_HEADS, tiles_per_seq, ATT_V_ROWS, tm), BF16)],
        scratch_shapes=[pltpu.VMEM((SUBLANE, LANE), F32)],
        compiler_params=_cparams("arbitrary"),
        name="attn_prep",
    )(proj, proj, proj, small, bf, tri, _key_bias_placement(), _head_grouping(kw, hw),
      _head_grouping(aw, ATT_HEAD_DIM))


ATT_HEADS_PER_STEP = 8
ATT_MIN_ROW_SUM = 2.0 ** -80


def _attn_kernel(qt_ref, k_ref, vt_ref, o_ref, m_ref, acc_ref, *, tq, tk, pad, heads):
    i = pl.program_id(2)
    r = tq // tk
    hd = ATT_HEAD_DIM
    hw = 2 * hd
    vr = ATT_V_ROWS

    def scores(j, hh, masked):
        k = k_ref[pl.ds(pl.multiple_of(j * tk, tk), tk), hh * hw:(hh + 1) * hw]
        s = _dot(k, qt_ref[hh * hw:(hh + 1) * hw, :])
        if masked:
            kpos = j * tk + lax.broadcasted_iota(I32, (tk, tq), 0)
            qpos = i * tq + lax.broadcasted_iota(I32, (tk, tq), 1)
            s = jnp.where((kpos <= qpos) & (kpos >= pad), s, NEG_INF)
        return s

    def fast_step(j, masked):
        s_next = scores(j, 0, masked)
        for hh in range(heads):
            s = s_next
            if hh + 1 < heads:
                s_next = scores(j, hh + 1, masked)
            p = jnp.exp2(s).astype(BF16)
            acc_ref[hh * vr:(hh + 1) * vr, :] += _dot(vt_ref[hh, j], p)

    def exact_step(j, masked):
        for hh in range(heads):
            s = scores(j, hh, masked)
            m_old = m_ref[hh:hh + 1, :]
            m_new = jnp.maximum(m_old, jnp.max(s, axis=0, keepdims=True))
            p = jnp.exp2(s - m_new).astype(BF16)
            acc_ref[hh * vr:(hh + 1) * vr, :] = (jnp.exp2(m_old - m_new) * acc_ref[hh * vr:(hh + 1) * vr, :]
                                                 + _dot(vt_ref[hh, j], p))
            m_ref[hh:hh + 1, :] = m_new

    def sweep(step):
        def masked_body(j, c):
            step(j, True)
            return c

        def plain_body(j, c):
            step(j, False)
            return c

        acc_ref[...] = jnp.zeros_like(acc_ref)
        step(0, True)
        lax.fori_loop(1, r * i, plain_body, 0)
        lax.fori_loop(jnp.maximum(r * i, 1), r * i + r, masked_body, 0)

    sweep(fast_step)
    real = i * tq + lax.broadcasted_iota(I32, (1, tq), 1) >= pad
    smallest = jnp.min(jnp.concatenate(
        [jnp.where(real, acc_ref[hh * vr + hd:hh * vr + hd + 1, :], 1.0) for hh in range(heads)], axis=0))

    @pl.when(smallest < ATT_MIN_ROW_SUM)
    def _():
        m_ref[...] = jnp.full_like(m_ref, NEG_INF)
        sweep(exact_step)

    for hh in range(heads):
        l = acc_ref[hh * vr + hd:hh * vr + hd + 1, :]
        l = jnp.where(l > 0.0, l, 1.0)
        o_ref[hh * hd:(hh + 1) * hd, :] = (acc_ref[hh * vr:hh * vr + hd, :] / l).astype(BF16)


def _attention(q_t, ka, v_t, batch, lp, pad, tq, tk):
    hd = ATT_HEAD_DIM
    hw = 2 * hd
    nh = ATT_HEADS_PER_STEP
    groups = ATT_HEADS // nh
    nq = lp // tq
    nkv = lp // tk
    kern = functools.partial(_attn_kernel, tq=tq, tk=tk, pad=pad, heads=nh)
    whole_seq = dict(pipeline_mode=pl.Buffered(1))
    return pl.pallas_call(
        kern,
        grid=(batch, groups, nq),
        in_specs=[pl.BlockSpec((nh * hw, tq), lambda b, g, i: (b * groups + g, i)),
                  pl.BlockSpec((lp, nh * hw), lambda b, g, i: (b, g), **whole_seq),
                  pl.BlockSpec((nh, nkv, ATT_V_ROWS, tk), lambda b, g, i: (b * groups + g, 0, 0, 0), **whole_seq)],
        out_specs=pl.BlockSpec((nh * hd, tq), lambda b, g, i: (b * groups + g, i)),
        out_shape=jax.ShapeDtypeStruct((batch * ATT_HEADS * hd, lp), BF16),
        scratch_shapes=[pltpu.VMEM((nh, tq), F32), pltpu.VMEM((nh * ATT_V_ROWS, tq), F32)],
        compiler_params=_cparams("parallel", "parallel", "arbitrary"),
        name="fox_attention",
    )(q_t, ka, v_t)


def _ssd_kernel(x_ref, b_ref, c_ref, z_ref, dt_ref, cw_ref, cb_ref, dtb_ref, alog_ref, dexp_ref, nw_ref,
                tri_ref, e_ref, o_ref, ubuf_ref, state_ref, ybuf_ref, *, pad):
    chunk = pl.program_id(1)
    blk = SSD_CHUNK
    inner = SSD_HEADS * SSD_HEAD_DIM
    gs = SSD_GROUPS * SSD_STATE
    gw = SSD_HPG * SSD_HEAD_DIM

    @pl.when(chunk == 0)
    def _():
        ubuf_ref[0:SUBLANE, :] = jnp.zeros((SUBLANE, ubuf_ref.shape[1]), F32)
        state_ref[...] = jnp.zeros_like(state_ref)

    ubuf_ref[SUBLANE:SUBLANE + blk, 0:inner] = x_ref[...].astype(F32)
    ubuf_ref[SUBLANE:SUBLANE + blk, inner:inner + gs] = b_ref[...].astype(F32)
    ubuf_ref[SUBLANE:SUBLANE + blk, inner + gs:inner + 2 * gs] = c_ref[...].astype(F32)
    conv = cb_ref[...]
    for k in range(SSD_CONV):
        conv = conv + cw_ref[k:k + 1, :] * ubuf_ref[pl.ds(SUBLANE - (SSD_CONV - 1) + k, blk), :]
    ubuf_ref[0:SUBLANE, :] = ubuf_ref[blk:blk + SUBLANE, :]
    xc = _silu(conv)
    xs = xc[:, 0:inner]
    bm = xc[:, inner:inner + gs]
    cm = xc[:, inner + gs:inner + 2 * gs]

    row = lax.broadcasted_iota(I32, (blk, LANE), 0)
    lane = lax.broadcasted_iota(I32, (blk, LANE), 1)
    dt = _softplus(dt_ref[...] + dtb_ref[...])
    dt = jnp.where((lane < SSD_HEADS) & (chunk * blk + row >= pad), dt, 0.0)
    a = -jnp.exp(alog_ref[...])
    tri = tri_ref[...]
    hi, mid, lo = _split3(dt * a)
    a_cum = _dot(tri, hi) + _dot(tri, mid) + _dot(tri, lo)
    a_last = a_cum[blk - 1:blk, :]
    ea = jnp.exp(a_cum)
    wgt = jnp.exp(a_last - a_cum) * dt

    def pieces(v, at):
        v_hi = v.astype(BF16).astype(F32)
        v_lo = v - v_hi
        return (pltpu.roll(v_hi, at, 1) if at else v_hi), pltpu.roll(v_lo, at + SSD_HEADS, 1)

    dt_hi, dt_lo = pieces(dt, 0)
    ea_hi, ea_lo = pieces(ea, 2 * SSD_HEADS)
    wg_hi, wg_lo = pieces(wgt, 4 * SSD_HEADS)
    h16 = SSD_HEADS
    packed = jnp.where(lane < h16, dt_hi,
             jnp.where(lane < 2 * h16, dt_lo,
             jnp.where(lane < 3 * h16, ea_hi,
             jnp.where(lane < 4 * h16, ea_lo,
             jnp.where(lane < 5 * h16, wg_hi,
             jnp.where(lane < 6 * h16, wg_lo, 0.0)))))).astype(BF16)
    expanded = _dot(packed, e_ref[...])
    dt_exp = expanded[:, 0:inner]
    ea_exp = expanded[:, inner:2 * inner]
    wg_exp = expanded[:, 2 * inner:3 * inner]
    xdt = (xs * dt_exp).astype(BF16)
    xw = (xs * wg_exp).astype(BF16)

    a_cum_t = a_cum.T
    causal = row >= lane
    for g in range(SSD_GROUPS):
        cg = cm[:, g * SSD_STATE:(g + 1) * SSD_STATE].astype(BF16)
        bg_f32 = bm[:, g * SSD_STATE:(g + 1) * SSD_STATE]
        cbg = _dot_nt(cg, bg_f32.astype(BF16))
        mats = []
        for hh in range(SSD_HPG):
            h = g * SSD_HPG + hh
            diff = a_cum[:, h:h + 1] - a_cum_t[h:h + 1, :]
            decay = jnp.exp(jnp.where(causal, diff, NEG_INF))
            mats.append((cbg * decay).astype(BF16))
        for j in range(SSD_HPG // 2):
            col = g * gw + j * LANE
            xp = xdt[:, col:col + LANE]
            y0 = _dot(mats[2 * j], xp)
            y1 = _dot(mats[2 * j + 1], xp)
            ybuf_ref[:, col:col + LANE] = jnp.where(lane < SSD_HEAD_DIM, y0, y1)
        st = state_ref[g]
        y_off = _dot(cg, st.astype(BF16))
        ybuf_ref[:, g * gw:(g + 1) * gw] += y_off * ea_exp[:, g * gw:(g + 1) * gw]
        chunk_decay = ea_exp[blk - 1:blk, g * gw:(g + 1) * gw]
        state_ref[g] = st * chunk_decay + _dot(bg_f32.T.astype(BF16), xw[:, g * gw:(g + 1) * gw])

    y = ybuf_ref[...] + dexp_ref[...] * xs
    u = y * _silu(z_ref[...].astype(F32))
    for g in range(SSD_GROUPS):
        ug = u[:, g * gw:(g + 1) * gw]
        ug = ug * lax.rsqrt(jnp.mean(ug * ug, axis=-1, keepdims=True) + RMS_EPS)
        o_ref[:, g * gw:(g + 1) * gw] = (ug * nw_ref[:, g * gw:(g + 1) * gw]).astype(BF16)


def _ssd_expand_matrix():
    inner = SSD_HEADS * SSD_HEAD_DIM
    rows = jnp.arange(LANE)
    cols = jnp.arange(3 * inner)
    seg = cols // inner
    head = (cols % inner) // SSD_HEAD_DIM
    hit = (rows[:, None] // (2 * SSD_HEADS) == seg[None, :]) & (rows[:, None] % SSD_HEADS == head[None, :]) \
          & (rows[:, None] < 6 * SSD_HEADS)
    return hit.astype(BF16)


def _ssd(proj, small, cols, conv_w, conv_b, dt_bias, a_log, d, norm_w, batch, lp, pad):
    tp = proj.shape[0]
    blk = SSD_CHUNK
    inner = SSD_HEADS * SSD_HEAD_DIM
    gs = SSD_GROUPS * SSD_STATE
    nc = lp // blk
    cdim = inner + 2 * gs
    cw = jnp.zeros((SUBLANE, cdim), F32).at[:SSD_CONV].set(conv_w)
    pad_lane = lambda v: jnp.zeros((1, LANE), F32).at[0, :v.shape[0]].set(v)
    dexp = jnp.repeat(d, SSD_HEAD_DIM).reshape(1, inner)
    tri = jnp.tril(jnp.ones((blk, blk), F32)).astype(BF16)
    rowblk = lambda b, c: b * nc + c
    const = lambda shape: pl.BlockSpec(shape, lambda b, c: (0,) * len(shape))
    return pl.pallas_call(
        functools.partial(_ssd_kernel, pad=pad),
        grid=(batch, nc),
        in_specs=[pl.BlockSpec((blk, inner), lambda b, c: (rowblk(b, c), cols["x"] // inner)),
                  pl.BlockSpec((blk, gs), lambda b, c: (rowblk(b, c), cols["B"] // gs)),
                  pl.BlockSpec((blk, gs), lambda b, c: (rowblk(b, c), cols["C"] // gs)),
                  pl.BlockSpec((blk, inner), lambda b, c: (rowblk(b, c), cols["z"] // inner)),
                  pl.BlockSpec((blk, LANE), lambda b, c: (rowblk(b, c), 1)),
                  const((SUBLANE, cdim)), const((1, cdim)), const((1, LANE)), const((1, LANE)),
                  const((1, inner)), const((1, inner)), const((blk, blk)), const((LANE, 3 * inner))],
        out_specs=pl.BlockSpec((blk, inner), lambda b, c: (rowblk(b, c), 0)),
        out_shape=jax.ShapeDtypeStruct((tp, inner), BF16),
        scratch_shapes=[pltpu.VMEM((blk + SUBLANE, cdim), F32),
                        pltpu.VMEM((SSD_GROUPS, SSD_STATE, SSD_HPG * SSD_HEAD_DIM), F32),
                        pltpu.VMEM((blk, inner), F32)],
        compiler_params=_cparams("parallel", "arbitrary"),
        name="ssd",
    )(proj, proj, proj, proj, small, cw, conv_b.reshape(1, cdim), pad_lane(dt_bias), pad_lane(a_log),
      dexp, norm_w.reshape(1, inner), tri, _ssd_expand_matrix())


def _merge_kernel(yat_ref, ys_ref, sb_ref, sc_ref, sh_ref, sch_ref, shh_ref, ga_ref, gs_ref, gc_ref, h_ref,
                  wpa_ref, wps_ref, wpc_ref, wo_ref, scw_ref, g_ref, b_ref, *rest,
                  tm, tiles_per_seq, pad, alpha, with_router):
    if with_router:
        rwh_ref, rwl_ref, rb_ref, o_ref, ri_ref, rw_ref, vbuf_ref = rest
    else:
        o_ref, vbuf_ref = rest
    vbuf_ref[0:SUBLANE, :] = sch_ref[...].astype(F32) * shh_ref[...].astype(F32)
    vbuf_ref[SUBLANE:SUBLANE + tm, :] = sc_ref[...].astype(F32) * sh_ref[...].astype(F32)
    conv = jnp.zeros((tm, vbuf_ref.shape[1]), F32)
    for k in range(SC_CONV):
        conv = conv + scw_ref[k:k + 1, :] * vbuf_ref[pl.ds(SUBLANE - (SC_CONV - 1) + k, tm), :]
    y_conv = (sb_ref[...].astype(F32) * conv).astype(BF16)
    y_att = yat_ref[...].astype(F32).T.astype(BF16)

    merged = jax.nn.sigmoid(ga_ref[...].astype(F32)) * _dot(y_att, wpa_ref[...])
    merged += jax.nn.sigmoid(gs_ref[...].astype(F32)) * _dot(ys_ref[...], wps_ref[...])
    merged += jax.nn.sigmoid(gc_ref[...].astype(F32)) * _dot(y_conv, wpc_ref[...])
    mix = _dot(merged.astype(BF16), wo_ref[...])
    hn = _ln(alpha * h_ref[...] + mix, g_ref[...], b_ref[...])
    hn = _zero_pad_rows(hn, pl.program_id(0), tm, tiles_per_seq, pad)
    o_ref[...] = hn

    if with_router:
        lane = lax.broadcasted_iota(I32, (tm, LANE), 1)
        lane_f = lane.astype(F32)
        logits = _dot_x3(hn, rwh_ref[...], rwl_ref[...]) + rb_ref[...]
        lg = jnp.where(lane < N_EXPERTS, logits, -jnp.inf)
        v1 = jnp.max(lg, axis=1, keepdims=True)
        i1 = jnp.min(jnp.where(lg == v1, lane_f, float(LANE)), axis=1, keepdims=True).astype(I32)
        lg2 = jnp.where(lane == i1, -jnp.inf, lg)
        v2 = jnp.max(lg2, axis=1, keepdims=True)
        i2 = jnp.min(jnp.where(lg2 == v2, lane_f, float(LANE)), axis=1, keepdims=True).astype(I32)
        e = jnp.exp(v2 - v1)
        w1 = 1.0 / (1.0 + e)
        w2 = e / (1.0 + e)
        ri_ref[...] = jnp.where(lane == 0, i1, jnp.where(lane == 1, i2, 0))
        rw_ref[...] = jnp.where(lane == 0, w1, jnp.where(lane == 1, w2, 0.0))


def _merge(y_att_t, y_ssd, proj, cols, h, wpa, wps, wpc, wo, sc_w, g, b, lp, pad, alpha, router=None):
    tp, d = h.shape
    scw = wpc.shape[0]
    aw = wpa.shape[0]
    tm = _divisor_tile(lp, 768, LANE)
    tiles_per_seq = lp // tm
    nt = tp // tm
    halo = lambda i: jnp.maximum(i * (tm // SUBLANE) - 1, 0)
    const = lambda shape: pl.BlockSpec(shape, lambda i: (0,) * len(shape))
    in_specs = [pl.BlockSpec((aw, tm), lambda i: (i // tiles_per_seq, i % tiles_per_seq)),
                pl.BlockSpec((tm, y_ssd.shape[1]), lambda i: (i, 0)),
                pl.BlockSpec((tm, scw), lambda i: (i, cols["sc_b"] // scw)),
                pl.BlockSpec((tm, scw), lambda i: (i, cols["sc_c"] // scw)),
                pl.BlockSpec((tm, scw), lambda i: (i, cols["sc_h"] // scw)),
                pl.BlockSpec((SUBLANE, scw), lambda i: (halo(i), cols["sc_c"] // scw)),
                pl.BlockSpec((SUBLANE, scw), lambda i: (halo(i), cols["sc_h"] // scw)),
                pl.BlockSpec((tm, d), lambda i: (i, cols["g_att"] // d)),
                pl.BlockSpec((tm, d), lambda i: (i, cols["g_ssd"] // d)),
                pl.BlockSpec((tm, d), lambda i: (i, cols["g_conv"] // d)),
                pl.BlockSpec((tm, d), lambda i: (i, 0)),
                const(wpa.shape), const(wps.shape), const(wpc.shape), const(wo.shape),
                const((SUBLANE, scw)), const((1, d)), const((1, d))]
    args = [y_att_t, y_ssd, proj, proj, proj, proj, proj, proj, proj, proj, h, wpa, wps, wpc, wo,
            jnp.zeros((SUBLANE, scw), F32).at[:SC_CONV].set(sc_w), g.reshape(1, d), b.reshape(1, d)]
    out_specs = [pl.BlockSpec((tm, d), lambda i: (i, 0))]
    out_shape = [jax.ShapeDtypeStruct((tp, d), F32)]
    if router is not None:
        rw_hi, rw_lo, rb = router
        in_specs += [const(rw_hi.shape), const(rw_lo.shape), const((1, LANE))]
        args += [rw_hi, rw_lo, rb]
        out_specs += [pl.BlockSpec((tm, LANE), lambda i: (i, 0)), pl.BlockSpec((tm, LANE), lambda i: (i, 0))]
        out_shape += [jax.ShapeDtypeStruct((tp, LANE), I32), jax.ShapeDtypeStruct((tp, LANE), F32)]
    kern = functools.partial(_merge_kernel, tm=tm, tiles_per_seq=tiles_per_seq, pad=pad, alpha=alpha,
                             with_router=router is not None)
    return pl.pallas_call(
        kern, grid=(nt,), in_specs=in_specs, out_specs=out_specs, out_shape=out_shape,
        scratch_shapes=[pltpu.VMEM((tm + SUBLANE, scw), F32)],
        compiler_params=_cparams("parallel"),
        name="merge_out",
    )(*args)


def _ffn_kernel(h_ref, wg_ref, wu_ref, wd_ref, g_ref, b_ref, o_ref, *, tm, tf, tiles_per_seq, pad, alpha):
    h = h_ref[...]
    xb = h.astype(BF16)
    acc = jnp.zeros_like(h)
    for c in range(wd_ref.shape[0] // tf):
        cols = slice(c * tf, (c + 1) * tf)
        act = (_silu(_dot(xb, wg_ref[:, cols])) * _dot(xb, wu_ref[:, cols])).astype(BF16)
        acc = acc + _dot(act, wd_ref[cols, :])
    hn = _ln(alpha * h + acc, g_ref[...], b_ref[...])
    o_ref[...] = _zero_pad_rows(hn, pl.program_id(0), tm, tiles_per_seq, pad)


def _ffn(h, w_gu, w_down, g, b, lp, pad, alpha):
    tp, d = h.shape
    ff = w_down.shape[0]
    tm = _divisor_tile(lp, 768, LANE)
    tf = _divisor_tile(ff, 2 * MXU_DIM, MXU_DIM)
    kern = functools.partial(_ffn_kernel, tm=tm, tf=tf, tiles_per_seq=lp // tm, pad=pad, alpha=alpha)
    resident = dict(pipeline_mode=pl.Buffered(1))
    return pl.pallas_call(
        kern,
        grid=(tp // tm,),
        in_specs=[pl.BlockSpec((tm, d), lambda i: (i, 0)),
                  pl.BlockSpec((d, ff), lambda i: (0, 0), **resident),
                  pl.BlockSpec((d, ff), lambda i: (0, 1), **resident),
                  pl.BlockSpec((ff, d), lambda i: (0, 0), **resident),
                  pl.BlockSpec((1, d), lambda i: (0, 0)),
                  pl.BlockSpec((1, d), lambda i: (0, 0))],
        out_specs=pl.BlockSpec((tm, d), lambda i: (i, 0)),
        out_shape=jax.ShapeDtypeStruct((tp, d), F32),
        compiler_params=_cparams("parallel"),
        name="dense_ffn",
    )(h, w_gu, w_gu, w_down, g.reshape(1, d), b.reshape(1, d))


def _moe_scatter_kernel(slot_ref, h_ref, init_ref, o_ref, buf_ref, sem, *, tm):
    del init_ref
    for s in range(TOKEN_TILE_ROWS):
        buf_ref[pl.ds(s, tm, stride=TOKEN_TILE_ROWS), :] = h_ref[:, s * LANE:(s + 1) * LANE]

    def issue(r, c):
        src = buf_ref.at[pl.ds(pl.multiple_of(r * TOKEN_TILE_ROWS, TOKEN_TILE_ROWS), TOKEN_TILE_ROWS)]
        for k in range(2):
            slot = slot_ref[0, 0, 2 * r + k]
            dst = o_ref.at[pl.ds(pl.multiple_of(slot * TOKEN_TILE_ROWS, TOKEN_TILE_ROWS), TOKEN_TILE_ROWS)]
            pltpu.make_async_copy(src, dst, sem).start(priority=k)
        return c

    lax.fori_loop(0, tm, issue, 0)
    for _ in range(2):
        pltpu.make_async_copy(buf_ref, o_ref.at[pl.ds(0, tm * TOKEN_TILE_ROWS)], sem).wait()


def _moe_scatter(h, slots, n_rows, lp):
    tp, d = h.shape
    assert d == TOKEN_TILE_ROWS * LANE
    tm = _divisor_tile(lp, 768, LANE)
    nt = tp // tm
    init = jnp.zeros((n_rows * TOKEN_TILE_ROWS, LANE), F32)
    return pl.pallas_call(
        functools.partial(_moe_scatter_kernel, tm=tm),
        grid=(nt,),
        in_specs=[pl.BlockSpec((1, 1, 2 * tm), lambda i: (i, 0, 0), memory_space=pltpu.SMEM),
                  pl.BlockSpec((tm, d), lambda i: (i, 0)),
                  pl.BlockSpec(memory_space=pl.ANY)],
        out_specs=pl.BlockSpec(memory_space=pl.ANY),
        out_shape=jax.ShapeDtypeStruct(init.shape, F32),
        scratch_shapes=[pltpu.VMEM((tm * TOKEN_TILE_ROWS, LANE), F32), pltpu.SemaphoreType.DMA(())],
        input_output_aliases={2: 0},
        compiler_params=_cparams("arbitrary"),
        name="moe_scatter",
    )(slots.reshape(nt, 1, 2 * tm), h, init)


def _moe_kernel(te_ref, tv_ref, x_ref, wg_ref, wu_ref, wd_ref, o_ref, xb_ref, acc_ref, *, tm):
    n = pl.program_id(0)
    f = pl.program_id(1)

    @pl.when(f == 0)
    def _():
        for s in range(TOKEN_TILE_ROWS):
            xb_ref[:, s * LANE:(s + 1) * LANE] = x_ref[pl.ds(s, tm, stride=TOKEN_TILE_ROWS), :].astype(BF16)
        acc_ref[...] = jnp.zeros_like(acc_ref)

    @pl.when(tv_ref[n] == 1)
    def _():
        xb = xb_ref[...]
        act = (_silu(_dot(xb, wg_ref[0])) * _dot(xb, wu_ref[0])).astype(BF16)
        acc_ref[...] += _dot(act, wd_ref[0])

    @pl.when(f == pl.num_programs(1) - 1)
    def _():
        for s in range(TOKEN_TILE_ROWS):
            o_ref[pl.ds(s, tm, stride=TOKEN_TILE_ROWS), :] = acc_ref[:, s * LANE:(s + 1) * LANE]


def _moe_experts(x_sorted, tile_expert, tile_valid, w_gu, w_down, tm, n_tiles):
    d = w_down.shape[2]
    ff = w_down.shape[1]
    tf = _divisor_tile(ff, 7 * MXU_DIM, MXU_DIM)
    nf = ff // tf
    fblk = lambda f, n, tv: f * tv[n] + (nf - 1) * (1 - tv[n])
    rows = tm * TOKEN_TILE_ROWS
    grid_spec = pltpu.PrefetchScalarGridSpec(
        num_scalar_prefetch=2,
        grid=(n_tiles, nf),
        in_specs=[pl.BlockSpec((rows, LANE), lambda n, f, te, tv: (n, 0)),
                  pl.BlockSpec((1, d, tf), lambda n, f, te, tv: (te[n], 0, fblk(f, n, tv))),
                  pl.BlockSpec((1, d, tf), lambda n, f, te, tv: (te[n], 0, nf + fblk(f, n, tv))),
                  pl.BlockSpec((1, tf, d), lambda n, f, te, tv: (te[n], fblk(f, n, tv), 0))],
        out_specs=pl.BlockSpec((rows, LANE), lambda n, f, te, tv: (n, 0)),
        scratch_shapes=[pltpu.VMEM((tm, d), BF16), pltpu.VMEM((tm, d), F32)])
    return pl.pallas_call(
        functools.partial(_moe_kernel, tm=tm),
        grid_spec=grid_spec,
        out_shape=jax.ShapeDtypeStruct((n_tiles * rows, LANE), F32),
        compiler_params=_cparams("parallel", "arbitrary"),
        name="moe_experts",
    )(tile_expert, tile_valid, x_sorted, w_gu, w_gu, w_down)


def _moe_combine_kernel(slot_ref, y_ref, rw_ref, h_ref, g_ref, b_ref, o_ref, gbuf_ref, sem,
                        *, tm, tiles_per_seq, pad, alpha):
    def issue(r, c):
        for k in range(2):
            slot = slot_ref[0, 0, 2 * r + k]
            src = y_ref.at[pl.ds(pl.multiple_of(slot * TOKEN_TILE_ROWS, TOKEN_TILE_ROWS), TOKEN_TILE_ROWS)]
            dst = gbuf_ref.at[pl.ds(pl.multiple_of((k * tm + r) * TOKEN_TILE_ROWS, TOKEN_TILE_ROWS),
                                    TOKEN_TILE_ROWS)]
            pltpu.make_async_copy(src, dst, sem).start(priority=k)
        return c

    lax.fori_loop(0, tm, issue, 0, unroll=4)
    pltpu.make_async_copy(y_ref.at[pl.ds(0, 2 * tm * TOKEN_TILE_ROWS)], gbuf_ref, sem).wait()

    rw = rw_ref[...]
    w0, w1 = rw[:, 0:1], rw[:, 1:2]
    ff = jnp.concatenate(
        [w0 * gbuf_ref[pl.ds(s, tm, stride=TOKEN_TILE_ROWS), :]
         + w1 * gbuf_ref[pl.ds(tm * TOKEN_TILE_ROWS + s, tm, stride=TOKEN_TILE_ROWS), :]
         for s in range(TOKEN_TILE_ROWS)], axis=1)
    hn = _ln(alpha * h_ref[...] + ff, g_ref[...], b_ref[...])
    o_ref[...] = _zero_pad_rows(hn, pl.program_id(0), tm, tiles_per_seq, pad)


def _moe_combine(y_sorted, slots, route_w, h, g, b, lp, pad, alpha):
    tp, d = h.shape
    tm = _divisor_tile(lp, 768, LANE)
    nt = tp // tm
    kern = functools.partial(_moe_combine_kernel, tm=tm, tiles_per_seq=lp // tm, pad=pad, alpha=alpha)
    return pl.pallas_call(
        kern,
        grid=(nt,),
        in_specs=[pl.BlockSpec((1, 1, 2 * tm), lambda i: (i, 0, 0), memory_space=pltpu.SMEM),
                  pl.BlockSpec(memory_space=pl.ANY),
                  pl.BlockSpec((tm, LANE), lambda i: (i, 0)),
                  pl.BlockSpec((tm, d), lambda i: (i, 0)),
                  pl.BlockSpec((1, d), lambda i: (0, 0)),
                  pl.BlockSpec((1, d), lambda i: (0, 0))],
        out_specs=pl.BlockSpec((tm, d), lambda i: (i, 0)),
        out_shape=jax.ShapeDtypeStruct((tp, d), F32),
        scratch_shapes=[pltpu.VMEM((2 * tm * TOKEN_TILE_ROWS, LANE), F32), pltpu.SemaphoreType.DMA(())],
        compiler_params=_cparams("arbitrary"),
        name="moe_combine",
    )(slots.reshape(nt, 1, 2 * tm), y_sorted, route_w, h, g.reshape(1, d), b.reshape(1, d))


def _moe(h, route_idx, route_w, w_gu, w_down, g, b, batch, lp, pad, alpha):
    tp, d = h.shape
    n_real = batch * (lp - pad)
    tm = 640
    experts = route_idx[:, :2].reshape(tp * 2)
    seq_row = jnp.arange(tp, dtype=I32) % lp
    real = jnp.repeat(seq_row >= pad, 2)
    onehot = ((experts[:, None] == jnp.arange(N_EXPERTS, dtype=I32)[None, :]) & real[:, None]).astype(I32)
    csum = jnp.cumsum(onehot, axis=0)
    counts = csum[-1]
    rank = jnp.sum(onehot * csum, axis=1) - 1
    tiles = (counts + tm - 1) // tm
    tile_end = jnp.cumsum(tiles)
    slot = ((tile_end - tiles) * tm)[experts] + rank
    n_tiles = (2 * n_real) // tm + N_EXPERTS
    tile_ids = jnp.arange(n_tiles, dtype=I32)
    tile_expert = jnp.minimum(jnp.sum((tile_ids[:, None] >= tile_end[None, :]).astype(I32), axis=1),
                              N_EXPERTS - 1).astype(I32)
    tile_valid = (tile_ids < tile_end[-1]).astype(I32)
    n_pad = batch * pad * 2
    spare_tiles = -(-n_pad // tm)
    pad_id = jnp.repeat((jnp.arange(tp, dtype=I32) // lp) * pad + seq_row, 2) * 2 + jnp.tile(jnp.arange(2, dtype=I32), tp)
    slot_scatter = jnp.where(real, slot, n_tiles * tm + pad_id).astype(I32)
    slot_gather = jnp.where(real, slot, 0).astype(I32)

    x_sorted = _moe_scatter(h, slot_scatter, (n_tiles + spare_tiles) * tm, lp)
    y_sorted = _moe_experts(x_sorted, tile_expert, tile_valid, w_gu, w_down, tm, n_tiles)
    return _moe_combine(y_sorted, slot_gather, route_w, h, g, b, lp, pad, alpha)


def _prepare_in_proj(w_in):
    d = w_in.shape[0]
    aw = ATT_HEADS * ATT_HEAD_DIM
    inner = SSD_HEADS * SSD_HEAD_DIM
    gs = SSD_GROUPS * SSD_STATE
    sizes = [aw, aw, aw, ATT_HEADS, inner, inner + 2 * gs, SSD_HEADS]
    sc3 = w_in.shape[1] - sum(sizes) - 3 * d
    sizes += [sc3, 3 * d]
    offs = [0]
    for s in sizes:
        offs.append(offs[-1] + s)
    part = lambda k: w_in[:, offs[k]:offs[k + 1]]

    def pad_heads(w):
        w = w.reshape(d, ATT_HEADS, ATT_HEAD_DIM)
        return jnp.pad(w, ((0, 0), (0, 0), (0, ATT_HEAD_DIM))).reshape(d, 2 * aw)

    pieces = [("k", pad_heads(part(1))), ("q", part(0) * (ATT_HEAD_DIM ** -0.5 * LOG2E)), ("v", part(2)),
              ("z", part(4)), ("x", part(5)[:, :inner]), ("B", part(5)[:, inner:inner + gs]),
              ("C", part(5)[:, inner + gs:]), ("sc_b", part(7)[:, :sc3 // 3]),
              ("sc_c", part(7)[:, sc3 // 3:2 * sc3 // 3]), ("sc_h", part(7)[:, 2 * sc3 // 3:]),
              ("g_att", part(8)[:, :d]), ("g_ssd", part(8)[:, d:2 * d]), ("g_conv", part(8)[:, 2 * d:])]
    cols, off = {}, 0
    for name, w in pieces:
        cols[name] = off
        off += w.shape[1]
    w_main = jnp.concatenate([w for _, w in pieces], axis=1).astype(BF16)
    w_small = jnp.zeros((d, 2 * LANE), F32).at[:, :ATT_HEADS].set(part(3)).at[:, LANE:LANE + SSD_HEADS].set(part(6))
    ws_hi = w_small.astype(BF16)
    ws_lo = (w_small - ws_hi.astype(F32)).astype(BF16)
    return w_main, ws_hi, ws_lo, cols


def kernel(x, meta_tokens, ln_in_g, ln_in_b, w_in, b_forget, ssd_conv_w, ssd_conv_b, ssd_dt_bias, ssd_a_log,
           ssd_d, ssd_norm_w, sc_conv_w, w_proj_attn, w_proj_ssd, w_proj_conv, w_out, ln_mix_g, ln_mix_b,
           dense_w_gu, dense_w_down, router_w, router_b, moe_w_gu, moe_w_down, ln_ffn_g, ln_ffn_b):
    batch, seq, d = x.shape
    depth = w_in.shape[0]
    alpha = (2 * depth) ** 0.25
    pad = (-(N_META + seq)) % MXU_DIM
    lp = pad + N_META + seq
    tp = batch * lp
    tk = MXU_DIM
    tq = _divisor_tile(lp, 3 * MXU_DIM, MXU_DIM)

    h = _embed_ln(x, meta_tokens, ln_in_g, ln_in_b, pad).reshape(tp, d)

    for layer in range(depth):
        w_main, ws_hi, ws_lo, cols = _prepare_in_proj(w_in[layer])
        proj, small = _in_proj(h, w_main, ws_hi, ws_lo)
        ka, q_t, v_t = _attn_prep(proj, small, cols, b_forget[layer], batch, lp, pad, tk)
        y_att_t = _attention(q_t, ka, v_t, batch, lp, pad, tq, tk)
        y_ssd = _ssd(proj, small, cols, ssd_conv_w[layer], ssd_conv_b[layer], ssd_dt_bias[layer],
                     ssd_a_log[layer], ssd_d[layer], ssd_norm_w[layer], batch, lp, pad)
        j = layer // 2
        router = None
        if layer % 2 == 1:
            rw = jnp.zeros((d, LANE), F32).at[:, :N_EXPERTS].set(router_w[j])
            rw_hi = rw.astype(BF16)
            rw_lo = (rw - rw_hi.astype(F32)).astype(BF16)
            router = (rw_hi, rw_lo, jnp.zeros((1, LANE), F32).at[0, :N_EXPERTS].set(router_b[j]))
        outs = _merge(y_att_t, y_ssd, proj, cols, h, w_proj_attn[layer].astype(BF16),
                      w_proj_ssd[layer].astype(BF16), w_proj_conv[layer].astype(BF16),
                      w_out[layer].astype(BF16), sc_conv_w[layer], ln_mix_g[layer], ln_mix_b[layer],
                      lp, pad, alpha, router)
        if layer % 2 == 0:
            h = _ffn(outs[0], dense_w_gu[j].astype(BF16), dense_w_down[j].astype(BF16),
                     ln_ffn_g[layer], ln_ffn_b[layer], lp, pad, alpha)
        else:
            h_mid, route_idx, route_w = outs
            h = _moe(h_mid, route_idx, route_w, moe_w_gu[j].astype(BF16), moe_w_down[j].astype(BF16),
                     ln_ffn_g[layer], ln_ffn_b[layer], batch, lp, pad, alpha)
    return h.reshape(batch, lp, d)[:, pad + N_META:]
```

```python
import functools

import jax
import jax.numpy as jnp
from jax import lax
from jax.experimental import pallas as pl
from jax.experimental.pallas import tpu as pltpu

F32, BF16, I32 = jnp.float32, jnp.bfloat16, jnp.int32

N_META = 16
ATT_HEADS = 8
ATT_HEAD_DIM = 64
SSD_HEADS = 16
SSD_HEAD_DIM = 64
SSD_GROUPS = 2
SSD_HPG = SSD_HEADS // SSD_GROUPS
SSD_STATE = 128
SSD_CONV = 4
SSD_CHUNK = 128
SC_CONV = 3
N_EXPERTS = 8
LN_EPS = 1e-5
RMS_EPS = 1e-5
NEG_INF = -1e30
LOG2E = 1.4426950408889634

LANE = 128
SUBLANE = 8
MXU_DIM = 256
VMEM_LIMIT = 56 * 1024 * 1024
TOKEN_TILE_ROWS = SUBLANE


def _cparams(*sem):
    return pltpu.CompilerParams(dimension_semantics=tuple(sem), vmem_limit_bytes=VMEM_LIMIT)


def _divisor_tile(n, target, quantum):
    best = None
    t = quantum
    while t <= min(n, target):
        if n % t == 0:
            best = t
        t += quantum
    assert best is not None, (n, target, quantum)
    return best


def _dot(a, b):
    return jnp.dot(a, b, preferred_element_type=F32)


def _dot_nt(a, b):
    return lax.dot_general(a, b, (((1,), (1,)), ((), ())), preferred_element_type=F32)


def _split2(x):
    hi = x.astype(BF16)
    lo = (x - hi.astype(F32)).astype(BF16)
    return hi, lo


def _split3(x):
    hi = x.astype(BF16)
    r = x - hi.astype(F32)
    mid = r.astype(BF16)
    lo = (r - mid.astype(F32)).astype(BF16)
    return hi, mid, lo


def _dot_x3(a_f32, w_hi, w_lo):
    a_hi, a_lo = _split2(a_f32)
    return _dot(a_hi, w_hi) + _dot(a_hi, w_lo) + _dot(a_lo, w_hi)


def _ln(v, g, b):
    mu = jnp.mean(v, axis=-1, keepdims=True)
    c = v - mu
    var = jnp.mean(c * c, axis=-1, keepdims=True)
    return c * lax.rsqrt(var + LN_EPS) * g + b


def _silu(v):
    return v * jax.nn.sigmoid(v)


def _softplus(v):
    return jnp.maximum(v, 0.0) + jnp.log(1.0 + jnp.exp(-jnp.abs(v)))


def _zero_pad_rows(v, tile_index, tm, tiles_per_seq, pad):
    row = (tile_index % tiles_per_seq) * tm + lax.broadcasted_iota(I32, (tm, 1), 0)
    return jnp.where(row >= pad, v, 0.0)


def _embed_ln_kernel(x_ref, meta_ref, g_ref, b_ref, o_ref, *, pad):
    j = pl.program_id(1)

    @pl.when(j == 0)
    def _():
        meta = _ln(meta_ref[...].astype(F32), g_ref[...], b_ref[...])
        o_ref[0] = jnp.concatenate([jnp.zeros((pad, meta.shape[1]), F32), meta], axis=0)

    @pl.when(j > 0)
    def _():
        o_ref[0] = _ln(x_ref[0].astype(F32), g_ref[...], b_ref[...])


def _embed_ln(x, meta, g, b, pad):
    batch, seq, d = x.shape
    head = pad + N_META
    assert pad % SUBLANE == 0 and seq % head == 0, (pad, seq)
    return pl.pallas_call(
        functools.partial(_embed_ln_kernel, pad=pad),
        grid=(batch, 1 + seq // head),
        in_specs=[pl.BlockSpec((1, head, d), lambda bi, j: (bi, jnp.maximum(j - 1, 0), 0)),
                  pl.BlockSpec((N_META, d), lambda bi, j: (0, 0)),
                  pl.BlockSpec((1, d), lambda bi, j: (0, 0)),
                  pl.BlockSpec((1, d), lambda bi, j: (0, 0))],
        out_specs=pl.BlockSpec((1, head, d), lambda bi, j: (bi, j, 0)),
        out_shape=jax.ShapeDtypeStruct((batch, head + seq, d), F32),
        compiler_params=_cparams("parallel", "arbitrary"),
        name="embed_ln",
    )(x, meta.astype(x.dtype), g.reshape(1, d), b.reshape(1, d))


def _in_proj_kernel(x_ref, w_ref, wsh_ref, wsl_ref, o_ref, os_ref, xb_ref):
    @pl.when(pl.program_id(1) == 0)
    def _():
        x = x_ref[...]
        xb_ref[...] = x.astype(BF16)
        os_ref[...] = _dot_x3(x, wsh_ref[...], wsl_ref[...])

    o_ref[...] = _dot(xb_ref[...], w_ref[...]).astype(BF16)


def _in_proj(h, w_main, ws_hi, ws_lo):
    tp, d = h.shape
    n = w_main.shape[1]
    ns = ws_hi.shape[1]
    tm = _divisor_tile(tp, 1280, LANE)
    tn = _divisor_tile(n, 9 * MXU_DIM, MXU_DIM)
    return pl.pallas_call(
        _in_proj_kernel,
        grid=(tp // tm, n // tn),
        in_specs=[pl.BlockSpec((tm, d), lambda i, j: (i, 0)),
                  pl.BlockSpec((d, tn), lambda i, j: (0, j)),
                  pl.BlockSpec((d, ns), lambda i, j: (0, 0)),
                  pl.BlockSpec((d, ns), lambda i, j: (0, 0))],
        out_specs=[pl.BlockSpec((tm, tn), lambda i, j: (i, j)),
                   pl.BlockSpec((tm, ns), lambda i, j: (i, 0))],
        out_shape=[jax.ShapeDtypeStruct((tp, n), BF16),
                   jax.ShapeDtypeStruct((tp, ns), F32)],
        scratch_shapes=[pltpu.VMEM((tm, d), BF16)],
        compiler_params=_cparams("parallel", "arbitrary"),
        name="in_proj",
    )(h, w_main, ws_hi, ws_lo)


ATT_BIAS_ROWS = 2 * SUBLANE
ATT_V_ROWS = ATT_HEAD_DIM + 2 * SUBLANE
ATT_BOUND_SLACK = (1.02, 1.0)


def _attn_prep_kernel(k_ref, q_ref, v_ref, f_ref, bf_ref, tri_ref, pk_ref, gk_ref, gq_ref,
                      ko_ref, qo_ref, vo_ref, carry_ref, *, tm, tiles_per_seq, pad):
    t = pl.program_id(0) % tiles_per_seq
    hd = ATT_HEAD_DIM
    hw = 2 * hd

    @pl.when(t == 0)
    def _():
        carry_ref[...] = jnp.zeros_like(carry_ref)

    x = f_ref[...] + bf_ref[...]
    log_f = -_softplus(-x) * LOG2E
    row = t * tm + lax.broadcasted_iota(I32, (tm, LANE), 0)
    lane = lax.broadcasted_iota(I32, (tm, LANE), 1)
    log_f = jnp.where((row >= pad) & (lane < ATT_HEADS), log_f, 0.0)
    tri = tri_ref[...]
    hi, mid, lo = _split3(log_f)
    c = _dot(tri, hi) + _dot(tri, mid) + _dot(tri, lo) + carry_ref[0:1, :]
    carry_ref[0:1, :] = c[tm - 1:tm, :]

    kf = k_ref[...].astype(F32)
    qf = q_ref[...].astype(F32)
    k_max2 = jnp.maximum(carry_ref[1:2, :], jnp.max(_dot((kf * kf).astype(BF16), gk_ref[...]), axis=0, keepdims=True))
    carry_ref[1:2, :] = k_max2
    q_norm2 = _dot((qf * qf).astype(BF16), gq_ref[...])
    m = jnp.sqrt(q_norm2 * k_max2) * ATT_BOUND_SLACK[0] + ATT_BOUND_SLACK[1]

    hi, mid, lo = _split3(c)
    packed = jnp.where(lane < 8, hi.astype(F32),
             jnp.where(lane < 16, pltpu.roll(mid.astype(F32), 8, 1),
             jnp.where(lane < 24, pltpu.roll(lo.astype(F32), 16, 1),
             jnp.where(lane == 24, 1.0, 0.0)))).astype(BF16)
    ko_ref[...] = (kf + _dot(packed, pk_ref[...])).astype(BF16)

    q_t = qf.T
    c_hi, c_mid, c_lo = (p.astype(F32) for p in _split3(c.T))
    m_hi, m_mid, m_lo = (p.astype(F32) for p in _split3(m.T))
    sub = lax.broadcasted_iota(I32, (SUBLANE, tm), 0)
    for h in range(ATT_HEADS):
        bias_a = jnp.where(sub == 0, c_hi[h:h + 1, :],
                 jnp.where(sub == 1, c_mid[h:h + 1, :],
                 jnp.where(sub == 2, c_lo[h:h + 1, :],
                 jnp.where(sub < 6, 1.0,
                 jnp.where(sub == 6, -m_hi[h:h + 1, :], -m_mid[h:h + 1, :])))))
        bias_b = jnp.where(sub == 0, -m_lo[h:h + 1, :], 0.0)
        blk = jnp.concatenate([q_t[h * hd:(h + 1) * hd, :], bias_a, bias_b,
                               jnp.zeros((hw - hd - ATT_BIAS_ROWS, tm), F32)], axis=0)
        qo_ref[h * hw:(h + 1) * hw, :] = blk.astype(BF16)

    v_t = v_ref[...].astype(F32).T
    ones_row = jnp.where(sub == 0, 1.0, 0.0)
    for h in range(ATT_HEADS):
        blk = jnp.concatenate([v_t[h * hd:(h + 1) * hd, :], ones_row, jnp.zeros((SUBLANE, tm), F32)], axis=0)
        vo_ref[h, 0] = blk.astype(BF16)


def _key_bias_placement():
    hw = 2 * ATT_HEAD_DIM
    rows, cols, vals = [], [], []
    for h in range(ATT_HEADS):
        base = h * hw + ATT_HEAD_DIM
        for piece in range(3):
            rows.append(24); cols.append(base + piece); vals.append(1.0)
            rows.append(8 * piece + h); cols.append(base + 3 + piece); vals.append(-1.0)
            rows.append(24); cols.append(base + 6 + piece); vals.append(1.0)
    pk = jnp.zeros((LANE, ATT_HEADS * hw), F32).at[jnp.array(rows), jnp.array(cols)].set(jnp.array(vals))
    return pk.astype(BF16)


def _head_grouping(width, per_head):
    return (jnp.arange(width)[:, None] // per_head == jnp.arange(LANE)[None, :]).astype(BF16)


def _attn_prep(proj, small, cols, b_forget, batch, lp, pad, tk):
    tp = proj.shape[0]
    hw = 2 * ATT_HEAD_DIM
    kw = ATT_HEADS * hw
    aw = ATT_HEADS * ATT_HEAD_DIM
    tm = tk
    tiles_per_seq = lp // tm
    tri = jnp.tril(jnp.ones((tm, tm), F32)).astype(BF16)
    bf = jnp.zeros((1, LANE), F32).at[0, :ATT_HEADS].set(b_forget)
    kern = functools.partial(_attn_prep_kernel, tm=tm, tiles_per_seq=tiles_per_seq, pad=pad)
    return pl.pallas_call(
        kern,
        grid=(tp // tm,),
        in_specs=[pl.BlockSpec((tm, kw), lambda i: (i, cols["k"] // kw)),
                  pl.BlockSpec((tm, aw), lambda i: (i, cols["q"] // aw)),
                  pl.BlockSpec((tm, aw), lambda i: (i, cols["v"] // aw)),
                  pl.BlockSpec((tm, LANE), lambda i: (i, 0)),
                  pl.BlockSpec((1, LANE), lambda i: (0, 0)),
                  pl.BlockSpec((tm, tm), lambda i: (0, 0)),
                  pl.BlockSpec((LANE, kw), lambda i: (0, 0)),
                  pl.BlockSpec((kw, LANE), lambda i: (0, 0)),
                  pl.BlockSpec((aw, LANE), lambda i: (0, 0))],
        out_specs=[pl.BlockSpec((tm, kw), lambda i: (i, 0)),
                   pl.BlockSpec((kw, tm), lambda i: (i // tiles_per_seq, i % tiles_per_seq)),
                   pl.BlockSpec((ATT_HEADS, 1, ATT_V_ROWS, tm),
                                lambda i: (i // tiles_per_seq, i % tiles_per_seq, 0, 0))],
        out_shape=[jax.ShapeDtypeStruct((tp, kw), BF16),
                   jax.ShapeDtypeStruct((batch * kw, lp), BF16),
                   jax.ShapeDtypeStruct((batch * ATT_HEADS, tiles_per_seq, ATT_V_ROWS, tm), BF16)],
        scratch_shapes=[pltpu.VMEM((SUBLANE, LANE), F32)],
        compiler_params=_cparams("arbitrary"),
        name="attn_prep",
    )(proj, proj, proj, small, bf, tri, _key_bias_placement(), _head_grouping(kw, hw),
      _head_grouping(aw, ATT_HEAD_DIM))


ATT_HEADS_PER_STEP = 8
ATT_MIN_ROW_SUM = 2.0 ** -80


def _attn_kernel(qt_ref, k_ref, vt_ref, o_ref, m_ref, acc_ref, *, tq, tk, pad, heads):
    i = pl.program_id(2)
    r = tq // tk
    hd = ATT_HEAD_DIM
    hw = 2 * hd
    vr = ATT_V_ROWS

    def scores(j, hh, masked):
        k = k_ref[pl.ds(pl.multiple_of(j * tk, tk), tk), hh * hw:(hh + 1) * hw]
        s = _dot(k, qt_ref[hh * hw:(hh + 1) * hw, :])
        if masked:
            kpos = j * tk + lax.broadcasted_iota(I32, (tk, tq), 0)
            qpos = i * tq + lax.broadcasted_iota(I32, (tk, tq), 1)
            s = jnp.where((kpos <= qpos) & (kpos >= pad), s, NEG_INF)
        return s

    def fast_step(j, masked):
        s_next = scores(j, 0, masked)
        for hh in range(heads):
            s = s_next
            if hh + 1 < heads:
                s_next = scores(j, hh + 1, masked)
            p = jnp.exp2(s).astype(BF16)
            acc_ref[hh * vr:(hh + 1) * vr, :] += _dot(vt_ref[hh, j], p)

    def exact_step(j, masked):
        for hh in range(heads):
            s = scores(j, hh, masked)
            m_old = m_ref[hh:hh + 1, :]
            m_new = jnp.maximum(m_old, jnp.max(s, axis=0, keepdims=True))
            p = jnp.exp2(s - m_new).astype(BF16)
            acc_ref[hh * vr:(hh + 1) * vr, :] = (jnp.exp2(m_old - m_new) * acc_ref[hh * vr:(hh + 1) * vr, :]
                                                 + _dot(vt_ref[hh, j], p))
            m_ref[hh:hh + 1, :] = m_new

    def sweep(step):
        def masked_body(j, c):
            step(j, True)
            return c

        def plain_body(j, c):
            step(j, False)
            return c

        acc_ref[...] = jnp.zeros_like(acc_ref)
        step(0, True)
        lax.fori_loop(1, r * i, plain_body, 0)
        lax.fori_loop(jnp.maximum(r * i, 1), r * i + r, masked_body, 0)

    sweep(fast_step)
    real = i * tq + lax.broadcasted_iota(I32, (1, tq), 1) >= pad
    smallest = jnp.min(jnp.concatenate(
        [jnp.where(real, acc_ref[hh * vr + hd:hh * vr + hd + 1, :], 1.0) for hh in range(heads)], axis=0))

    @pl.when(smallest < ATT_MIN_ROW_SUM)
    def _():
        m_ref[...] = jnp.full_like(m_ref, NEG_INF)
        sweep(exact_step)

    for hh in range(heads):
        l = acc_ref[hh * vr + hd:hh * vr + hd + 1, :]
        l = jnp.where(l > 0.0, l, 1.0)
        o_ref[hh * hd:(hh + 1) * hd, :] = (acc_ref[hh * vr:hh * vr + hd, :] / l).astype(BF16)


def _attention(q_t, ka, v_t, batch, lp, pad, tq, tk):
    hd = ATT_HEAD_DIM
    hw = 2 * hd
    nh = ATT_HEADS_PER_STEP
    groups = ATT_HEADS // nh
    nq = lp // tq
    nkv = lp // tk
    kern = functools.partial(_attn_kernel, tq=tq, tk=tk, pad=pad, heads=nh)
    whole_seq = dict(pipeline_mode=pl.Buffered(1))
    return pl.pallas_call(
        kern,
        grid=(batch, groups, nq),
        in_specs=[pl.BlockSpec((nh * hw, tq), lambda b, g, i: (b * groups + g, i)),
                  pl.BlockSpec((lp, nh * hw), lambda b, g, i: (b, g), **whole_seq),
                  pl.BlockSpec((nh, nkv, ATT_V_ROWS, tk), lambda b, g, i: (b * groups + g, 0, 0, 0), **whole_seq)],
        out_specs=pl.BlockSpec((nh * hd, tq), lambda b, g, i: (b * groups + g, i)),
        out_shape=jax.ShapeDtypeStruct((batch * ATT_HEADS * hd, lp), BF16),
        scratch_shapes=[pltpu.VMEM((nh, tq), F32), pltpu.VMEM((nh * ATT_V_ROWS, tq), F32)],
        compiler_params=_cparams("parallel", "parallel", "arbitrary"),
        name="fox_attention",
    )(q_t, ka, v_t)


SSD_CHUNKS_PER_STEP = 2


def _ssd_kernel(x_ref, b_ref, c_ref, z_ref, dt_ref, cw_ref, cb_ref, dtb_ref, alog_ref, dexp_ref, nw_ref,
                tri_ref, e_ref, o_ref, ubuf_ref, state_ref, ybuf_ref, *, pad):
    @pl.when(pl.program_id(1) == 0)
    def _():
        ubuf_ref[0:SUBLANE, :] = jnp.zeros((SUBLANE, ubuf_ref.shape[1]), F32)
        state_ref[...] = jnp.zeros_like(state_ref)

    for cc in range(SSD_CHUNKS_PER_STEP):
        rows = pl.ds(cc * SSD_CHUNK, SSD_CHUNK)
        _ssd_chunk(pl.program_id(1) * SSD_CHUNKS_PER_STEP + cc,
                   x_ref.at[rows], b_ref.at[rows], c_ref.at[rows], z_ref.at[rows], dt_ref.at[rows],
                   cw_ref, cb_ref, dtb_ref, alog_ref, dexp_ref, nw_ref, tri_ref, e_ref, o_ref.at[rows],
                   ubuf_ref, state_ref, ybuf_ref, pad)


def _ssd_chunk(chunk, x_ref, b_ref, c_ref, z_ref, dt_ref, cw_ref, cb_ref, dtb_ref, alog_ref, dexp_ref, nw_ref,
               tri_ref, e_ref, o_ref, ubuf_ref, state_ref, ybuf_ref, pad):
    blk = SSD_CHUNK
    inner = SSD_HEADS * SSD_HEAD_DIM
    gs = SSD_GROUPS * SSD_STATE
    gw = SSD_HPG * SSD_HEAD_DIM

    ubuf_ref[SUBLANE:SUBLANE + blk, 0:inner] = x_ref[...].astype(F32)
    ubuf_ref[SUBLANE:SUBLANE + blk, inner:inner + gs] = b_ref[...].astype(F32)
    ubuf_ref[SUBLANE:SUBLANE + blk, inner + gs:inner + 2 * gs] = c_ref[...].astype(F32)
    conv = cb_ref[...]
    for k in range(SSD_CONV):
        conv = conv + cw_ref[k:k + 1, :] * ubuf_ref[pl.ds(SUBLANE - (SSD_CONV - 1) + k, blk), :]
    ubuf_ref[0:SUBLANE, :] = ubuf_ref[blk:blk + SUBLANE, :]
    xc = _silu(conv)
    xs = xc[:, 0:inner]
    bm = xc[:, inner:inner + gs]
    cm = xc[:, inner + gs:inner + 2 * gs]

    row = lax.broadcasted_iota(I32, (blk, LANE), 0)
    lane = lax.broadcasted_iota(I32, (blk, LANE), 1)
    dt = _softplus(dt_ref[...] + dtb_ref[...])
    dt = jnp.where((lane < SSD_HEADS) & (chunk * blk + row >= pad), dt, 0.0)
    a = -jnp.exp(alog_ref[...])
    tri = tri_ref[...]
    hi, mid, lo = _split3(dt * a)
    a_cum = _dot(tri, hi) + _dot(tri, mid) + _dot(tri, lo)
    a_last = a_cum[blk - 1:blk, :]
    ea = jnp.exp(a_cum)
    wgt = jnp.exp(a_last - a_cum) * dt

    def pieces(v, at):
        v_hi = v.astype(BF16).astype(F32)
        v_lo = v - v_hi
        return (pltpu.roll(v_hi, at, 1) if at else v_hi), pltpu.roll(v_lo, at + SSD_HEADS, 1)

    dt_hi, dt_lo = pieces(dt, 0)
    ea_hi, ea_lo = pieces(ea, 2 * SSD_HEADS)
    wg_hi, wg_lo = pieces(wgt, 4 * SSD_HEADS)
    h16 = SSD_HEADS
    packed = jnp.where(lane < h16, dt_hi,
             jnp.where(lane < 2 * h16, dt_lo,
             jnp.where(lane < 3 * h16, ea_hi,
             jnp.where(lane < 4 * h16, ea_lo,
             jnp.where(lane < 5 * h16, wg_hi,
             jnp.where(lane < 6 * h16, wg_lo, 0.0)))))).astype(BF16)
    expanded = _dot(packed, e_ref[...])
    dt_exp = expanded[:, 0:inner]
    ea_exp = expanded[:, inner:2 * inner]
    wg_exp = expanded[:, 2 * inner:3 * inner]
    xdt = (xs * dt_exp).astype(BF16)
    xw = (xs * wg_exp).astype(BF16)

    a_cum_t = a_cum.T
    causal = row >= lane
    for g in range(SSD_GROUPS):
        cg = cm[:, g * SSD_STATE:(g + 1) * SSD_STATE].astype(BF16)
        bg_f32 = bm[:, g * SSD_STATE:(g + 1) * SSD_STATE]
        cbg = _dot_nt(cg, bg_f32.astype(BF16))
        mats = []
        for hh in range(SSD_HPG):
            h = g * SSD_HPG + hh
            diff = a_cum[:, h:h + 1] - a_cum_t[h:h + 1, :]
            decay = jnp.exp(jnp.where(causal, diff, NEG_INF))
            mats.append((cbg * decay).astype(BF16))
        for j in range(SSD_HPG // 2):
            col = g * gw + j * LANE
            xp = xdt[:, col:col + LANE]
            y0 = _dot(mats[2 * j], xp)
            y1 = _dot(mats[2 * j + 1], xp)
            ybuf_ref[:, col:col + LANE] = jnp.where(lane < SSD_HEAD_DIM, y0, y1)
        st = state_ref[g]
        y_off = _dot(cg, st.astype(BF16))
        ybuf_ref[:, g * gw:(g + 1) * gw] += y_off * ea_exp[:, g * gw:(g + 1) * gw]
        chunk_decay = ea_exp[blk - 1:blk, g * gw:(g + 1) * gw]
        state_ref[g] = st * chunk_decay + _dot(bg_f32.T.astype(BF16), xw[:, g * gw:(g + 1) * gw])

    y = ybuf_ref[...] + dexp_ref[...] * xs
    u = y * _silu(z_ref[...].astype(F32))
    for g in range(SSD_GROUPS):
        ug = u[:, g * gw:(g + 1) * gw]
        ug = ug * lax.rsqrt(jnp.mean(ug * ug, axis=-1, keepdims=True) + RMS_EPS)
        o_ref[:, g * gw:(g + 1) * gw] = (ug * nw_ref[:, g * gw:(g + 1) * gw]).astype(BF16)


def _ssd_expand_matrix():
    inner = SSD_HEADS * SSD_HEAD_DIM
    rows = jnp.arange(LANE)
    cols = jnp.arange(3 * inner)
    seg = cols // inner
    head = (cols % inner) // SSD_HEAD_DIM
    hit = (rows[:, None] // (2 * SSD_HEADS) == seg[None, :]) & (rows[:, None] % SSD_HEADS == head[None, :]) \
          & (rows[:, None] < 6 * SSD_HEADS)
    return hit.astype(BF16)


def _ssd(proj, small, cols, conv_w, conv_b, dt_bias, a_log, d, norm_w, batch, lp, pad):
    tp = proj.shape[0]
    blk = SSD_CHUNK
    inner = SSD_HEADS * SSD_HEAD_DIM
    gs = SSD_GROUPS * SSD_STATE
    nc = lp // blk
    cdim = inner + 2 * gs
    cw = jnp.zeros((SUBLANE, cdim), F32).at[:SSD_CONV].set(conv_w)
    pad_lane = lambda v: jnp.zeros((1, LANE), F32).at[0, :v.shape[0]].set(v)
    dexp = jnp.repeat(d, SSD_HEAD_DIM).reshape(1, inner)
    tri = jnp.tril(jnp.ones((blk, blk), F32)).astype(BF16)
    assert nc % SSD_CHUNKS_PER_STEP == 0, (lp, blk)
    steps = nc // SSD_CHUNKS_PER_STEP
    rows = SSD_CHUNKS_PER_STEP * blk
    rowblk = lambda b, c: b * steps + c
    const = lambda shape: pl.BlockSpec(shape, lambda b, c: (0,) * len(shape))
    return pl.pallas_call(
        functools.partial(_ssd_kernel, pad=pad),
        grid=(batch, steps),
        in_specs=[pl.BlockSpec((rows, inner), lambda b, c: (rowblk(b, c), cols["x"] // inner)),
                  pl.BlockSpec((rows, gs), lambda b, c: (rowblk(b, c), cols["B"] // gs)),
                  pl.BlockSpec((rows, gs), lambda b, c: (rowblk(b, c), cols["C"] // gs)),
                  pl.BlockSpec((rows, inner), lambda b, c: (rowblk(b, c), cols["z"] // inner)),
                  pl.BlockSpec((rows, LANE), lambda b, c: (rowblk(b, c), 1)),
                  const((SUBLANE, cdim)), const((1, cdim)), const((1, LANE)), const((1, LANE)),
                  const((1, inner)), const((1, inner)), const((blk, blk)), const((LANE, 3 * inner))],
        out_specs=pl.BlockSpec((rows, inner), lambda b, c: (rowblk(b, c), 0)),
        out_shape=jax.ShapeDtypeStruct((tp, inner), BF16),
        scratch_shapes=[pltpu.VMEM((blk + SUBLANE, cdim), F32),
                        pltpu.VMEM((SSD_GROUPS, SSD_STATE, SSD_HPG * SSD_HEAD_DIM), F32),
                        pltpu.VMEM((blk, inner), F32)],
        compiler_params=_cparams("parallel", "arbitrary"),
        name="ssd",
    )(proj, proj, proj, proj, small, cw, conv_b.reshape(1, cdim), pad_lane(dt_bias), pad_lane(a_log),
      dexp, norm_w.reshape(1, inner), tri, _ssd_expand_matrix())


def _merge_kernel(yat_ref, ys_ref, sb_ref, sc_ref, sh_ref, sch_ref, shh_ref, ga_ref, gs_ref, gc_ref, h_ref,
                  wpa_ref, wps_ref, wpc_ref, wo_ref, scw_ref, g_ref, b_ref, *rest,
                  tm, tiles_per_seq, pad, alpha, with_router):
    if with_router:
        rwh_ref, rwl_ref, rb_ref, o_ref, ri_ref, rw_ref, vbuf_ref = rest
    else:
        o_ref, vbuf_ref = rest
    vbuf_ref[0:SUBLANE, :] = sch_ref[...].astype(F32) * shh_ref[...].astype(F32)
    vbuf_ref[SUBLANE:SUBLANE + tm, :] = sc_ref[...].astype(F32) * sh_ref[...].astype(F32)
    conv = jnp.zeros((tm, vbuf_ref.shape[1]), F32)
    for k in range(SC_CONV):
        conv = conv + scw_ref[k:k + 1, :] * vbuf_ref[pl.ds(SUBLANE - (SC_CONV - 1) + k, tm), :]
    y_conv = (sb_ref[...].astype(F32) * conv).astype(BF16)
    y_att = yat_ref[...].astype(F32).T.astype(BF16)

    merged = jax.nn.sigmoid(ga_ref[...].astype(F32)) * _dot(y_att, wpa_ref[...])
    merged += jax.nn.sigmoid(gs_ref[...].astype(F32)) * _dot(ys_ref[...], wps_ref[...])
    merged += jax.nn.sigmoid(gc_ref[...].astype(F32)) * _dot(y_conv, wpc_ref[...])
    mix = _dot(merged.astype(BF16), wo_ref[...])
    hn = _ln(alpha * h_ref[...] + mix, g_ref[...], b_ref[...])
    hn = _zero_pad_rows(hn, pl.program_id(0), tm, tiles_per_seq, pad)
    o_ref[...] = hn

    if with_router:
        lane = lax.broadcasted_iota(I32, (tm, LANE), 1)
        lane_f = lane.astype(F32)
        logits = _dot_x3(hn, rwh_ref[...], rwl_ref[...]) + rb_ref[...]
        lg = jnp.where(lane < N_EXPERTS, logits, -jnp.inf)
        v1 = jnp.max(lg, axis=1, keepdims=True)
        i1 = jnp.min(jnp.where(lg == v1, lane_f, float(LANE)), axis=1, keepdims=True).astype(I32)
        lg2 = jnp.where(lane == i1, -jnp.inf, lg)
        v2 = jnp.max(lg2, axis=1, keepdims=True)
        i2 = jnp.min(jnp.where(lg2 == v2, lane_f, float(LANE)), axis=1, keepdims=True).astype(I32)
        e = jnp.exp(v2 - v1)
        w1 = 1.0 / (1.0 + e)
        w2 = e / (1.0 + e)
        ri_ref[...] = jnp.where(lane == 0, i1, jnp.where(lane == 1, i2, 0))
        rw_ref[...] = jnp.where(lane == 0, w1, jnp.where(lane == 1, w2, 0.0))


def _merge(y_att_t, y_ssd, proj, cols, h, wpa, wps, wpc, wo, sc_w, g, b, lp, pad, alpha, router=None):
    tp, d = h.shape
    scw = wpc.shape[0]
    aw = wpa.shape[0]
    tm = _divisor_tile(lp, 768, LANE)
    tiles_per_seq = lp // tm
    nt = tp // tm
    halo = lambda i: jnp.maximum(i * (tm // SUBLANE) - 1, 0)
    const = lambda shape: pl.BlockSpec(shape, lambda i: (0,) * len(shape))
    in_specs = [pl.BlockSpec((aw, tm), lambda i: (i // tiles_per_seq, i % tiles_per_seq)),
                pl.BlockSpec((tm, y_ssd.shape[1]), lambda i: (i, 0)),
                pl.BlockSpec((tm, scw), lambda i: (i, cols["sc_b"] // scw)),
                pl.BlockSpec((tm, scw), lambda i: (i, cols["sc_c"] // scw)),
                pl.BlockSpec((tm, scw), lambda i: (i, cols["sc_h"] // scw)),
                pl.BlockSpec((SUBLANE, scw), lambda i: (halo(i), cols["sc_c"] // scw)),
                pl.BlockSpec((SUBLANE, scw), lambda i: (halo(i), cols["sc_h"] // scw)),
                pl.BlockSpec((tm, d), lambda i: (i, cols["g_att"] // d)),
                pl.BlockSpec((tm, d), lambda i: (i, cols["g_ssd"] // d)),
                pl.BlockSpec((tm, d), lambda i: (i, cols["g_conv"] // d)),
                pl.BlockSpec((tm, d), lambda i: (i, 0)),
                const(wpa.shape), const(wps.shape), const(wpc.shape), const(wo.shape),
                const((SUBLANE, scw)), const((1, d)), const((1, d))]
    args = [y_att_t, y_ssd, proj, proj, proj, proj, proj, proj, proj, proj, h, wpa, wps, wpc, wo,
            jnp.zeros((SUBLANE, scw), F32).at[:SC_CONV].set(sc_w), g.reshape(1, d), b.reshape(1, d)]
    out_specs = [pl.BlockSpec((tm, d), lambda i: (i, 0))]
    out_shape = [jax.ShapeDtypeStruct((tp, d), F32)]
    if router is not None:
        rw_hi, rw_lo, rb = router
        in_specs += [const(rw_hi.shape), const(rw_lo.shape), const((1, LANE))]
        args += [rw_hi, rw_lo, rb]
        out_specs += [pl.BlockSpec((tm, LANE), lambda i: (i, 0)), pl.BlockSpec((tm, LANE), lambda i: (i, 0))]
        out_shape += [jax.ShapeDtypeStruct((tp, LANE), I32), jax.ShapeDtypeStruct((tp, LANE), F32)]
    kern = functools.partial(_merge_kernel, tm=tm, tiles_per_seq=tiles_per_seq, pad=pad, alpha=alpha,
                             with_router=router is not None)
    return pl.pallas_call(
        kern, grid=(nt,), in_specs=in_specs, out_specs=out_specs, out_shape=out_shape,
        scratch_shapes=[pltpu.VMEM((tm + SUBLANE, scw), F32)],
        compiler_params=_cparams("parallel"),
        name="merge_out",
    )(*args)


def _ffn_kernel(h_ref, wg_ref, wu_ref, wd_ref, g_ref, b_ref, o_ref, *, tm, tf, tiles_per_seq, pad, alpha):
    h = h_ref[...]
    xb = h.astype(BF16)
    acc = jnp.zeros_like(h)
    for c in range(wd_ref.shape[0] // tf):
        cols = slice(c * tf, (c + 1) * tf)
        act = (_silu(_dot(xb, wg_ref[:, cols])) * _dot(xb, wu_ref[:, cols])).astype(BF16)
        acc = acc + _dot(act, wd_ref[cols, :])
    hn = _ln(alpha * h + acc, g_ref[...], b_ref[...])
    o_ref[...] = _zero_pad_rows(hn, pl.program_id(0), tm, tiles_per_seq, pad)


def _ffn(h, w_gu, w_down, g, b, lp, pad, alpha):
    tp, d = h.shape
    ff = w_down.shape[0]
    tm = _divisor_tile(lp, 768, LANE)
    tf = _divisor_tile(ff, 2 * MXU_DIM, MXU_DIM)
    kern = functools.partial(_ffn_kernel, tm=tm, tf=tf, tiles_per_seq=lp // tm, pad=pad, alpha=alpha)
    resident = dict(pipeline_mode=pl.Buffered(1))
    return pl.pallas_call(
        kern,
        grid=(tp // tm,),
        in_specs=[pl.BlockSpec((tm, d), lambda i: (i, 0)),
                  pl.BlockSpec((d, ff), lambda i: (0, 0), **resident),
                  pl.BlockSpec((d, ff), lambda i: (0, 1), **resident),
                  pl.BlockSpec((ff, d), lambda i: (0, 0), **resident),
                  pl.BlockSpec((1, d), lambda i: (0, 0)),
                  pl.BlockSpec((1, d), lambda i: (0, 0))],
        out_specs=pl.BlockSpec((tm, d), lambda i: (i, 0)),
        out_shape=jax.ShapeDtypeStruct((tp, d), F32),
        compiler_params=_cparams("parallel"),
        name="dense_ffn",
    )(h, w_gu, w_gu, w_down, g.reshape(1, d), b.reshape(1, d))


def _moe_scatter_kernel(slot_ref, h_ref, init_ref, o_ref, buf_ref, sem, *, tm):
    del init_ref
    for s in range(TOKEN_TILE_ROWS):
        buf_ref[pl.ds(s, tm, stride=TOKEN_TILE_ROWS), :] = h_ref[:, s * LANE:(s + 1) * LANE]

    def issue(r, c):
        src = buf_ref.at[pl.ds(pl.multiple_of(r * TOKEN_TILE_ROWS, TOKEN_TILE_ROWS), TOKEN_TILE_ROWS)]
        for k in range(2):
            slot = slot_ref[0, 0, 2 * r + k]
            dst = o_ref.at[pl.ds(pl.multiple_of(slot * TOKEN_TILE_ROWS, TOKEN_TILE_ROWS), TOKEN_TILE_ROWS)]
            pltpu.make_async_copy(src, dst, sem).start(priority=k)
        return c

    lax.fori_loop(0, tm, issue, 0)
    for _ in range(2):
        pltpu.make_async_copy(buf_ref, o_ref.at[pl.ds(0, tm * TOKEN_TILE_ROWS)], sem).wait()


def _moe_scatter(h, slots, n_rows, lp):
    tp, d = h.shape
    assert d == TOKEN_TILE_ROWS * LANE
    tm = _divisor_tile(lp, 768, LANE)
    nt = tp // tm
    init = jnp.zeros((n_rows * TOKEN_TILE_ROWS, LANE), F32)
    return pl.pallas_call(
        functools.partial(_moe_scatter_kernel, tm=tm),
        grid=(nt,),
        in_specs=[pl.BlockSpec((1, 1, 2 * tm), lambda i: (i, 0, 0), memory_space=pltpu.SMEM),
                  pl.BlockSpec((tm, d), lambda i: (i, 0)),
                  pl.BlockSpec(memory_space=pl.ANY)],
        out_specs=pl.BlockSpec(memory_space=pl.ANY),
        out_shape=jax.ShapeDtypeStruct(init.shape, F32),
        scratch_shapes=[pltpu.VMEM((tm * TOKEN_TILE_ROWS, LANE), F32), pltpu.SemaphoreType.DMA(())],
        input_output_aliases={2: 0},
        compiler_params=_cparams("arbitrary"),
        name="moe_scatter",
    )(slots.reshape(nt, 1, 2 * tm), h, init)


def _moe_kernel(te_ref, tv_ref, x_ref, wg_ref, wu_ref, wd_ref, o_ref, xb_ref, acc_ref, *, tm):
    n = pl.program_id(0)
    f = pl.program_id(1)

    @pl.when(f == 0)
    def _():
        for s in range(TOKEN_TILE_ROWS):
            xb_ref[:, s * LANE:(s + 1) * LANE] = x_ref[pl.ds(s, tm, stride=TOKEN_TILE_ROWS), :].astype(BF16)
        acc_ref[...] = jnp.zeros_like(acc_ref)

    @pl.when(tv_ref[n] == 1)
    def _():
        xb = xb_ref[...]
        act = (_silu(_dot(xb, wg_ref[0])) * _dot(xb, wu_ref[0])).astype(BF16)
        acc_ref[...] += _dot(act, wd_ref[0])

    @pl.when(f == pl.num_programs(1) - 1)
    def _():
        for s in range(TOKEN_TILE_ROWS):
            o_ref[pl.ds(s, tm, stride=TOKEN_TILE_ROWS), :] = acc_ref[:, s * LANE:(s + 1) * LANE]


def _moe_experts(x_sorted, tile_expert, tile_valid, w_gu, w_down, tm, n_tiles):
    d = w_down.shape[2]
    ff = w_down.shape[1]
    tf = _divisor_tile(ff, 7 * MXU_DIM, MXU_DIM)
    nf = ff // tf
    fblk = lambda f, n, tv: f * tv[n] + (nf - 1) * (1 - tv[n])
    rows = tm * TOKEN_TILE_ROWS
    grid_spec = pltpu.PrefetchScalarGridSpec(
        num_scalar_prefetch=2,
        grid=(n_tiles, nf),
        in_specs=[pl.BlockSpec((rows, LANE), lambda n, f, te, tv: (n, 0)),
                  pl.BlockSpec((1, d, tf), lambda n, f, te, tv: (te[n], 0, fblk(f, n, tv))),
                  pl.BlockSpec((1, d, tf), lambda n, f, te, tv: (te[n], 0, nf + fblk(f, n, tv))),
                  pl.BlockSpec((1, tf, d), lambda n, f, te, tv: (te[n], fblk(f, n, tv), 0))],
        out_specs=pl.BlockSpec((rows, LANE), lambda n, f, te, tv: (n, 0)),
        scratch_shapes=[pltpu.VMEM((tm, d), BF16), pltpu.VMEM((tm, d), F32)])
    return pl.pallas_call(
        functools.partial(_moe_kernel, tm=tm),
        grid_spec=grid_spec,
        out_shape=jax.ShapeDtypeStruct((n_tiles * rows, LANE), F32),
        compiler_params=_cparams("parallel", "arbitrary"),
        name="moe_experts",
    )(tile_expert, tile_valid, x_sorted, w_gu, w_gu, w_down)


def _moe_combine_kernel(slot_ref, y_ref, rw_ref, h_ref, g_ref, b_ref, o_ref, gbuf_ref, sem,
                        *, tm, tiles_per_seq, pad, alpha):
    def issue(r, c):
        for k in range(2):
            slot = slot_ref[0, 0, 2 * r + k]
            src = y_ref.at[pl.ds(pl.multiple_of(slot * TOKEN_TILE_ROWS, TOKEN_TILE_ROWS), TOKEN_TILE_ROWS)]
            dst = gbuf_ref.at[pl.ds(pl.multiple_of((k * tm + r) * TOKEN_TILE_ROWS, TOKEN_TILE_ROWS),
                                    TOKEN_TILE_ROWS)]
            pltpu.make_async_copy(src, dst, sem).start(priority=k)
        return c

    lax.fori_loop(0, tm, issue, 0, unroll=4)
    pltpu.make_async_copy(y_ref.at[pl.ds(0, 2 * tm * TOKEN_TILE_ROWS)], gbuf_ref, sem).wait()

    rw = rw_ref[...]
    w0, w1 = rw[:, 0:1], rw[:, 1:2]
    ff = jnp.concatenate(
        [w0 * gbuf_ref[pl.ds(s, tm, stride=TOKEN_TILE_ROWS), :]
         + w1 * gbuf_ref[pl.ds(tm * TOKEN_TILE_ROWS + s, tm, stride=TOKEN_TILE_ROWS), :]
         for s in range(TOKEN_TILE_ROWS)], axis=1)
    hn = _ln(alpha * h_ref[...] + ff, g_ref[...], b_ref[...])
    o_ref[...] = _zero_pad_rows(hn, pl.program_id(0), tm, tiles_per_seq, pad)


def _moe_combine(y_sorted, slots, route_w, h, g, b, lp, pad, alpha):
    tp, d = h.shape
    tm = _divisor_tile(lp, 768, LANE)
    nt = tp // tm
    kern = functools.partial(_moe_combine_kernel, tm=tm, tiles_per_seq=lp // tm, pad=pad, alpha=alpha)
    return pl.pallas_call(
        kern,
        grid=(nt,),
        in_specs=[pl.BlockSpec((1, 1, 2 * tm), lambda i: (i, 0, 0), memory_space=pltpu.SMEM),
                  pl.BlockSpec(memory_space=pl.ANY),
                  pl.BlockSpec((tm, LANE), lambda i: (i, 0)),
                  pl.BlockSpec((tm, d), lambda i: (i, 0)),
                  pl.BlockSpec((1, d), lambda i: (0, 0)),
                  pl.BlockSpec((1, d), lambda i: (0, 0))],
        out_specs=pl.BlockSpec((tm, d), lambda i: (i, 0)),
        out_shape=jax.ShapeDtypeStruct((tp, d), F32),
        scratch_shapes=[pltpu.VMEM((2 * tm * TOKEN_TILE_ROWS, LANE), F32), pltpu.SemaphoreType.DMA(())],
        compiler_params=_cparams("arbitrary"),
        name="moe_combine",
    )(slots.reshape(nt, 1, 2 * tm), y_sorted, route_w, h, g.reshape(1, d), b.reshape(1, d))


def _moe(h, route_idx, route_w, w_gu, w_down, g, b, batch, lp, pad, alpha):
    tp, d = h.shape
    n_real = batch * (lp - pad)
    tm = 640
    experts = route_idx[:, :2].reshape(tp * 2)
    seq_row = jnp.arange(tp, dtype=I32) % lp
    real = jnp.repeat(seq_row >= pad, 2)
    onehot = ((experts[:, None] == jnp.arange(N_EXPERTS, dtype=I32)[None, :]) & real[:, None]).astype(I32)
    csum = jnp.cumsum(onehot, axis=0)
    counts = csum[-1]
    rank = jnp.sum(onehot * csum, axis=1) - 1
    tiles = (counts + tm - 1) // tm
    tile_end = jnp.cumsum(tiles)
    slot = ((tile_end - tiles) * tm)[experts] + rank
    n_tiles = (2 * n_real) // tm + N_EXPERTS
    tile_ids = jnp.arange(n_tiles, dtype=I32)
    tile_expert = jnp.minimum(jnp.sum((tile_ids[:, None] >= tile_end[None, :]).astype(I32), axis=1),
                              N_EXPERTS - 1).astype(I32)
    tile_valid = (tile_ids < tile_end[-1]).astype(I32)
    n_pad = batch * pad * 2
    spare_tiles = -(-n_pad // tm)
    pad_id = jnp.repeat((jnp.arange(tp, dtype=I32) // lp) * pad + seq_row, 2) * 2 + jnp.tile(jnp.arange(2, dtype=I32), tp)
    slot_scatter = jnp.where(real, slot, n_tiles * tm + pad_id).astype(I32)
    slot_gather = jnp.where(real, slot, 0).astype(I32)

    x_sorted = _moe_scatter(h, slot_scatter, (n_tiles + spare_tiles) * tm, lp)
    y_sorted = _moe_experts(x_sorted, tile_expert, tile_valid, w_gu, w_down, tm, n_tiles)
    return _moe_combine(y_sorted, slot_gather, route_w, h, g, b, lp, pad, alpha)


def _prepare_in_proj(w_in):
    d = w_in.shape[0]
    aw = ATT_HEADS * ATT_HEAD_DIM
    inner = SSD_HEADS * SSD_HEAD_DIM
    gs = SSD_GROUPS * SSD_STATE
    sizes = [aw, aw, aw, ATT_HEADS, inner, inner + 2 * gs, SSD_HEADS]
    sc3 = w_in.shape[1] - sum(sizes) - 3 * d
    sizes += [sc3, 3 * d]
    offs = [0]
    for s in sizes:
        offs.append(offs[-1] + s)
    part = lambda k: w_in[:, offs[k]:offs[k + 1]]

    def pad_heads(w):
        w = w.reshape(d, ATT_HEADS, ATT_HEAD_DIM)
        return jnp.pad(w, ((0, 0), (0, 0), (0, ATT_HEAD_DIM))).reshape(d, 2 * aw)

    pieces = [("k", pad_heads(part(1))), ("q", part(0) * (ATT_HEAD_DIM ** -0.5 * LOG2E)), ("v", part(2)),
              ("z", part(4)), ("x", part(5)[:, :inner]), ("B", part(5)[:, inner:inner + gs]),
              ("C", part(5)[:, inner + gs:]), ("sc_b", part(7)[:, :sc3 // 3]),
              ("sc_c", part(7)[:, sc3 // 3:2 * sc3 // 3]), ("sc_h", part(7)[:, 2 * sc3 // 3:]),
              ("g_att", part(8)[:, :d]), ("g_ssd", part(8)[:, d:2 * d]), ("g_conv", part(8)[:, 2 * d:])]
    cols, off = {}, 0
    for name, w in pieces:
        cols[name] = off
        off += w.shape[1]
    w_main = jnp.concatenate([w for _, w in pieces], axis=1).astype(BF16)
    w_small = jnp.zeros((d, 2 * LANE), F32).at[:, :ATT_HEADS].set(part(3)).at[:, LANE:LANE + SSD_HEADS].set(part(6))
    ws_hi = w_small.astype(BF16)
    ws_lo = (w_small - ws_hi.astype(F32)).astype(BF16)
    return w_main, ws_hi, ws_lo, cols


def kernel(x, meta_tokens, ln_in_g, ln_in_b, w_in, b_forget, ssd_conv_w, ssd_conv_b, ssd_dt_bias, ssd_a_log,
           ssd_d, ssd_norm_w, sc_conv_w, w_proj_attn, w_proj_ssd, w_proj_conv, w_out, ln_mix_g, ln_mix_b,
           dense_w_gu, dense_w_down, router_w, router_b, moe_w_gu, moe_w_down, ln_ffn_g, ln_ffn_b):
    batch, seq, d = x.shape
    depth = w_in.shape[0]
    alpha = (2 * depth) ** 0.25
    pad = (-(N_META + seq)) % MXU_DIM
    lp = pad + N_META + seq
    tp = batch * lp
    tk = MXU_DIM
    tq = _divisor_tile(lp, 3 * MXU_DIM, MXU_DIM)

    h = _embed_ln(x, meta_tokens, ln_in_g, ln_in_b, pad).reshape(tp, d)

    for layer in range(depth):
        w_main, ws_hi, ws_lo, cols = _prepare_in_proj(w_in[layer])
        proj, small = _in_proj(h, w_main, ws_hi, ws_lo)
        ka, q_t, v_t = _attn_prep(proj, small, cols, b_forget[layer], batch, lp, pad, tk)
        y_att_t = _attention(q_t, ka, v_t, batch, lp, pad, tq, tk)
        y_ssd = _ssd(proj, small, cols, ssd_conv_w[layer], ssd_conv_b[layer], ssd_dt_bias[layer],
                     ssd_a_log[layer], ssd_d[layer], ssd_norm_w[layer], batch, lp, pad)
        j = layer // 2
        router = None
        if layer % 2 == 1:
            rw = jnp.zeros((d, LANE), F32).at[:, :N_EXPERTS].set(router_w[j])
            rw_hi = rw.astype(BF16)
            rw_lo = (rw - rw_hi.astype(F32)).astype(BF16)
            router = (rw_hi, rw_lo, jnp.zeros((1, LANE), F32).at[0, :N_EXPERTS].set(router_b[j]))
        outs = _merge(y_att_t, y_ssd, proj, cols, h, w_proj_attn[layer].astype(BF16),
                      w_proj_ssd[layer].astype(BF16), w_proj_conv[layer].astype(BF16),
                      w_out[layer].astype(BF16), sc_conv_w[layer], ln_mix_g[layer], ln_mix_b[layer],
                      lp, pad, alpha, router)
        if layer % 2 == 0:
            h = _ffn(outs[0], dense_w_gu[j].astype(BF16), dense_w_down[j].astype(BF16),
                     ln_ffn_g[layer], ln_ffn_b[layer], lp, pad, alpha)
        else:
            h_mid, route_idx, route_w = outs
            h = _moe(h_mid, route_idx, route_w, moe_w_gu[j].astype(BF16), moe_w_down[j].astype(BF16),
                     ln_ffn_g[layer], ln_ffn_b[layer], batch, lp, pad, alpha)
    return h.reshape(batch, lp, d)[:, pad + N_META:]
```

```python
import functools

import jax
import jax.numpy as jnp
from jax import lax
from jax.experimental import pallas as pl
from jax.experimental.pallas import tpu as pltpu

F32, BF16, I32 = jnp.float32, jnp.bfloat16, jnp.int32

N_META = 16
ATT_HEADS = 8
ATT_HEAD_DIM = 64
SSD_HEADS = 16
SSD_HEAD_DIM = 64
SSD_GROUPS = 2
SSD_HPG = SSD_HEADS // SSD_GROUPS
SSD_STATE = 128
SSD_CONV = 4
SSD_CHUNK = 128
SC_CONV = 3
N_EXPERTS = 8
LN_EPS = 1e-5
RMS_EPS = 1e-5
NEG_INF = -1e30
LOG2E = 1.4426950408889634

LANE = 128
SUBLANE = 8
MXU_DIM = 256
VMEM_LIMIT = 56 * 1024 * 1024
TOKEN_TILE_ROWS = SUBLANE


def _cparams(*sem):
    return pltpu.CompilerParams(dimension_semantics=tuple(sem), vmem_limit_bytes=VMEM_LIMIT)


def _divisor_tile(n, target, quantum):
    best = None
    t = quantum
    while t <= min(n, target):
        if n % t == 0:
            best = t
        t += quantum
    assert best is not None, (n, target, quantum)
    return best


def _dot(a, b):
    return jnp.dot(a, b, preferred_element_type=F32)


def _dot_nt(a, b):
    return lax.dot_general(a, b, (((1,), (1,)), ((), ())), preferred_element_type=F32)


def _split2(x):
    hi = x.astype(BF16)
    lo = (x - hi.astype(F32)).astype(BF16)
    return hi, lo


def _split3(x):
    hi = x.astype(BF16)
    r = x - hi.astype(F32)
    mid = r.astype(BF16)
    lo = (r - mid.astype(F32)).astype(BF16)
    return hi, mid, lo


def _dot_x3(a_f32, w_hi, w_lo):
    a_hi, a_lo = _split2(a_f32)
    return _dot(a_hi, w_hi) + _dot(a_hi, w_lo) + _dot(a_lo, w_hi)


def _ln(v, g, b):
    mu = jnp.mean(v, axis=-1, keepdims=True)
    c = v - mu
    var = jnp.mean(c * c, axis=-1, keepdims=True)
    return c * lax.rsqrt(var + LN_EPS) * g + b


def _silu(v):
    return v * jax.nn.sigmoid(v)


def _softplus(v):
    return jnp.maximum(v, 0.0) + jnp.log(1.0 + jnp.exp(-jnp.abs(v)))


def _zero_pad_rows(v, tile_index, tm, tiles_per_seq, pad):
    row = (tile_index % tiles_per_seq) * tm + lax.broadcasted_iota(I32, (tm, 1), 0)
    return jnp.where(row >= pad, v, 0.0)


def _embed_ln_kernel(x_ref, meta_ref, g_ref, b_ref, o_ref, *, pad):
    j = pl.program_id(1)

    @pl.when(j == 0)
    def _():
        meta = _ln(meta_ref[...].astype(F32), g_ref[...], b_ref[...])
        o_ref[0] = jnp.concatenate([jnp.zeros((pad, meta.shape[1]), F32), meta], axis=0)

    @pl.when(j > 0)
    def _():
        o_ref[0] = _ln(x_ref[0].astype(F32), g_ref[...], b_ref[...])


def _embed_ln(x, meta, g, b, pad):
    batch, seq, d = x.shape
    head = pad + N_META
    assert pad % SUBLANE == 0 and seq % head == 0, (pad, seq)
    return pl.pallas_call(
        functools.partial(_embed_ln_kernel, pad=pad),
        grid=(batch, 1 + seq // head),
        in_specs=[pl.BlockSpec((1, head, d), lambda bi, j: (bi, jnp.maximum(j - 1, 0), 0)),
                  pl.BlockSpec((N_META, d), lambda bi, j: (0, 0)),
                  pl.BlockSpec((1, d), lambda bi, j: (0, 0)),
                  pl.BlockSpec((1, d), lambda bi, j: (0, 0))],
        out_specs=pl.BlockSpec((1, head, d), lambda bi, j: (bi, j, 0)),
        out_shape=jax.ShapeDtypeStruct((batch, head + seq, d), F32),
        compiler_params=_cparams("parallel", "arbitrary"),
        name="embed_ln",
    )(x, meta.astype(x.dtype), g.reshape(1, d), b.reshape(1, d))


def _in_proj_kernel(x_ref, w_ref, wsh_ref, wsl_ref, o_ref, os_ref, xb_ref):
    @pl.when(pl.program_id(1) == 0)
    def _():
        x = x_ref[...]
        xb_ref[...] = x.astype(BF16)
        os_ref[...] = _dot_x3(x, wsh_ref[...], wsl_ref[...])

    o_ref[...] = _dot(xb_ref[...], w_ref[...]).astype(BF16)


def _in_proj(h, w_main, ws_hi, ws_lo):
    tp, d = h.shape
    n = w_main.shape[1]
    ns = ws_hi.shape[1]
    tm = _divisor_tile(tp, 1280, LANE)
    tn = _divisor_tile(n, 9 * MXU_DIM, MXU_DIM)
    return pl.pallas_call(
        _in_proj_kernel,
        grid=(tp // tm, n // tn),
        in_specs=[pl.BlockSpec((tm, d), lambda i, j: (i, 0)),
                  pl.BlockSpec((d, tn), lambda i, j: (0, j)),
                  pl.BlockSpec((d, ns), lambda i, j: (0, 0)),
                  pl.BlockSpec((d, ns), lambda i, j: (0, 0))],
        out_specs=[pl.BlockSpec((tm, tn), lambda i, j: (i, j)),
                   pl.BlockSpec((tm, ns), lambda i, j: (i, 0))],
        out_shape=[jax.ShapeDtypeStruct((tp, n), BF16),
                   jax.ShapeDtypeStruct((tp, ns), F32)],
        scratch_shapes=[pltpu.VMEM((tm, d), BF16)],
        compiler_params=_cparams("parallel", "arbitrary"),
        name="in_proj",
    )(h, w_main, ws_hi, ws_lo)


ATT_BIAS_ROWS = 2 * SUBLANE
ATT_V_ROWS = ATT_HEAD_DIM + 2 * SUBLANE
ATT_BOUND_SLACK = (1.02, 1.0)


def _attn_prep_kernel(k_ref, q_ref, v_ref, f_ref, bf_ref, tri_ref, pk_ref, gk_ref, gq_ref,
                      ko_ref, qo_ref, vo_ref, carry_ref, *, tm, tiles_per_seq, pad):
    t = pl.program_id(0) % tiles_per_seq
    hd = ATT_HEAD_DIM
    hw = 2 * hd

    @pl.when(t == 0)
    def _():
        carry_ref[...] = jnp.zeros_like(carry_ref)

    x = f_ref[...] + bf_ref[...]
    log_f = -_softplus(-x) * LOG2E
    row = t * tm + lax.broadcasted_iota(I32, (tm, LANE), 0)
    lane = lax.broadcasted_iota(I32, (tm, LANE), 1)
    log_f = jnp.where((row >= pad) & (lane < ATT_HEADS), log_f, 0.0)
    tri = tri_ref[...]
    hi, mid, lo = _split3(log_f)
    c = _dot(tri, hi) + _dot(tri, mid) + _dot(tri, lo) + carry_ref[0:1, :]
    carry_ref[0:1, :] = c[tm - 1:tm, :]

    kf = k_ref[...].astype(F32)
    qf = q_ref[...].astype(F32)
    k_max2 = jnp.maximum(carry_ref[1:2, :], jnp.max(_dot((kf * kf).astype(BF16), gk_ref[...]), axis=0, keepdims=True))
    carry_ref[1:2, :] = k_max2
    q_norm2 = _dot((qf * qf).astype(BF16), gq_ref[...])
    m = jnp.sqrt(q_norm2 * k_max2) * ATT_BOUND_SLACK[0] + ATT_BOUND_SLACK[1]

    hi, mid, lo = _split3(c)
    packed = jnp.where(lane < 8, hi.astype(F32),
             jnp.where(lane < 16, pltpu.roll(mid.astype(F32), 8, 1),
             jnp.where(lane < 24, pltpu.roll(lo.astype(F32), 16, 1),
             jnp.where(lane == 24, 1.0, 0.0)))).astype(BF16)
    ko_ref[...] = (kf + _dot(packed, pk_ref[...])).astype(BF16)

    q_t = qf.T
    c_hi, c_mid, c_lo = (p.astype(F32) for p in _split3(c.T))
    m_hi, m_mid, m_lo = (p.astype(F32) for p in _split3(m.T))
    sub = lax.broadcasted_iota(I32, (SUBLANE, tm), 0)
    for h in range(ATT_HEADS):
        bias_a = jnp.where(sub == 0, c_hi[h:h + 1, :],
                 jnp.where(sub == 1, c_mid[h:h + 1, :],
                 jnp.where(sub == 2, c_lo[h:h + 1, :],
                 jnp.where(sub < 6, 1.0,
                 jnp.where(sub == 6, -m_hi[h:h + 1, :], -m_mid[h:h + 1, :])))))
        bias_b = jnp.where(sub == 0, -m_lo[h:h + 1, :], 0.0)
        blk = jnp.concatenate([q_t[h * hd:(h + 1) * hd, :], bias_a, bias_b,
                               jnp.zeros((hw - hd - ATT_BIAS_ROWS, tm), F32)], axis=0)
        qo_ref[h * hw:(h + 1) * hw, :] = blk.astype(BF16)

    v_t = v_ref[...].astype(F32).T
    ones_row = jnp.where(sub == 0, 1.0, 0.0)
    for h in range(ATT_HEADS):
        blk = jnp.concatenate([v_t[h * hd:(h + 1) * hd, :], ones_row, jnp.zeros((SUBLANE, tm), F32)], axis=0)
        vo_ref[h, 0] = blk.astype(BF16)


def _key_bias_placement():
    hw = 2 * ATT_HEAD_DIM
    rows, cols, vals = [], [], []
    for h in range(ATT_HEADS):
        base = h * hw + ATT_HEAD_DIM
        for piece in range(3):
            rows.append(24); cols.append(base + piece); vals.append(1.0)
            rows.append(8 * piece + h); cols.append(base + 3 + piece); vals.append(-1.0)
            rows.append(24); cols.append(base + 6 + piece); vals.append(1.0)
    pk = jnp.zeros((LANE, ATT_HEADS * hw), F32).at[jnp.array(rows), jnp.array(cols)].set(jnp.array(vals))
    return pk.astype(BF16)


def _head_grouping(width, per_head):
    return (jnp.arange(width)[:, None] // per_head == jnp.arange(LANE)[None, :]).astype(BF16)


def _attn_prep(proj, small, cols, b_forget, batch, lp, pad, tk):
    tp = proj.shape[0]
    hw = 2 * ATT_HEAD_DIM
    kw = ATT_HEADS * hw
    aw = ATT_HEADS * ATT_HEAD_DIM
    tm = tk
    tiles_per_seq = lp // tm
    tri = jnp.tril(jnp.ones((tm, tm), F32)).astype(BF16)
    bf = jnp.zeros((1, LANE), F32).at[0, :ATT_HEADS].set(b_forget)
    kern = functools.partial(_attn_prep_kernel, tm=tm, tiles_per_seq=tiles_per_seq, pad=pad)
    return pl.pallas_call(
        kern,
        grid=(tp // tm,),
        in_specs=[pl.BlockSpec((tm, kw), lambda i: (i, cols["k"] // kw)),
                  pl.BlockSpec((tm, aw), lambda i: (i, cols["q"] // aw)),
                  pl.BlockSpec((tm, aw), lambda i: (i, cols["v"] // aw)),
                  pl.BlockSpec((tm, LANE), lambda i: (i, 0)),
                  pl.BlockSpec((1, LANE), lambda i: (0, 0)),
                  pl.BlockSpec((tm, tm), lambda i: (0, 0)),
                  pl.BlockSpec((LANE, kw), lambda i: (0, 0)),
                  pl.BlockSpec((kw, LANE), lambda i: (0, 0)),
                  pl.BlockSpec((aw, LANE), lambda i: (0, 0))],
        out_specs=[pl.BlockSpec((tm, kw), lambda i: (i, 0)),
                   pl.BlockSpec((kw, tm), lambda i: (i // tiles_per_seq, i % tiles_per_seq)),
                   pl.BlockSpec((ATT_HEADS, 1, ATT_V_ROWS, tm),
                                lambda i: (i // tiles_per_seq, i % tiles_per_seq, 0, 0))],
        out_shape=[jax.ShapeDtypeStruct((tp, kw), BF16),
                   jax.ShapeDtypeStruct((batch * kw, lp), BF16),
                   jax.ShapeDtypeStruct((batch * ATT_HEADS, tiles_per_seq, ATT_V_ROWS, tm), BF16)],
        scratch_shapes=[pltpu.VMEM((SUBLANE, LANE), F32)],
        compiler_params=_cparams("arbitrary"),
        name="attn_prep",
    )(proj, proj, proj, small, bf, tri, _key_bias_placement(), _head_grouping(kw, hw),
      _head_grouping(aw, ATT_HEAD_DIM))


ATT_HEADS_PER_STEP = 8
ATT_MIN_ROW_SUM = 2.0 ** -80


def _attn_kernel(qt_ref, k_ref, vt_ref, o_ref, m_ref, acc_ref, *, tq, tk, pad, heads):
    i = pl.program_id(2)
    r = tq // tk
    hd = ATT_HEAD_DIM
    hw = 2 * hd
    vr = ATT_V_ROWS

    def scores(j, hh, masked):
        k = k_ref[pl.ds(pl.multiple_of(j * tk, tk), tk), hh * hw:(hh + 1) * hw]
        s = _dot(k, qt_ref[hh * hw:(hh + 1) * hw, :])
        if masked:
            kpos = j * tk + lax.broadcasted_iota(I32, (tk, tq), 0)
            qpos = i * tq + lax.broadcasted_iota(I32, (tk, tq), 1)
            s = jnp.where((kpos <= qpos) & (kpos >= pad), s, NEG_INF)
        return s

    def fast_step(j, masked):
        s_next = scores(j, 0, masked)
        for hh in range(heads):
            s = s_next
            if hh + 1 < heads:
                s_next = scores(j, hh + 1, masked)
            p = jnp.exp2(s).astype(BF16)
            acc_ref[hh * vr:(hh + 1) * vr, :] += _dot(vt_ref[hh, j], p)

    def exact_step(j, masked):
        for hh in range(heads):
            s = scores(j, hh, masked)
            m_old = m_ref[hh:hh + 1, :]
            m_new = jnp.maximum(m_old, jnp.max(s, axis=0, keepdims=True))
            p = jnp.exp2(s - m_new).astype(BF16)
            acc_ref[hh * vr:(hh + 1) * vr, :] = (jnp.exp2(m_old - m_new) * acc_ref[hh * vr:(hh + 1) * vr, :]
                                                 + _dot(vt_ref[hh, j], p))
            m_ref[hh:hh + 1, :] = m_new

    def sweep(step):
        def masked_body(j, c):
            step(j, True)
            return c

        def plain_body(j, c):
            step(j, False)
            return c

        acc_ref[...] = jnp.zeros_like(acc_ref)
        step(0, True)
        lax.fori_loop(1, r * i, plain_body, 0)
        lax.fori_loop(jnp.maximum(r * i, 1), r * i + r, masked_body, 0)

    sweep(fast_step)
    real = i * tq + lax.broadcasted_iota(I32, (1, tq), 1) >= pad
    smallest = jnp.min(jnp.concatenate(
        [jnp.where(real, acc_ref[hh * vr + hd:hh * vr + hd + 1, :], 1.0) for hh in range(heads)], axis=0))

    @pl.when(smallest < ATT_MIN_ROW_SUM)
    def _():
        m_ref[...] = jnp.full_like(m_ref, NEG_INF)
        sweep(exact_step)

    for hh in range(heads):
        l = acc_ref[hh * vr + hd:hh * vr + hd + 1, :]
        l = jnp.where(l > 0.0, l, 1.0)
        o_ref[hh * hd:(hh + 1) * hd, :] = (acc_ref[hh * vr:hh * vr + hd, :] / l).astype(BF16)


def _attention(q_t, ka, v_t, batch, lp, pad, tq, tk):
    hd = ATT_HEAD_DIM
    hw = 2 * hd
    nh = ATT_HEADS_PER_STEP
    groups = ATT_HEADS // nh
    nq = lp // tq
    nkv = lp // tk
    kern = functools.partial(_attn_kernel, tq=tq, tk=tk, pad=pad, heads=nh)
    whole_seq = dict(pipeline_mode=pl.Buffered(1))
    return pl.pallas_call(
        kern,
        grid=(batch, groups, nq),
        in_specs=[pl.BlockSpec((nh * hw, tq), lambda b, g, i: (b * groups + g, i)),
                  pl.BlockSpec((lp, nh * hw), lambda b, g, i: (b, g), **whole_seq),
                  pl.BlockSpec((nh, nkv, ATT_V_ROWS, tk), lambda b, g, i: (b * groups + g, 0, 0, 0), **whole_seq)],
        out_specs=pl.BlockSpec((nh * hd, tq), lambda b, g, i: (b * groups + g, i)),
        out_shape=jax.ShapeDtypeStruct((batch * ATT_HEADS * hd, lp), BF16),
        scratch_shapes=[pltpu.VMEM((nh, tq), F32), pltpu.VMEM((nh * ATT_V_ROWS, tq), F32)],
        compiler_params=_cparams("parallel", "parallel", "arbitrary"),
        name="fox_attention",
    )(q_t, ka, v_t)


SSD_MAX_CHUNKS_PER_STEP = 3


def _ssd_kernel(x_ref, b_ref, c_ref, z_ref, dt_ref, cw_ref, cb_ref, dtb_ref, alog_ref, dexp_ref, nw_ref,
                tri_ref, e_ref, o_ref, ubuf_ref, state_ref, ybuf_ref, *, pad, per_step):
    @pl.when(pl.program_id(1) == 0)
    def _():
        ubuf_ref[0:SUBLANE, :] = jnp.zeros((SUBLANE, ubuf_ref.shape[1]), F32)
        state_ref[...] = jnp.zeros_like(state_ref)

    for cc in range(per_step):
        rows = pl.ds(cc * SSD_CHUNK, SSD_CHUNK)
        _ssd_chunk(pl.program_id(1) * per_step + cc,
                   x_ref.at[rows], b_ref.at[rows], c_ref.at[rows], z_ref.at[rows], dt_ref.at[rows],
                   cw_ref, cb_ref, dtb_ref, alog_ref, dexp_ref, nw_ref, tri_ref, e_ref, o_ref.at[rows],
                   ubuf_ref, state_ref, ybuf_ref, pad)


def _ssd_chunk(chunk, x_ref, b_ref, c_ref, z_ref, dt_ref, cw_ref, cb_ref, dtb_ref, alog_ref, dexp_ref, nw_ref,
               tri_ref, e_ref, o_ref, ubuf_ref, state_ref, ybuf_ref, pad):
    blk = SSD_CHUNK
    inner = SSD_HEADS * SSD_HEAD_DIM
    gs = SSD_GROUPS * SSD_STATE
    gw = SSD_HPG * SSD_HEAD_DIM

    ubuf_ref[SUBLANE:SUBLANE + blk, 0:inner] = x_ref[...].astype(F32)
    ubuf_ref[SUBLANE:SUBLANE + blk, inner:inner + gs] = b_ref[...].astype(F32)
    ubuf_ref[SUBLANE:SUBLANE + blk, inner + gs:inner + 2 * gs] = c_ref[...].astype(F32)
    conv = cb_ref[...]
    for k in range(SSD_CONV):
        conv = conv + cw_ref[k:k + 1, :] * ubuf_ref[pl.ds(SUBLANE - (SSD_CONV - 1) + k, blk), :]
    ubuf_ref[0:SUBLANE, :] = ubuf_ref[blk:blk + SUBLANE, :]
    xc = _silu(conv)
    xs = xc[:, 0:inner]
    bm = xc[:, inner:inner + gs]
    cm = xc[:, inner + gs:inner + 2 * gs]

    row = lax.broadcasted_iota(I32, (blk, LANE), 0)
    lane = lax.broadcasted_iota(I32, (blk, LANE), 1)
    dt = _softplus(dt_ref[...] + dtb_ref[...])
    dt = jnp.where((lane < SSD_HEADS) & (chunk * blk + row >= pad), dt, 0.0)
    a = -jnp.exp(alog_ref[...])
    tri = tri_ref[...]
    hi, mid, lo = _split3(dt * a)
    a_cum = _dot(tri, hi) + _dot(tri, mid) + _dot(tri, lo)
    a_last = a_cum[blk - 1:blk, :]
    ea = jnp.exp(a_cum)
    wgt = jnp.exp(a_last - a_cum) * dt

    def pieces(v, at):
        v_hi = v.astype(BF16).astype(F32)
        v_lo = v - v_hi
        return (pltpu.roll(v_hi, at, 1) if at else v_hi), pltpu.roll(v_lo, at + SSD_HEADS, 1)

    dt_hi, dt_lo = pieces(dt, 0)
    ea_hi, ea_lo = pieces(ea, 2 * SSD_HEADS)
    wg_hi, wg_lo = pieces(wgt, 4 * SSD_HEADS)
    h16 = SSD_HEADS
    packed = jnp.where(lane < h16, dt_hi,
             jnp.where(lane < 2 * h16, dt_lo,
             jnp.where(lane < 3 * h16, ea_hi,
             jnp.where(lane < 4 * h16, ea_lo,
             jnp.where(lane < 5 * h16, wg_hi,
             jnp.where(lane < 6 * h16, wg_lo, 0.0)))))).astype(BF16)
    expanded = _dot(packed, e_ref[...])
    dt_exp = expanded[:, 0:inner]
    ea_exp = expanded[:, inner:2 * inner]
    wg_exp = expanded[:, 2 * inner:3 * inner]
    xdt = (xs * dt_exp).astype(BF16)
    xw = (xs * wg_exp).astype(BF16)

    a_cum_t = a_cum.T
    causal = row >= lane
    for g in range(SSD_GROUPS):
        cg = cm[:, g * SSD_STATE:(g + 1) * SSD_STATE].astype(BF16)
        bg_f32 = bm[:, g * SSD_STATE:(g + 1) * SSD_STATE]
        cbg = _dot_nt(cg, bg_f32.astype(BF16))
        mats = []
        for hh in range(SSD_HPG):
            h = g * SSD_HPG + hh
            diff = a_cum[:, h:h + 1] - a_cum_t[h:h + 1, :]
            decay = jnp.exp(jnp.where(causal, diff, NEG_INF))
            mats.append((cbg * decay).astype(BF16))
        for j in range(SSD_HPG // 2):
            col = g * gw + j * LANE
            xp = xdt[:, col:col + LANE]
            y0 = _dot(mats[2 * j], xp)
            y1 = _dot(mats[2 * j + 1], xp)
            ybuf_ref[:, col:col + LANE] = jnp.where(lane < SSD_HEAD_DIM, y0, y1)
        st = state_ref[g]
        y_off = _dot(cg, st.astype(BF16))
        ybuf_ref[:, g * gw:(g + 1) * gw] += y_off * ea_exp[:, g * gw:(g + 1) * gw]
        chunk_decay = ea_exp[blk - 1:blk, g * gw:(g + 1) * gw]
        state_ref[g] = st * chunk_decay + _dot(bg_f32.T.astype(BF16), xw[:, g * gw:(g + 1) * gw])

    y = ybuf_ref[...] + dexp_ref[...] * xs
    u = y * _silu(z_ref[...].astype(F32))
    for g in range(SSD_GROUPS):
        ug = u[:, g * gw:(g + 1) * gw]
        ug = ug * lax.rsqrt(jnp.mean(ug * ug, axis=-1, keepdims=True) + RMS_EPS)
        o_ref[:, g * gw:(g + 1) * gw] = (ug * nw_ref[:, g * gw:(g + 1) * gw]).astype(BF16)


def _ssd_expand_matrix():
    inner = SSD_HEADS * SSD_HEAD_DIM
    rows = jnp.arange(LANE)
    cols = jnp.arange(3 * inner)
    seg = cols // inner
    head = (cols % inner) // SSD_HEAD_DIM
    hit = (rows[:, None] // (2 * SSD_HEADS) == seg[None, :]) & (rows[:, None] % SSD_HEADS == head[None, :]) \
          & (rows[:, None] < 6 * SSD_HEADS)
    return hit.astype(BF16)


def _ssd(proj, small, cols, conv_w, conv_b, dt_bias, a_log, d, norm_w, batch, lp, pad):
    tp = proj.shape[0]
    blk = SSD_CHUNK
    inner = SSD_HEADS * SSD_HEAD_DIM
    gs = SSD_GROUPS * SSD_STATE
    nc = lp // blk
    cdim = inner + 2 * gs
    cw = jnp.zeros((SUBLANE, cdim), F32).at[:SSD_CONV].set(conv_w)
    pad_lane = lambda v: jnp.zeros((1, LANE), F32).at[0, :v.shape[0]].set(v)
    dexp = jnp.repeat(d, SSD_HEAD_DIM).reshape(1, inner)
    tri = jnp.tril(jnp.ones((blk, blk), F32)).astype(BF16)
    per_step = _divisor_tile(nc, SSD_MAX_CHUNKS_PER_STEP, 1)
    steps = nc // per_step
    rows = per_step * blk
    rowblk = lambda b, c: b * steps + c
    const = lambda shape: pl.BlockSpec(shape, lambda b, c: (0,) * len(shape))
    return pl.pallas_call(
        functools.partial(_ssd_kernel, pad=pad, per_step=per_step),
        grid=(batch, steps),
        in_specs=[pl.BlockSpec((rows, inner), lambda b, c: (rowblk(b, c), cols["x"] // inner)),
                  pl.BlockSpec((rows, gs), lambda b, c: (rowblk(b, c), cols["B"] // gs)),
                  pl.BlockSpec((rows, gs), lambda b, c: (rowblk(b, c), cols["C"] // gs)),
                  pl.BlockSpec((rows, inner), lambda b, c: (rowblk(b, c), cols["z"] // inner)),
                  pl.BlockSpec((rows, LANE), lambda b, c: (rowblk(b, c), 1)),
                  const((SUBLANE, cdim)), const((1, cdim)), const((1, LANE)), const((1, LANE)),
                  const((1, inner)), const((1, inner)), const((blk, blk)), const((LANE, 3 * inner))],
        out_specs=pl.BlockSpec((rows, inner), lambda b, c: (rowblk(b, c), 0)),
        out_shape=jax.ShapeDtypeStruct((tp, inner), BF16),
        scratch_shapes=[pltpu.VMEM((blk + SUBLANE, cdim), F32),
                        pltpu.VMEM((SSD_GROUPS, SSD_STATE, SSD_HPG * SSD_HEAD_DIM), F32),
                        pltpu.VMEM((blk, inner), F32)],
        compiler_params=_cparams("parallel", "arbitrary"),
        name="ssd",
    )(proj, proj, proj, proj, small, cw, conv_b.reshape(1, cdim), pad_lane(dt_bias), pad_lane(a_log),
      dexp, norm_w.reshape(1, inner), tri, _ssd_expand_matrix())


def _merge_kernel(yat_ref, ys_ref, sb_ref, sc_ref, sh_ref, sch_ref, shh_ref, ga_ref, gs_ref, gc_ref, h_ref,
                  wpa_ref, wps_ref, wpc_ref, wo_ref, scw_ref, g_ref, b_ref, *rest,
                  tm, tiles_per_seq, pad, alpha, with_router):
    if with_router:
        rwh_ref, rwl_ref, rb_ref, o_ref, ri_ref, rw_ref, vbuf_ref = rest
    else:
        o_ref, vbuf_ref = rest
    vbuf_ref[0:SUBLANE, :] = sch_ref[...].astype(F32) * shh_ref[...].astype(F32)
    vbuf_ref[SUBLANE:SUBLANE + tm, :] = sc_ref[...].astype(F32) * sh_ref[...].astype(F32)
    conv = jnp.zeros((tm, vbuf_ref.shape[1]), F32)
    for k in range(SC_CONV):
        conv = conv + scw_ref[k:k + 1, :] * vbuf_ref[pl.ds(SUBLANE - (SC_CONV - 1) + k, tm), :]
    y_conv = (sb_ref[...].astype(F32) * conv).astype(BF16)
    y_att = yat_ref[...].astype(F32).T.astype(BF16)

    merged = jax.nn.sigmoid(ga_ref[...].astype(F32)) * _dot(y_att, wpa_ref[...])
    merged += jax.nn.sigmoid(gs_ref[...].astype(F32)) * _dot(ys_ref[...], wps_ref[...])
    merged += jax.nn.sigmoid(gc_ref[...].astype(F32)) * _dot(y_conv, wpc_ref[...])
    mix = _dot(merged.astype(BF16), wo_ref[...])
    hn = _ln(alpha * h_ref[...] + mix, g_ref[...], b_ref[...])
    hn = _zero_pad_rows(hn, pl.program_id(0), tm, tiles_per_seq, pad)
    o_ref[...] = hn

    if with_router:
        lane = lax.broadcasted_iota(I32, (tm, LANE), 1)
        lane_f = lane.astype(F32)
        logits = _dot_x3(hn, rwh_ref[...], rwl_ref[...]) + rb_ref[...]
        lg = jnp.where(lane < N_EXPERTS, logits, -jnp.inf)
        v1 = jnp.max(lg, axis=1, keepdims=True)
        i1 = jnp.min(jnp.where(lg == v1, lane_f, float(LANE)), axis=1, keepdims=True).astype(I32)
        lg2 = jnp.where(lane == i1, -jnp.inf, lg)
        v2 = jnp.max(lg2, axis=1, keepdims=True)
        i2 = jnp.min(jnp.where(lg2 == v2, lane_f, float(LANE)), axis=1, keepdims=True).astype(I32)
        e = jnp.exp(v2 - v1)
        w1 = 1.0 / (1.0 + e)
        w2 = e / (1.0 + e)
        ri_ref[...] = jnp.where(lane == 0, i1, jnp.where(lane == 1, i2, 0))
        rw_ref[...] = jnp.where(lane == 0, w1, jnp.where(lane == 1, w2, 0.0))


def _merge(y_att_t, y_ssd, proj, cols, h, wpa, wps, wpc, wo, sc_w, g, b, lp, pad, alpha, router=None):
    tp, d = h.shape
    scw = wpc.shape[0]
    aw = wpa.shape[0]
    tm = _divisor_tile(lp, 768, LANE)
    tiles_per_seq = lp // tm
    nt = tp // tm
    halo = lambda i: jnp.maximum(i * (tm // SUBLANE) - 1, 0)
    const = lambda shape: pl.BlockSpec(shape, lambda i: (0,) * len(shape))
    in_specs = [pl.BlockSpec((aw, tm), lambda i: (i // tiles_per_seq, i % tiles_per_seq)),
                pl.BlockSpec((tm, y_ssd.shape[1]), lambda i: (i, 0)),
                pl.BlockSpec((tm, scw), lambda i: (i, cols["sc_b"] // scw)),
                pl.BlockSpec((tm, scw), lambda i: (i, cols["sc_c"] // scw)),
                pl.BlockSpec((tm, scw), lambda i: (i, cols["sc_h"] // scw)),
                pl.BlockSpec((SUBLANE, scw), lambda i: (halo(i), cols["sc_c"] // scw)),
                pl.BlockSpec((SUBLANE, scw), lambda i: (halo(i), cols["sc_h"] // scw)),
                pl.BlockSpec((tm, d), lambda i: (i, cols["g_att"] // d)),
                pl.BlockSpec((tm, d), lambda i: (i, cols["g_ssd"] // d)),
                pl.BlockSpec((tm, d), lambda i: (i, cols["g_conv"] // d)),
                pl.BlockSpec((tm, d), lambda i: (i, 0)),
                const(wpa.shape), const(wps.shape), const(wpc.shape), const(wo.shape),
                const((SUBLANE, scw)), const((1, d)), const((1, d))]
    args = [y_att_t, y_ssd, proj, proj, proj, proj, proj, proj, proj, proj, h, wpa, wps, wpc, wo,
            jnp.zeros((SUBLANE, scw), F32).at[:SC_CONV].set(sc_w), g.reshape(1, d), b.reshape(1, d)]
    out_specs = [pl.BlockSpec((tm, d), lambda i: (i, 0))]
    out_shape = [jax.ShapeDtypeStruct((tp, d), F32)]
    if router is not None:
        rw_hi, rw_lo, rb = router
        in_specs += [const(rw_hi.shape), const(rw_lo.shape), const((1, LANE))]
        args += [rw_hi, rw_lo, rb]
        out_specs += [pl.BlockSpec((tm, LANE), lambda i: (i, 0)), pl.BlockSpec((tm, LANE), lambda i: (i, 0))]
        out_shape += [jax.ShapeDtypeStruct((tp, LANE), I32), jax.ShapeDtypeStruct((tp, LANE), F32)]
    kern = functools.partial(_merge_kernel, tm=tm, tiles_per_seq=tiles_per_seq, pad=pad, alpha=alpha,
                             with_router=router is not None)
    return pl.pallas_call(
        kern, grid=(nt,), in_specs=in_specs, out_specs=out_specs, out_shape=out_shape,
        scratch_shapes=[pltpu.VMEM((tm + SUBLANE, scw), F32)],
        compiler_params=_cparams("parallel"),
        name="merge_out",
    )(*args)


def _ffn_kernel(h_ref, wg_ref, wu_ref, wd_ref, g_ref, b_ref, o_ref, *, tm, tf, tiles_per_seq, pad, alpha):
    h = h_ref[...]
    xb = h.astype(BF16)
    acc = jnp.zeros_like(h)
    for c in range(wd_ref.shape[0] // tf):
        cols = slice(c * tf, (c + 1) * tf)
        act = (_silu(_dot(xb, wg_ref[:, cols])) * _dot(xb, wu_ref[:, cols])).astype(BF16)
        acc = acc + _dot(act, wd_ref[cols, :])
    hn = _ln(alpha * h + acc, g_ref[...], b_ref[...])
    o_ref[...] = _zero_pad_rows(hn, pl.program_id(0), tm, tiles_per_seq, pad)


def _ffn(h, w_gu, w_down, g, b, lp, pad, alpha):
    tp, d = h.shape
    ff = w_down.shape[0]
    tm = _divisor_tile(lp, 768, LANE)
    tf = _divisor_tile(ff, 2 * MXU_DIM, MXU_DIM)
    kern = functools.partial(_ffn_kernel, tm=tm, tf=tf, tiles_per_seq=lp // tm, pad=pad, alpha=alpha)
    resident = dict(pipeline_mode=pl.Buffered(1))
    return pl.pallas_call(
        kern,
        grid=(tp // tm,),
        in_specs=[pl.BlockSpec((tm, d), lambda i: (i, 0)),
                  pl.BlockSpec((d, ff), lambda i: (0, 0), **resident),
                  pl.BlockSpec((d, ff), lambda i: (0, 1), **resident),
                  pl.BlockSpec((ff, d), lambda i: (0, 0), **resident),
                  pl.BlockSpec((1, d), lambda i: (0, 0)),
                  pl.BlockSpec((1, d), lambda i: (0, 0))],
        out_specs=pl.BlockSpec((tm, d), lambda i: (i, 0)),
        out_shape=jax.ShapeDtypeStruct((tp, d), F32),
        compiler_params=_cparams("parallel"),
        name="dense_ffn",
    )(h, w_gu, w_gu, w_down, g.reshape(1, d), b.reshape(1, d))


def _moe_scatter_kernel(slot_ref, h_ref, init_ref, o_ref, buf_ref, sem, *, tm):
    del init_ref
    for s in range(TOKEN_TILE_ROWS):
        buf_ref[pl.ds(s, tm, stride=TOKEN_TILE_ROWS), :] = h_ref[:, s * LANE:(s + 1) * LANE]

    def issue(r, c):
        src = buf_ref.at[pl.ds(pl.multiple_of(r * TOKEN_TILE_ROWS, TOKEN_TILE_ROWS), TOKEN_TILE_ROWS)]
        for k in range(2):
            slot = slot_ref[0, 0, 2 * r + k]
            dst = o_ref.at[pl.ds(pl.multiple_of(slot * TOKEN_TILE_ROWS, TOKEN_TILE_ROWS), TOKEN_TILE_ROWS)]
            pltpu.make_async_copy(src, dst, sem).start(priority=k)
        return c

    lax.fori_loop(0, tm, issue, 0)
    for _ in range(2):
        pltpu.make_async_copy(buf_ref, o_ref.at[pl.ds(0, tm * TOKEN_TILE_ROWS)], sem).wait()


def _moe_scatter(h, slots, n_rows, lp):
    tp, d = h.shape
    assert d == TOKEN_TILE_ROWS * LANE
    tm = _divisor_tile(lp, 768, LANE)
    nt = tp // tm
    init = jnp.zeros((n_rows * TOKEN_TILE_ROWS, LANE), F32)
    return pl.pallas_call(
        functools.partial(_moe_scatter_kernel, tm=tm),
        grid=(nt,),
        in_specs=[pl.BlockSpec((1, 1, 2 * tm), lambda i: (i, 0, 0), memory_space=pltpu.SMEM),
                  pl.BlockSpec((tm, d), lambda i: (i, 0)),
                  pl.BlockSpec(memory_space=pl.ANY)],
        out_specs=pl.BlockSpec(memory_space=pl.ANY),
        out_shape=jax.ShapeDtypeStruct(init.shape, F32),
        scratch_shapes=[pltpu.VMEM((tm * TOKEN_TILE_ROWS, LANE), F32), pltpu.SemaphoreType.DMA(())],
        input_output_aliases={2: 0},
        compiler_params=_cparams("arbitrary"),
        name="moe_scatter",
    )(slots.reshape(nt, 1, 2 * tm), h, init)


def _moe_kernel(te_ref, tv_ref, x_ref, wg_ref, wu_ref, wd_ref, o_ref, xb_ref, acc_ref, *, tm):
    n = pl.program_id(0)
    f = pl.program_id(1)

    @pl.when(f == 0)
    def _():
        for s in range(TOKEN_TILE_ROWS):
            xb_ref[:, s * LANE:(s + 1) * LANE] = x_ref[pl.ds(s, tm, stride=TOKEN_TILE_ROWS), :].astype(BF16)
        acc_ref[...] = jnp.zeros_like(acc_ref)

    @pl.when(tv_ref[n] == 1)
    def _():
        xb = xb_ref[...]
        act = (_silu(_dot(xb, wg_ref[0])) * _dot(xb, wu_ref[0])).astype(BF16)
        acc_ref[...] += _dot(act, wd_ref[0])

    @pl.when(f == pl.num_programs(1) - 1)
    def _():
        for s in range(TOKEN_TILE_ROWS):
            o_ref[pl.ds(s, tm, stride=TOKEN_TILE_ROWS), :] = acc_ref[:, s * LANE:(s + 1) * LANE]


def _moe_experts(x_sorted, tile_expert, tile_valid, w_gu, w_down, tm, n_tiles):
    d = w_down.shape[2]
    ff = w_down.shape[1]
    tf = _divisor_tile(ff, 7 * MXU_DIM, MXU_DIM)
    nf = ff // tf
    fblk = lambda f, n, tv: f * tv[n] + (nf - 1) * (1 - tv[n])
    rows = tm * TOKEN_TILE_ROWS
    grid_spec = pltpu.PrefetchScalarGridSpec(
        num_scalar_prefetch=2,
        grid=(n_tiles, nf),
        in_specs=[pl.BlockSpec((rows, LANE), lambda n, f, te, tv: (n, 0)),
                  pl.BlockSpec((1, d, tf), lambda n, f, te, tv: (te[n], 0, fblk(f, n, tv))),
                  pl.BlockSpec((1, d, tf), lambda n, f, te, tv: (te[n], 0, nf + fblk(f, n, tv))),
                  pl.BlockSpec((1, tf, d), lambda n, f, te, tv: (te[n], fblk(f, n, tv), 0))],
        out_specs=pl.BlockSpec((rows, LANE), lambda n, f, te, tv: (n, 0)),
        scratch_shapes=[pltpu.VMEM((tm, d), BF16), pltpu.VMEM((tm, d), F32)])
    return pl.pallas_call(
        functools.partial(_moe_kernel, tm=tm),
        grid_spec=grid_spec,
        out_shape=jax.ShapeDtypeStruct((n_tiles * rows, LANE), F32),
        compiler_params=_cparams("parallel", "arbitrary"),
        name="moe_experts",
    )(tile_expert, tile_valid, x_sorted, w_gu, w_gu, w_down)


def _moe_combine_kernel(slot_ref, y_ref, rw_ref, h_ref, g_ref, b_ref, o_ref, gbuf_ref, sem,
                        *, tm, tiles_per_seq, pad, alpha):
    def issue(r, c):
        for k in range(2):
            slot = slot_ref[0, 0, 2 * r + k]
            src = y_ref.at[pl.ds(pl.multiple_of(slot * TOKEN_TILE_ROWS, TOKEN_TILE_ROWS), TOKEN_TILE_ROWS)]
            dst = gbuf_ref.at[pl.ds(pl.multiple_of((k * tm + r) * TOKEN_TILE_ROWS, TOKEN_TILE_ROWS),
                                    TOKEN_TILE_ROWS)]
            pltpu.make_async_copy(src, dst, sem).start(priority=k)
        return c

    lax.fori_loop(0, tm, issue, 0, unroll=4)
    pltpu.make_async_copy(y_ref.at[pl.ds(0, 2 * tm * TOKEN_TILE_ROWS)], gbuf_ref, sem).wait()

    rw = rw_ref[...]
    w0, w1 = rw[:, 0:1], rw[:, 1:2]
    ff = jnp.concatenate(
        [w0 * gbuf_ref[pl.ds(s, tm, stride=TOKEN_TILE_ROWS), :]
         + w1 * gbuf_ref[pl.ds(tm * TOKEN_TILE_ROWS + s, tm, stride=TOKEN_TILE_ROWS), :]
         for s in range(TOKEN_TILE_ROWS)], axis=1)
    hn = _ln(alpha * h_ref[...] + ff, g_ref[...], b_ref[...])
    o_ref[...] = _zero_pad_rows(hn, pl.program_id(0), tm, tiles_per_seq, pad)


def _moe_combine(y_sorted, slots, route_w, h, g, b, lp, pad, alpha):
    tp, d = h.shape
    tm = _divisor_tile(lp, 768, LANE)
    nt = tp // tm
    kern = functools.partial(_moe_combine_kernel, tm=tm, tiles_per_seq=lp // tm, pad=pad, alpha=alpha)
    return pl.pallas_call(
        kern,
        grid=(nt,),
        in_specs=[pl.BlockSpec((1, 1, 2 * tm), lambda i: (i, 0, 0), memory_space=pltpu.SMEM),
                  pl.BlockSpec(memory_space=pl.ANY),
                  pl.BlockSpec((tm, LANE), lambda i: (i, 0)),
                  pl.BlockSpec((tm, d), lambda i: (i, 0)),
                  pl.BlockSpec((1, d), lambda i: (0, 0)),
                  pl.BlockSpec((1, d), lambda i: (0, 0))],
        out_specs=pl.BlockSpec((tm, d), lambda i: (i, 0)),
        out_shape=jax.ShapeDtypeStruct((tp, d), F32),
        scratch_shapes=[pltpu.VMEM((2 * tm * TOKEN_TILE_ROWS, LANE), F32), pltpu.SemaphoreType.DMA(())],
        compiler_params=_cparams("arbitrary"),
        name="moe_combine",
    )(slots.reshape(nt, 1, 2 * tm), y_sorted, route_w, h, g.reshape(1, d), b.reshape(1, d))


def _moe(h, route_idx, route_w, w_gu, w_down, g, b, batch, lp, pad, alpha):
    tp, d = h.shape
    n_real = batch * (lp - pad)
    tm = 640
    experts = route_idx[:, :2].reshape(tp * 2)
    seq_row = jnp.arange(tp, dtype=I32) % lp
    real = jnp.repeat(seq_row >= pad, 2)
    onehot = ((experts[:, None] == jnp.arange(N_EXPERTS, dtype=I32)[None, :]) & real[:, None]).astype(I32)
    csum = jnp.cumsum(onehot, axis=0)
    counts = csum[-1]
    rank = jnp.sum(onehot * csum, axis=1) - 1
    tiles = (counts + tm - 1) // tm
    tile_end = jnp.cumsum(tiles)
    slot = ((tile_end - tiles) * tm)[experts] + rank
    n_tiles = (2 * n_real) // tm + N_EXPERTS
    tile_ids = jnp.arange(n_tiles, dtype=I32)
    tile_expert = jnp.minimum(jnp.sum((tile_ids[:, None] >= tile_end[None, :]).astype(I32), axis=1),
                              N_EXPERTS - 1).astype(I32)
    tile_valid = (tile_ids < tile_end[-1]).astype(I32)
    n_pad = batch * pad * 2
    spare_tiles = -(-n_pad // tm)
    pad_id = jnp.repeat((jnp.arange(tp, dtype=I32) // lp) * pad + seq_row, 2) * 2 + jnp.tile(jnp.arange(2, dtype=I32), tp)
    slot_scatter = jnp.where(real, slot, n_tiles * tm + pad_id).astype(I32)
    slot_gather = jnp.where(real, slot, 0).astype(I32)

    x_sorted = _moe_scatter(h, slot_scatter, (n_tiles + spare_tiles) * tm, lp)
    y_sorted = _moe_experts(x_sorted, tile_expert, tile_valid, w_gu, w_down, tm, n_tiles)
    return _moe_combine(y_sorted, slot_gather, route_w, h, g, b, lp, pad, alpha)


def _prepare_in_proj(w_in):
    d = w_in.shape[0]
    aw = ATT_HEADS * ATT_HEAD_DIM
    inner = SSD_HEADS * SSD_HEAD_DIM
    gs = SSD_GROUPS * SSD_STATE
    sizes = [aw, aw, aw, ATT_HEADS, inner, inner + 2 * gs, SSD_HEADS]
    sc3 = w_in.shape[1] - sum(sizes) - 3 * d
    sizes += [sc3, 3 * d]
    offs = [0]
    for s in sizes:
        offs.append(offs[-1] + s)
    part = lambda k: w_in[:, offs[k]:offs[k + 1]]

    def pad_heads(w):
        w = w.reshape(d, ATT_HEADS, ATT_HEAD_DIM)
        return jnp.pad(w, ((0, 0), (0, 0), (0, ATT_HEAD_DIM))).reshape(d, 2 * aw)

    pieces = [("k", pad_heads(part(1))), ("q", part(0) * (ATT_HEAD_DIM ** -0.5 * LOG2E)), ("v", part(2)),
              ("z", part(4)), ("x", part(5)[:, :inner]), ("B", part(5)[:, inner:inner + gs]),
              ("C", part(5)[:, inner + gs:]), ("sc_b", part(7)[:, :sc3 // 3]),
              ("sc_c", part(7)[:, sc3 // 3:2 * sc3 // 3]), ("sc_h", part(7)[:, 2 * sc3 // 3:]),
              ("g_att", part(8)[:, :d]), ("g_ssd", part(8)[:, d:2 * d]), ("g_conv", part(8)[:, 2 * d:])]
    cols, off = {}, 0
    for name, w in pieces:
        cols[name] = off
        off += w.shape[1]
    w_main = jnp.concatenate([w for _, w in pieces], axis=1).astype(BF16)
    w_small = jnp.zeros((d, 2 * LANE), F32).at[:, :ATT_HEADS].set(part(3)).at[:, LANE:LANE + SSD_HEADS].set(part(6))
    ws_hi = w_small.astype(BF16)
    ws_lo = (w_small - ws_hi.astype(F32)).astype(BF16)
    return w_main, ws_hi, ws_lo, cols


def kernel(x, meta_tokens, ln_in_g, ln_in_b, w_in, b_forget, ssd_conv_w, ssd_conv_b, ssd_dt_bias, ssd_a_log,
           ssd_d, ssd_norm_w, sc_conv_w, w_proj_attn, w_proj_ssd, w_proj_conv, w_out, ln_mix_g, ln_mix_b,
           dense_w_gu, dense_w_down, router_w, router_b, moe_w_gu, moe_w_down, ln_ffn_g, ln_ffn_b):
    batch, seq, d = x.shape
    depth = w_in.shape[0]
    alpha = (2 * depth) ** 0.25
    pad = (-(N_META + seq)) % MXU_DIM
    lp = pad + N_META + seq
    tp = batch * lp
    tk = MXU_DIM
    tq = _divisor_tile(lp, 3 * MXU_DIM, MXU_DIM)

    h = _embed_ln(x, meta_tokens, ln_in_g, ln_in_b, pad).reshape(tp, d)

    for layer in range(depth):
        w_main, ws_hi, ws_lo, cols = _prepare_in_proj(w_in[layer])
        proj, small = _in_proj(h, w_main, ws_hi, ws_lo)
        ka, q_t, v_t = _attn_prep(proj, small, cols, b_forget[layer], batch, lp, pad, tk)
        y_att_t = _attention(q_t, ka, v_t, batch, lp, pad, tq, tk)
        y_ssd = _ssd(proj, small, cols, ssd_conv_w[layer], ssd_conv_b[layer], ssd_dt_bias[layer],
                     ssd_a_log[layer], ssd_d[layer], ssd_norm_w[layer], batch, lp, pad)
        j = layer // 2
        router = None
        if layer % 2 == 1:
            rw = jnp.zeros((d, LANE), F32).at[:, :N_EXPERTS].set(router_w[j])
            rw_hi = rw.astype(BF16)
            rw_lo = (rw - rw_hi.astype(F32)).astype(BF16)
            router = (rw_hi, rw_lo, jnp.zeros((1, LANE), F32).at[0, :N_EXPERTS].set(router_b[j]))
        outs = _merge(y_att_t, y_ssd, proj, cols, h, w_proj_attn[layer].astype(BF16),
                      w_proj_ssd[layer].astype(BF16), w_proj_conv[layer].astype(BF16),
                      w_out[layer].astype(BF16), sc_conv_w[layer], ln_mix_g[layer], ln_mix_b[layer],
                      lp, pad, alpha, router)
        if layer % 2 == 0:
            h = _ffn(outs[0], dense_w_gu[j].astype(BF16), dense_w_down[j].astype(BF16),
                     ln_ffn_g[layer], ln_ffn_b[layer], lp, pad, alpha)
        else:
            h_mid, route_idx, route_w = outs
            h = _moe(h_mid, route_idx, route_w, moe_w_gu[j].astype(BF16), moe_w_down[j].astype(BF16),
                     ln_ffn_g[layer], ln_ffn_b[layer], batch, lp, pad, alpha)
    return h.reshape(batch, lp, d)[:, pad + N_META:]
```

```python
import functools

import jax
import jax.numpy as jnp
from jax import lax
from jax.experimental import pallas as pl
from jax.experimental.pallas import tpu as pltpu

F32, BF16, I32 = jnp.float32, jnp.bfloat16, jnp.int32

N_META = 16
ATT_HEADS = 8
ATT_HEAD_DIM = 64
SSD_HEADS = 16
SSD_HEAD_DIM = 64
SSD_GROUPS = 2
SSD_HPG = SSD_HEADS // SSD_GROUPS
SSD_STATE = 128
SSD_CONV = 4
SSD_CHUNK = 128
SC_CONV = 3
N_EXPERTS = 8
LN_EPS = 1e-5
RMS_EPS = 1e-5
NEG_INF = -1e30
LOG2E = 1.4426950408889634

LANE = 128
SUBLANE = 8
MXU_DIM = 256
VMEM_LIMIT = 56 * 1024 * 1024
TOKEN_TILE_ROWS = SUBLANE


def _cparams(*sem):
    return pltpu.CompilerParams(dimension_semantics=tuple(sem), vmem_limit_bytes=VMEM_LIMIT)


def _divisor_tile(n, target, quantum):
    best = None
    t = quantum
    while t <= min(n, target):
        if n % t == 0:
            best = t
        t += quantum
    assert best is not None, (n, target, quantum)
    return best


def _dot(a, b):
    return jnp.dot(a, b, preferred_element_type=F32)


def _dot_nt(a, b):
    return lax.dot_general(a, b, (((1,), (1,)), ((), ())), preferred_element_type=F32)


def _split2(x):
    hi = x.astype(BF16)
    lo = (x - hi.astype(F32)).astype(BF16)
    return hi, lo


def _split3(x):
    hi = x.astype(BF16)
    r = x - hi.astype(F32)
    mid = r.astype(BF16)
    lo = (r - mid.astype(F32)).astype(BF16)
    return hi, mid, lo


def _dot_x3(a_f32, w_hi, w_lo):
    a_hi, a_lo = _split2(a_f32)
    return _dot(a_hi, w_hi) + _dot(a_hi, w_lo) + _dot(a_lo, w_hi)


def _ln(v, g, b):
    mu = jnp.mean(v, axis=-1, keepdims=True)
    c = v - mu
    var = jnp.mean(c * c, axis=-1, keepdims=True)
    return c * lax.rsqrt(var + LN_EPS) * g + b


def _silu(v):
    return v * jax.nn.sigmoid(v)


def _softplus(v):
    return jnp.maximum(v, 0.0) + jnp.log(1.0 + jnp.exp(-jnp.abs(v)))


def _zero_pad_rows(v, tile_index, tm, tiles_per_seq, pad):
    row = (tile_index % tiles_per_seq) * tm + lax.broadcasted_iota(I32, (tm, 1), 0)
    return jnp.where(row >= pad, v, 0.0)


def _embed_ln_kernel(x_ref, meta_ref, g_ref, b_ref, o_ref, *, pad):
    j = pl.program_id(1)

    @pl.when(j == 0)
    def _():
        meta = _ln(meta_ref[...].astype(F32), g_ref[...], b_ref[...])
        o_ref[0] = jnp.concatenate([jnp.zeros((pad, meta.shape[1]), F32), meta], axis=0)

    @pl.when(j > 0)
    def _():
        o_ref[0] = _ln(x_ref[0].astype(F32), g_ref[...], b_ref[...])


def _embed_ln(x, meta, g, b, pad):
    batch, seq, d = x.shape
    head = pad + N_META
    assert pad % SUBLANE == 0 and seq % head == 0, (pad, seq)
    return pl.pallas_call(
        functools.partial(_embed_ln_kernel, pad=pad),
        grid=(batch, 1 + seq // head),
        in_specs=[pl.BlockSpec((1, head, d), lambda bi, j: (bi, jnp.maximum(j - 1, 0), 0)),
                  pl.BlockSpec((N_META, d), lambda bi, j: (0, 0)),
                  pl.BlockSpec((1, d), lambda bi, j: (0, 0)),
                  pl.BlockSpec((1, d), lambda bi, j: (0, 0))],
        out_specs=pl.BlockSpec((1, head, d), lambda bi, j: (bi, j, 0)),
        out_shape=jax.ShapeDtypeStruct((batch, head + seq, d), F32),
        compiler_params=_cparams("parallel", "arbitrary"),
        name="embed_ln",
    )(x, meta.astype(x.dtype), g.reshape(1, d), b.reshape(1, d))


def _in_proj_kernel(x_ref, w_ref, wsh_ref, wsl_ref, o_ref, os_ref, xb_ref):
    @pl.when(pl.program_id(1) == 0)
    def _():
        x = x_ref[...]
        xb_ref[...] = x.astype(BF16)
        os_ref[...] = _dot_x3(x, wsh_ref[...], wsl_ref[...])

    o_ref[...] = _dot(xb_ref[...], w_ref[...]).astype(BF16)


def _in_proj(h, w_main, ws_hi, ws_lo):
    tp, d = h.shape
    n = w_main.shape[1]
    ns = ws_hi.shape[1]
    tm = _divisor_tile(tp, 1280, LANE)
    tn = _divisor_tile(n, 9 * MXU_DIM, MXU_DIM)
    return pl.pallas_call(
        _in_proj_kernel,
        grid=(tp // tm, n // tn),
        in_specs=[pl.BlockSpec((tm, d), lambda i, j: (i, 0)),
                  pl.BlockSpec((d, tn), lambda i, j: (0, j)),
                  pl.BlockSpec((d, ns), lambda i, j: (0, 0)),
                  pl.BlockSpec((d, ns), lambda i, j: (0, 0))],
        out_specs=[pl.BlockSpec((tm, tn), lambda i, j: (i, j)),
                   pl.BlockSpec((tm, ns), lambda i, j: (i, 0))],
        out_shape=[jax.ShapeDtypeStruct((tp, n), BF16),
                   jax.ShapeDtypeStruct((tp, ns), F32)],
        scratch_shapes=[pltpu.VMEM((tm, d), BF16)],
        compiler_params=_cparams("parallel", "arbitrary"),
        name="in_proj",
    )(h, w_main, ws_hi, ws_lo)


ATT_BIAS_ROWS = 2 * SUBLANE
ATT_V_ROWS = ATT_HEAD_DIM + 2 * SUBLANE
ATT_BOUND_SLACK = (1.02, 1.0)


def _attn_prep_kernel(k_ref, q_ref, v_ref, f_ref, bf_ref, tri_ref, pk_ref, gk_ref, gq_ref,
                      ko_ref, qo_ref, vo_ref, carry_ref, *, tm, tiles_per_seq, pad):
    t = pl.program_id(0) % tiles_per_seq
    hd = ATT_HEAD_DIM
    hw = 2 * hd

    @pl.when(t == 0)
    def _():
        carry_ref[...] = jnp.zeros_like(carry_ref)

    x = f_ref[...] + bf_ref[...]
    log_f = -_softplus(-x) * LOG2E
    row = t * tm + lax.broadcasted_iota(I32, (tm, LANE), 0)
    lane = lax.broadcasted_iota(I32, (tm, LANE), 1)
    log_f = jnp.where((row >= pad) & (lane < ATT_HEADS), log_f, 0.0)
    tri = tri_ref[...]
    hi, mid, lo = _split3(log_f)
    c = _dot(tri, hi) + _dot(tri, mid) + _dot(tri, lo) + carry_ref[0:1, :]
    carry_ref[0:1, :] = c[tm - 1:tm, :]

    kf = k_ref[...].astype(F32)
    qf = q_ref[...].astype(F32)
    k_max2 = jnp.maximum(carry_ref[1:2, :], jnp.max(_dot((kf * kf).astype(BF16), gk_ref[...]), axis=0, keepdims=True))
    carry_ref[1:2, :] = k_max2
    q_norm2 = _dot((qf * qf).astype(BF16), gq_ref[...])
    m = jnp.sqrt(q_norm2 * k_max2) * ATT_BOUND_SLACK[0] + ATT_BOUND_SLACK[1]

    hi, mid, lo = _split3(c)
    packed = jnp.where(lane < 8, hi.astype(F32),
             jnp.where(lane < 16, pltpu.roll(mid.astype(F32), 8, 1),
             jnp.where(lane < 24, pltpu.roll(lo.astype(F32), 16, 1),
             jnp.where(lane == 24, 1.0, 0.0)))).astype(BF16)
    ko_ref[...] = (kf + _dot(packed, pk_ref[...])).astype(BF16)

    q_t = qf.T
    c_hi, c_mid, c_lo = (p.astype(F32) for p in _split3(c.T))
    m_hi, m_mid, m_lo = (p.astype(F32) for p in _split3(m.T))
    sub = lax.broadcasted_iota(I32, (SUBLANE, tm), 0)
    for h in range(ATT_HEADS):
        bias_a = jnp.where(sub == 0, c_hi[h:h + 1, :],
                 jnp.where(sub == 1, c_mid[h:h + 1, :],
                 jnp.where(sub == 2, c_lo[h:h + 1, :],
                 jnp.where(sub < 6, 1.0,
                 jnp.where(sub == 6, -m_hi[h:h + 1, :], -m_mid[h:h + 1, :])))))
        bias_b = jnp.where(sub == 0, -m_lo[h:h + 1, :], 0.0)
        blk = jnp.concatenate([q_t[h * hd:(h + 1) * hd, :], bias_a, bias_b,
                               jnp.zeros((hw - hd - ATT_BIAS_ROWS, tm), F32)], axis=0)
        qo_ref[h * hw:(h + 1) * hw, :] = blk.astype(BF16)

    v_t = v_ref[...].astype(F32).T
    ones_row = jnp.where(sub == 0, 1.0, 0.0)
    for h in range(ATT_HEADS):
        blk = jnp.concatenate([v_t[h * hd:(h + 1) * hd, :], ones_row, jnp.zeros((SUBLANE, tm), F32)], axis=0)
        vo_ref[h, 0] = blk.astype(BF16)


def _key_bias_placement():
    hw = 2 * ATT_HEAD_DIM
    rows, cols, vals = [], [], []
    for h in range(ATT_HEADS):
        base = h * hw + ATT_HEAD_DIM
        for piece in range(3):
            rows.append(24); cols.append(base + piece); vals.append(1.0)
            rows.append(8 * piece + h); cols.append(base + 3 + piece); vals.append(-1.0)
            rows.append(24); cols.append(base + 6 + piece); vals.append(1.0)
    pk = jnp.zeros((LANE, ATT_HEADS * hw), F32).at[jnp.array(rows), jnp.array(cols)].set(jnp.array(vals))
    return pk.astype(BF16)


def _head_grouping(width, per_head):
    return (jnp.arange(width)[:, None] // per_head == jnp.arange(LANE)[None, :]).astype(BF16)


def _attn_prep(proj, small, cols, b_forget, batch, lp, pad, tk):
    tp = proj.shape[0]
    hw = 2 * ATT_HEAD_DIM
    kw = ATT_HEADS * hw
    aw = ATT_HEADS * ATT_HEAD_DIM
    tm = tk
    tiles_per_seq = lp // tm
    tri = jnp.tril(jnp.ones((tm, tm), F32)).astype(BF16)
    bf = jnp.zeros((1, LANE), F32).at[0, :ATT_HEADS].set(b_forget)
    kern = functools.partial(_attn_prep_kernel, tm=tm, tiles_per_seq=tiles_per_seq, pad=pad)
    return pl.pallas_call(
        kern,
        grid=(tp // tm,),
        in_specs=[pl.BlockSpec((tm, kw), lambda i: (i, cols["k"] // kw)),
                  pl.BlockSpec((tm, aw), lambda i: (i, cols["q"] // aw)),
                  pl.BlockSpec((tm, aw), lambda i: (i, cols["v"] // aw)),
                  pl.BlockSpec((tm, LANE), lambda i: (i, 0)),
                  pl.BlockSpec((1, LANE), lambda i: (0, 0)),
                  pl.BlockSpec((tm, tm), lambda i: (0, 0)),
                  pl.BlockSpec((LANE, kw), lambda i: (0, 0)),
                  pl.BlockSpec((kw, LANE), lambda i: (0, 0)),
                  pl.BlockSpec((aw, LANE), lambda i: (0, 0))],
        out_specs=[pl.BlockSpec((tm, kw), lambda i: (i, 0)),
                   pl.BlockSpec((kw, tm), lambda i: (i // tiles_per_seq, i % tiles_per_seq)),
                   pl.BlockSpec((ATT_HEADS, 1, ATT_V_ROWS, tm),
                                lambda i: (i // tiles_per_seq, i % tiles_per_seq, 0, 0))],
        out_shape=[jax.ShapeDtypeStruct((tp, kw), BF16),
                   jax.ShapeDtypeStruct((batch * kw, lp), BF16),
                   jax.ShapeDtypeStruct((batch * ATT_HEADS, tiles_per_seq, ATT_V_ROWS, tm), BF16)],
        scratch_shapes=[pltpu.VMEM((SUBLANE, LANE), F32)],
        compiler_params=_cparams("arbitrary"),
        name="attn_prep",
    )(proj, proj, proj, small, bf, tri, _key_bias_placement(), _head_grouping(kw, hw),
      _head_grouping(aw, ATT_HEAD_DIM))


ATT_HEADS_PER_STEP = 8
ATT_MIN_ROW_SUM = 2.0 ** -80


def _attn_kernel(qt_ref, k_ref, vt_ref, o_ref, m_ref, acc_ref, *, tq, tk, pad, heads):
    i = pl.program_id(2)
    r = tq // tk
    hd = ATT_HEAD_DIM
    hw = 2 * hd
    vr = ATT_V_ROWS

    def scores(j, hh, masked):
        k = k_ref[pl.ds(pl.multiple_of(j * tk, tk), tk), hh * hw:(hh + 1) * hw]
        s = _dot(k, qt_ref[hh * hw:(hh + 1) * hw, :])
        if masked:
            kpos = j * tk + lax.broadcasted_iota(I32, (tk, tq), 0)
            qpos = i * tq + lax.broadcasted_iota(I32, (tk, tq), 1)
            s = jnp.where((kpos <= qpos) & (kpos >= pad), s, NEG_INF)
        return s

    def fast_step(j, masked):
        s_next = scores(j, 0, masked)
        for hh in range(heads):
            s = s_next
            if hh + 1 < heads:
                s_next = scores(j, hh + 1, masked)
            p = jnp.exp2(s).astype(BF16)
            acc_ref[hh * vr:(hh + 1) * vr, :] += _dot(vt_ref[hh, j], p)

    def exact_step(j, masked):
        for hh in range(heads):
            s = scores(j, hh, masked)
            m_old = m_ref[hh:hh + 1, :]
            m_new = jnp.maximum(m_old, jnp.max(s, axis=0, keepdims=True))
            p = jnp.exp2(s - m_new).astype(BF16)
            acc_ref[hh * vr:(hh + 1) * vr, :] = (jnp.exp2(m_old - m_new) * acc_ref[hh * vr:(hh + 1) * vr, :]
                                                 + _dot(vt_ref[hh, j], p))
            m_ref[hh:hh + 1, :] = m_new

    def sweep(step):
        def masked_body(j, c):
            step(j, True)
            return c

        def plain_body(j, c):
            step(j, False)
            return c

        acc_ref[...] = jnp.zeros_like(acc_ref)
        step(0, True)
        lax.fori_loop(1, r * i, plain_body, 0)
        lax.fori_loop(jnp.maximum(r * i, 1), r * i + r, masked_body, 0)

    sweep(fast_step)
    real = i * tq + lax.broadcasted_iota(I32, (1, tq), 1) >= pad
    smallest = jnp.min(jnp.concatenate(
        [jnp.where(real, acc_ref[hh * vr + hd:hh * vr + hd + 1, :], 1.0) for hh in range(heads)], axis=0))

    @pl.when(smallest < ATT_MIN_ROW_SUM)
    def _():
        m_ref[...] = jnp.full_like(m_ref, NEG_INF)
        sweep(exact_step)

    for hh in range(heads):
        l = acc_ref[hh * vr + hd:hh * vr + hd + 1, :]
        l = jnp.where(l > 0.0, l, 1.0)
        o_ref[hh * hd:(hh + 1) * hd, :] = (acc_ref[hh * vr:hh * vr + hd, :] / l).astype(BF16)


def _attention(q_t, ka, v_t, batch, lp, pad, tq, tk):
    hd = ATT_HEAD_DIM
    hw = 2 * hd
    nh = ATT_HEADS_PER_STEP
    groups = ATT_HEADS // nh
    nq = lp // tq
    nkv = lp // tk
    kern = functools.partial(_attn_kernel, tq=tq, tk=tk, pad=pad, heads=nh)
    whole_seq = dict(pipeline_mode=pl.Buffered(1))
    return pl.pallas_call(
        kern,
        grid=(batch, groups, nq),
        in_specs=[pl.BlockSpec((nh * hw, tq), lambda b, g, i: (b * groups + g, i)),
                  pl.BlockSpec((lp, nh * hw), lambda b, g, i: (b, g), **whole_seq),
                  pl.BlockSpec((nh, nkv, ATT_V_ROWS, tk), lambda b, g, i: (b * groups + g, 0, 0, 0), **whole_seq)],
        out_specs=pl.BlockSpec((nh * hd, tq), lambda b, g, i: (b * groups + g, i)),
        out_shape=jax.ShapeDtypeStruct((batch * ATT_HEADS * hd, lp), BF16),
        scratch_shapes=[pltpu.VMEM((nh, tq), F32), pltpu.VMEM((nh * ATT_V_ROWS, tq), F32)],
        compiler_params=_cparams("parallel", "parallel", "arbitrary"),
        name="fox_attention",
    )(q_t, ka, v_t)


SSD_MAX_CHUNKS_PER_STEP = 6


def _ssd_kernel(x_ref, b_ref, c_ref, z_ref, dt_ref, cw_ref, cb_ref, dtb_ref, alog_ref, dexp_ref, nw_ref,
                tri_ref, e_ref, o_ref, ubuf_ref, state_ref, ybuf_ref, *, pad, per_step):
    @pl.when(pl.program_id(1) == 0)
    def _():
        ubuf_ref[0:SUBLANE, :] = jnp.zeros((SUBLANE, ubuf_ref.shape[1]), F32)
        state_ref[...] = jnp.zeros_like(state_ref)

    for cc in range(per_step):
        rows = pl.ds(cc * SSD_CHUNK, SSD_CHUNK)
        _ssd_chunk(pl.program_id(1) * per_step + cc,
                   x_ref.at[rows], b_ref.at[rows], c_ref.at[rows], z_ref.at[rows], dt_ref.at[rows],
                   cw_ref, cb_ref, dtb_ref, alog_ref, dexp_ref, nw_ref, tri_ref, e_ref, o_ref.at[rows],
                   ubuf_ref, state_ref, ybuf_ref, pad)


def _ssd_chunk(chunk, x_ref, b_ref, c_ref, z_ref, dt_ref, cw_ref, cb_ref, dtb_ref, alog_ref, dexp_ref, nw_ref,
               tri_ref, e_ref, o_ref, ubuf_ref, state_ref, ybuf_ref, pad):
    blk = SSD_CHUNK
    inner = SSD_HEADS * SSD_HEAD_DIM
    gs = SSD_GROUPS * SSD_STATE
    gw = SSD_HPG * SSD_HEAD_DIM

    ubuf_ref[SUBLANE:SUBLANE + blk, 0:inner] = x_ref[...].astype(F32)
    ubuf_ref[SUBLANE:SUBLANE + blk, inner:inner + gs] = b_ref[...].astype(F32)
    ubuf_ref[SUBLANE:SUBLANE + blk, inner + gs:inner + 2 * gs] = c_ref[...].astype(F32)
    conv = cb_ref[...]
    for k in range(SSD_CONV):
        conv = conv + cw_ref[k:k + 1, :] * ubuf_ref[pl.ds(SUBLANE - (SSD_CONV - 1) + k, blk), :]
    ubuf_ref[0:SUBLANE, :] = ubuf_ref[blk:blk + SUBLANE, :]
    xc = _silu(conv)
    xs = xc[:, 0:inner]
    bm = xc[:, inner:inner + gs]
    cm = xc[:, inner + gs:inner + 2 * gs]

    row = lax.broadcasted_iota(I32, (blk, LANE), 0)
    lane = lax.broadcasted_iota(I32, (blk, LANE), 1)
    dt = _softplus(dt_ref[...] + dtb_ref[...])
    dt = jnp.where((lane < SSD_HEADS) & (chunk * blk + row >= pad), dt, 0.0)
    a = -jnp.exp(alog_ref[...])
    tri = tri_ref[...]
    hi, mid, lo = _split3(dt * a)
    a_cum = _dot(tri, hi) + _dot(tri, mid) + _dot(tri, lo)
    a_last = a_cum[blk - 1:blk, :]
    ea = jnp.exp(a_cum)
    wgt = jnp.exp(a_last - a_cum) * dt

    def pieces(v, at):
        v_hi = v.astype(BF16).astype(F32)
        v_lo = v - v_hi
        return (pltpu.roll(v_hi, at, 1) if at else v_hi), pltpu.roll(v_lo, at + SSD_HEADS, 1)

    dt_hi, dt_lo = pieces(dt, 0)
    ea_hi, ea_lo = pieces(ea, 2 * SSD_HEADS)
    wg_hi, wg_lo = pieces(wgt, 4 * SSD_HEADS)
    h16 = SSD_HEADS
    packed = jnp.where(lane < h16, dt_hi,
             jnp.where(lane < 2 * h16, dt_lo,
             jnp.where(lane < 3 * h16, ea_hi,
             jnp.where(lane < 4 * h16, ea_lo,
             jnp.where(lane < 5 * h16, wg_hi,
             jnp.where(lane < 6 * h16, wg_lo, 0.0)))))).astype(BF16)
    expanded = _dot(packed, e_ref[...])
    dt_exp = expanded[:, 0:inner]
    ea_exp = expanded[:, inner:2 * inner]
    wg_exp = expanded[:, 2 * inner:3 * inner]
    xdt = (xs * dt_exp).astype(BF16)
    xw = (xs * wg_exp).astype(BF16)

    a_cum_t = a_cum.T
    causal = row >= lane
    for g in range(SSD_GROUPS):
        cg = cm[:, g * SSD_STATE:(g + 1) * SSD_STATE].astype(BF16)
        bg_f32 = bm[:, g * SSD_STATE:(g + 1) * SSD_STATE]
        cbg = _dot_nt(cg, bg_f32.astype(BF16))
        mats = []
        for hh in range(SSD_HPG):
            h = g * SSD_HPG + hh
            diff = a_cum[:, h:h + 1] - a_cum_t[h:h + 1, :]
            decay = jnp.exp(jnp.where(causal, diff, NEG_INF))
            mats.append((cbg * decay).astype(BF16))
        for j in range(SSD_HPG // 2):
            col = g * gw + j * LANE
            xp = xdt[:, col:col + LANE]
            y0 = _dot(mats[2 * j], xp)
            y1 = _dot(mats[2 * j + 1], xp)
            ybuf_ref[:, col:col + LANE] = jnp.where(lane < SSD_HEAD_DIM, y0, y1)
        st = state_ref[g]
        y_off = _dot(cg, st.astype(BF16))
        ybuf_ref[:, g * gw:(g + 1) * gw] += y_off * ea_exp[:, g * gw:(g + 1) * gw]
        chunk_decay = ea_exp[blk - 1:blk, g * gw:(g + 1) * gw]
        state_ref[g] = st * chunk_decay + _dot(bg_f32.T.astype(BF16), xw[:, g * gw:(g + 1) * gw])

    y = ybuf_ref[...] + dexp_ref[...] * xs
    u = y * _silu(z_ref[...].astype(F32))
    for g in range(SSD_GROUPS):
        ug = u[:, g * gw:(g + 1) * gw]
        ug = ug * lax.rsqrt(jnp.mean(ug * ug, axis=-1, keepdims=True) + RMS_EPS)
        o_ref[:, g * gw:(g + 1) * gw] = (ug * nw_ref[:, g * gw:(g + 1) * gw]).astype(BF16)


def _ssd_expand_matrix():
    inner = SSD_HEADS * SSD_HEAD_DIM
    rows = jnp.arange(LANE)
    cols = jnp.arange(3 * inner)
    seg = cols // inner
    head = (cols % inner) // SSD_HEAD_DIM
    hit = (rows[:, None] // (2 * SSD_HEADS) == seg[None, :]) & (rows[:, None] % SSD_HEADS == head[None, :]) \
          & (rows[:, None] < 6 * SSD_HEADS)
    return hit.astype(BF16)


def _ssd(proj, small, cols, conv_w, conv_b, dt_bias, a_log, d, norm_w, batch, lp, pad):
    tp = proj.shape[0]
    blk = SSD_CHUNK
    inner = SSD_HEADS * SSD_HEAD_DIM
    gs = SSD_GROUPS * SSD_STATE
    nc = lp // blk
    cdim = inner + 2 * gs
    cw = jnp.zeros((SUBLANE, cdim), F32).at[:SSD_CONV].set(conv_w)
    pad_lane = lambda v: jnp.zeros((1, LANE), F32).at[0, :v.shape[0]].set(v)
    dexp = jnp.repeat(d, SSD_HEAD_DIM).reshape(1, inner)
    tri = jnp.tril(jnp.ones((blk, blk), F32)).astype(BF16)
    per_step = _divisor_tile(nc, SSD_MAX_CHUNKS_PER_STEP, 1)
    steps = nc // per_step
    rows = per_step * blk
    rowblk = lambda b, c: b * steps + c
    const = lambda shape: pl.BlockSpec(shape, lambda b, c: (0,) * len(shape))
    return pl.pallas_call(
        functools.partial(_ssd_kernel, pad=pad, per_step=per_step),
        grid=(batch, steps),
        in_specs=[pl.BlockSpec((rows, inner), lambda b, c: (rowblk(b, c), cols["x"] // inner)),
                  pl.BlockSpec((rows, gs), lambda b, c: (rowblk(b, c), cols["B"] // gs)),
                  pl.BlockSpec((rows, gs), lambda b, c: (rowblk(b, c), cols["C"] // gs)),
                  pl.BlockSpec((rows, inner), lambda b, c: (rowblk(b, c), cols["z"] // inner)),
                  pl.BlockSpec((rows, LANE), lambda b, c: (rowblk(b, c), 1)),
                  const((SUBLANE, cdim)), const((1, cdim)), const((1, LANE)), const((1, LANE)),
                  const((1, inner)), const((1, inner)), const((blk, blk)), const((LANE, 3 * inner))],
        out_specs=pl.BlockSpec((rows, inner), lambda b, c: (rowblk(b, c), 0)),
        out_shape=jax.ShapeDtypeStruct((tp, inner), BF16),
        scratch_shapes=[pltpu.VMEM((blk + SUBLANE, cdim), F32),
                        pltpu.VMEM((SSD_GROUPS, SSD_STATE, SSD_HPG * SSD_HEAD_DIM), F32),
                        pltpu.VMEM((blk, inner), F32)],
        compiler_params=_cparams("parallel", "arbitrary"),
        name="ssd",
    )(proj, proj, proj, proj, small, cw, conv_b.reshape(1, cdim), pad_lane(dt_bias), pad_lane(a_log),
      dexp, norm_w.reshape(1, inner), tri, _ssd_expand_matrix())


def _merge_kernel(yat_ref, ys_ref, sb_ref, sc_ref, sh_ref, sch_ref, shh_ref, ga_ref, gs_ref, gc_ref, h_ref,
                  wpa_ref, wps_ref, wpc_ref, wo_ref, scw_ref, g_ref, b_ref, *rest,
                  tm, tiles_per_seq, pad, alpha, with_router):
    if with_router:
        rwh_ref, rwl_ref, rb_ref, o_ref, ri_ref, rw_ref, vbuf_ref = rest
    else:
        o_ref, vbuf_ref = rest
    vbuf_ref[0:SUBLANE, :] = sch_ref[...].astype(F32) * shh_ref[...].astype(F32)
    vbuf_ref[SUBLANE:SUBLANE + tm, :] = sc_ref[...].astype(F32) * sh_ref[...].astype(F32)
    conv = jnp.zeros((tm, vbuf_ref.shape[1]), F32)
    for k in range(SC_CONV):
        conv = conv + scw_ref[k:k + 1, :] * vbuf_ref[pl.ds(SUBLANE - (SC_CONV - 1) + k, tm), :]
    y_conv = (sb_ref[...].astype(F32) * conv).astype(BF16)
    y_att = yat_ref[...].astype(F32).T.astype(BF16)

    merged = jax.nn.sigmoid(ga_ref[...].astype(F32)) * _dot(y_att, wpa_ref[...])
    merged += jax.nn.sigmoid(gs_ref[...].astype(F32)) * _dot(ys_ref[...], wps_ref[...])
    merged += jax.nn.sigmoid(gc_ref[...].astype(F32)) * _dot(y_conv, wpc_ref[...])
    mix = _dot(merged.astype(BF16), wo_ref[...])
    hn = _ln(alpha * h_ref[...] + mix, g_ref[...], b_ref[...])
    hn = _zero_pad_rows(hn, pl.program_id(0), tm, tiles_per_seq, pad)
    o_ref[...] = hn

    if with_router:
        lane = lax.broadcasted_iota(I32, (tm, LANE), 1)
        lane_f = lane.astype(F32)
        logits = _dot_x3(hn, rwh_ref[...], rwl_ref[...]) + rb_ref[...]
        lg = jnp.where(lane < N_EXPERTS, logits, -jnp.inf)
        v1 = jnp.max(lg, axis=1, keepdims=True)
        i1 = jnp.min(jnp.where(lg == v1, lane_f, float(LANE)), axis=1, keepdims=True).astype(I32)
        lg2 = jnp.where(lane == i1, -jnp.inf, lg)
        v2 = jnp.max(lg2, axis=1, keepdims=True)
        i2 = jnp.min(jnp.where(lg2 == v2, lane_f, float(LANE)), axis=1, keepdims=True).astype(I32)
        e = jnp.exp(v2 - v1)
        w1 = 1.0 / (1.0 + e)
        w2 = e / (1.0 + e)
        ri_ref[...] = jnp.where(lane == 0, i1, jnp.where(lane == 1, i2, 0))
        rw_ref[...] = jnp.where(lane == 0, w1, jnp.where(lane == 1, w2, 0.0))


def _merge(y_att_t, y_ssd, proj, cols, h, wpa, wps, wpc, wo, sc_w, g, b, lp, pad, alpha, router=None):
    tp, d = h.shape
    scw = wpc.shape[0]
    aw = wpa.shape[0]
    tm = _divisor_tile(lp, 768, LANE)
    tiles_per_seq = lp // tm
    nt = tp // tm
    halo = lambda i: jnp.maximum(i * (tm // SUBLANE) - 1, 0)
    const = lambda shape: pl.BlockSpec(shape, lambda i: (0,) * len(shape))
    in_specs = [pl.BlockSpec((aw, tm), lambda i: (i // tiles_per_seq, i % tiles_per_seq)),
                pl.BlockSpec((tm, y_ssd.shape[1]), lambda i: (i, 0)),
                pl.BlockSpec((tm, scw), lambda i: (i, cols["sc_b"] // scw)),
                pl.BlockSpec((tm, scw), lambda i: (i, cols["sc_c"] // scw)),
                pl.BlockSpec((tm, scw), lambda i: (i, cols["sc_h"] // scw)),
                pl.BlockSpec((SUBLANE, scw), lambda i: (halo(i), cols["sc_c"] // scw)),
                pl.BlockSpec((SUBLANE, scw), lambda i: (halo(i), cols["sc_h"] // scw)),
                pl.BlockSpec((tm, d), lambda i: (i, cols["g_att"] // d)),
                pl.BlockSpec((tm, d), lambda i: (i, cols["g_ssd"] // d)),
                pl.BlockSpec((tm, d), lambda i: (i, cols["g_conv"] // d)),
                pl.BlockSpec((tm, d), lambda i: (i, 0)),
                const(wpa.shape), const(wps.shape), const(wpc.shape), const(wo.shape),
                const((SUBLANE, scw)), const((1, d)), const((1, d))]
    args = [y_att_t, y_ssd, proj, proj, proj, proj, proj, proj, proj, proj, h, wpa, wps, wpc, wo,
            jnp.zeros((SUBLANE, scw), F32).at[:SC_CONV].set(sc_w), g.reshape(1, d), b.reshape(1, d)]
    out_specs = [pl.BlockSpec((tm, d), lambda i: (i, 0))]
    out_shape = [jax.ShapeDtypeStruct((tp, d), F32)]
    if router is not None:
        rw_hi, rw_lo, rb = router
        in_specs += [const(rw_hi.shape), const(rw_lo.shape), const((1, LANE))]
        args += [rw_hi, rw_lo, rb]
        out_specs += [pl.BlockSpec((tm, LANE), lambda i: (i, 0)), pl.BlockSpec((tm, LANE), lambda i: (i, 0))]
        out_shape += [jax.ShapeDtypeStruct((tp, LANE), I32), jax.ShapeDtypeStruct((tp, LANE), F32)]
    kern = functools.partial(_merge_kernel, tm=tm, tiles_per_seq=tiles_per_seq, pad=pad, alpha=alpha,
                             with_router=router is not None)
    return pl.pallas_call(
        kern, grid=(nt,), in_specs=in_specs, out_specs=out_specs, out_shape=out_shape,
        scratch_shapes=[pltpu.VMEM((tm + SUBLANE, scw), F32)],
        compiler_params=_cparams("parallel"),
        name="merge_out",
    )(*args)


def _ffn_kernel(h_ref, wg_ref, wu_ref, wd_ref, g_ref, b_ref, o_ref, *, tm, tf, tiles_per_seq, pad, alpha):
    h = h_ref[...]
    xb = h.astype(BF16)
    acc = jnp.zeros_like(h)
    for c in range(wd_ref.shape[0] // tf):
        cols = slice(c * tf, (c + 1) * tf)
        act = (_silu(_dot(xb, wg_ref[:, cols])) * _dot(xb, wu_ref[:, cols])).astype(BF16)
        acc = acc + _dot(act, wd_ref[cols, :])
    hn = _ln(alpha * h + acc, g_ref[...], b_ref[...])
    o_ref[...] = _zero_pad_rows(hn, pl.program_id(0), tm, tiles_per_seq, pad)


def _ffn(h, w_gu, w_down, g, b, lp, pad, alpha):
    tp, d = h.shape
    ff = w_down.shape[0]
    tm = _divisor_tile(lp, 768, LANE)
    tf = _divisor_tile(ff, 2 * MXU_DIM, MXU_DIM)
    kern = functools.partial(_ffn_kernel, tm=tm, tf=tf, tiles_per_seq=lp // tm, pad=pad, alpha=alpha)
    resident = dict(pipeline_mode=pl.Buffered(1))
    return pl.pallas_call(
        kern,
        grid=(tp // tm,),
        in_specs=[pl.BlockSpec((tm, d), lambda i: (i, 0)),
                  pl.BlockSpec((d, ff), lambda i: (0, 0), **resident),
                  pl.BlockSpec((d, ff), lambda i: (0, 1), **resident),
                  pl.BlockSpec((ff, d), lambda i: (0, 0), **resident),
                  pl.BlockSpec((1, d), lambda i: (0, 0)),
                  pl.BlockSpec((1, d), lambda i: (0, 0))],
        out_specs=pl.BlockSpec((tm, d), lambda i: (i, 0)),
        out_shape=jax.ShapeDtypeStruct((tp, d), F32),
        compiler_params=_cparams("parallel"),
        name="dense_ffn",
    )(h, w_gu, w_gu, w_down, g.reshape(1, d), b.reshape(1, d))


def _moe_scatter_kernel(slot_ref, h_ref, init_ref, o_ref, buf_ref, sem, *, tm):
    del init_ref
    for s in range(TOKEN_TILE_ROWS):
        buf_ref[pl.ds(s, tm, stride=TOKEN_TILE_ROWS), :] = h_ref[:, s * LANE:(s + 1) * LANE]

    def issue(r, c):
        src = buf_ref.at[pl.ds(pl.multiple_of(r * TOKEN_TILE_ROWS, TOKEN_TILE_ROWS), TOKEN_TILE_ROWS)]
        for k in range(2):
            slot = slot_ref[0, 0, 2 * r + k]
            dst = o_ref.at[pl.ds(pl.multiple_of(slot * TOKEN_TILE_ROWS, TOKEN_TILE_ROWS), TOKEN_TILE_ROWS)]
            pltpu.make_async_copy(src, dst, sem).start(priority=k)
        return c

    lax.fori_loop(0, tm, issue, 0)
    for _ in range(2):
        pltpu.make_async_copy(buf_ref, o_ref.at[pl.ds(0, tm * TOKEN_TILE_ROWS)], sem).wait()


def _moe_scatter(h, slots, n_rows, lp):
    tp, d = h.shape
    assert d == TOKEN_TILE_ROWS * LANE
    tm = _divisor_tile(lp, 768, LANE)
    nt = tp // tm
    init = jnp.zeros((n_rows * TOKEN_TILE_ROWS, LANE), F32)
    return pl.pallas_call(
        functools.partial(_moe_scatter_kernel, tm=tm),
        grid=(nt,),
        in_specs=[pl.BlockSpec((1, 1, 2 * tm), lambda i: (i, 0, 0), memory_space=pltpu.SMEM),
                  pl.BlockSpec((tm, d), lambda i: (i, 0)),
                  pl.BlockSpec(memory_space=pl.ANY)],
        out_specs=pl.BlockSpec(memory_space=pl.ANY),
        out_shape=jax.ShapeDtypeStruct(init.shape, F32),
        scratch_shapes=[pltpu.VMEM((tm * TOKEN_TILE_ROWS, LANE), F32), pltpu.SemaphoreType.DMA(())],
        input_output_aliases={2: 0},
        compiler_params=_cparams("arbitrary"),
        name="moe_scatter",
    )(slots.reshape(nt, 1, 2 * tm), h, init)


def _moe_kernel(te_ref, tv_ref, x_ref, wg_ref, wu_ref, wd_ref, o_ref, xb_ref, acc_ref, *, tm):
    n = pl.program_id(0)
    f = pl.program_id(1)

    @pl.when(f == 0)
    def _():
        for s in range(TOKEN_TILE_ROWS):
            xb_ref[:, s * LANE:(s + 1) * LANE] = x_ref[pl.ds(s, tm, stride=TOKEN_TILE_ROWS), :].astype(BF16)
        acc_ref[...] = jnp.zeros_like(acc_ref)

    @pl.when(tv_ref[n] == 1)
    def _():
        xb = xb_ref[...]
        act = (_silu(_dot(xb, wg_ref[0])) * _dot(xb, wu_ref[0])).astype(BF16)
        acc_ref[...] += _dot(act, wd_ref[0])

    @pl.when(f == pl.num_programs(1) - 1)
    def _():
        for s in range(TOKEN_TILE_ROWS):
            o_ref[pl.ds(s, tm, stride=TOKEN_TILE_ROWS), :] = acc_ref[:, s * LANE:(s + 1) * LANE]


def _moe_experts(x_sorted, tile_expert, tile_valid, w_gu, w_down, tm, n_tiles):
    d = w_down.shape[2]
    ff = w_down.shape[1]
    tf = _divisor_tile(ff, 7 * MXU_DIM, MXU_DIM)
    nf = ff // tf
    fblk = lambda f, n, tv: f * tv[n] + (nf - 1) * (1 - tv[n])
    rows = tm * TOKEN_TILE_ROWS
    grid_spec = pltpu.PrefetchScalarGridSpec(
        num_scalar_prefetch=2,
        grid=(n_tiles, nf),
        in_specs=[pl.BlockSpec((rows, LANE), lambda n, f, te, tv: (n, 0)),
                  pl.BlockSpec((1, d, tf), lambda n, f, te, tv: (te[n], 0, fblk(f, n, tv))),
                  pl.BlockSpec((1, d, tf), lambda n, f, te, tv: (te[n], 0, nf + fblk(f, n, tv))),
                  pl.BlockSpec((1, tf, d), lambda n, f, te, tv: (te[n], fblk(f, n, tv), 0))],
        out_specs=pl.BlockSpec((rows, LANE), lambda n, f, te, tv: (n, 0)),
        scratch_shapes=[pltpu.VMEM((tm, d), BF16), pltpu.VMEM((tm, d), F32)])
    return pl.pallas_call(
        functools.partial(_moe_kernel, tm=tm),
        grid_spec=grid_spec,
        out_shape=jax.ShapeDtypeStruct((n_tiles * rows, LANE), F32),
        compiler_params=_cparams("parallel", "arbitrary"),
        name="moe_experts",
    )(tile_expert, tile_valid, x_sorted, w_gu, w_gu, w_down)


def _moe_combine_kernel(slot_ref, y_ref, rw_ref, h_ref, g_ref, b_ref, o_ref, gbuf_ref, sem,
                        *, tm, tiles_per_seq, pad, alpha):
    def issue(r, c):
        for k in range(2):
            slot = slot_ref[0, 0, 2 * r + k]
            src = y_ref.at[pl.ds(pl.multiple_of(slot * TOKEN_TILE_ROWS, TOKEN_TILE_ROWS), TOKEN_TILE_ROWS)]
            dst = gbuf_ref.at[pl.ds(pl.multiple_of((k * tm + r) * TOKEN_TILE_ROWS, TOKEN_TILE_ROWS),
                                    TOKEN_TILE_ROWS)]
            pltpu.make_async_copy(src, dst, sem).start(priority=k)
        return c

    lax.fori_loop(0, tm, issue, 0, unroll=4)
    pltpu.make_async_copy(y_ref.at[pl.ds(0, 2 * tm * TOKEN_TILE_ROWS)], gbuf_ref, sem).wait()

    rw = rw_ref[...]
    w0, w1 = rw[:, 0:1], rw[:, 1:2]
    ff = jnp.concatenate(
        [w0 * gbuf_ref[pl.ds(s, tm, stride=TOKEN_TILE_ROWS), :]
         + w1 * gbuf_ref[pl.ds(tm * TOKEN_TILE_ROWS + s, tm, stride=TOKEN_TILE_ROWS), :]
         for s in range(TOKEN_TILE_ROWS)], axis=1)
    hn = _ln(alpha * h_ref[...] + ff, g_ref[...], b_ref[...])
    o_ref[...] = _zero_pad_rows(hn, pl.program_id(0), tm, tiles_per_seq, pad)


def _moe_combine(y_sorted, slots, route_w, h, g, b, lp, pad, alpha):
    tp, d = h.shape
    tm = _divisor_tile(lp, 768, LANE)
    nt = tp // tm
    kern = functools.partial(_moe_combine_kernel, tm=tm, tiles_per_seq=lp // tm, pad=pad, alpha=alpha)
    return pl.pallas_call(
        kern,
        grid=(nt,),
        in_specs=[pl.BlockSpec((1, 1, 2 * tm), lambda i: (i, 0, 0), memory_space=pltpu.SMEM),
                  pl.BlockSpec(memory_space=pl.ANY),
                  pl.BlockSpec((tm, LANE), lambda i: (i, 0)),
                  pl.BlockSpec((tm, d), lambda i: (i, 0)),
                  pl.BlockSpec((1, d), lambda i: (0, 0)),
                  pl.BlockSpec((1, d), lambda i: (0, 0))],
        out_specs=pl.BlockSpec((tm, d), lambda i: (i, 0)),
        out_shape=jax.ShapeDtypeStruct((tp, d), F32),
        scratch_shapes=[pltpu.VMEM((2 * tm * TOKEN_TILE_ROWS, LANE), F32), pltpu.SemaphoreType.DMA(())],
        compiler_params=_cparams("arbitrary"),
        name="moe_combine",
    )(slots.reshape(nt, 1, 2 * tm), y_sorted, route_w, h, g.reshape(1, d), b.reshape(1, d))


def _moe(h, route_idx, route_w, w_gu, w_down, g, b, batch, lp, pad, alpha):
    tp, d = h.shape
    n_real = batch * (lp - pad)
    tm = 640
    experts = route_idx[:, :2].reshape(tp * 2)
    seq_row = jnp.arange(tp, dtype=I32) % lp
    real = jnp.repeat(seq_row >= pad, 2)
    onehot = ((experts[:, None] == jnp.arange(N_EXPERTS, dtype=I32)[None, :]) & real[:, None]).astype(I32)
    csum = jnp.cumsum(onehot, axis=0)
    counts = csum[-1]
    rank = jnp.sum(onehot * csum, axis=1) - 1
    tiles = (counts + tm - 1) // tm
    tile_end = jnp.cumsum(tiles)
    slot = ((tile_end - tiles) * tm)[experts] + rank
    n_tiles = (2 * n_real) // tm + N_EXPERTS
    tile_ids = jnp.arange(n_tiles, dtype=I32)
    tile_expert = jnp.minimum(jnp.sum((tile_ids[:, None] >= tile_end[None, :]).astype(I32), axis=1),
                              N_EXPERTS - 1).astype(I32)
    tile_valid = (tile_ids < tile_end[-1]).astype(I32)
    n_pad = batch * pad * 2
    spare_tiles = -(-n_pad // tm)
    pad_id = jnp.repeat((jnp.arange(tp, dtype=I32) // lp) * pad + seq_row, 2) * 2 + jnp.tile(jnp.arange(2, dtype=I32), tp)
    slot_scatter = jnp.where(real, slot, n_tiles * tm + pad_id).astype(I32)
    slot_gather = jnp.where(real, slot, 0).astype(I32)

    x_sorted = _moe_scatter(h, slot_scatter, (n_tiles + spare_tiles) * tm, lp)
    y_sorted = _moe_experts(x_sorted, tile_expert, tile_valid, w_gu, w_down, tm, n_tiles)
    return _moe_combine(y_sorted, slot_gather, route_w, h, g, b, lp, pad, alpha)


def _prepare_in_proj(w_in):
    d = w_in.shape[0]
    aw = ATT_HEADS * ATT_HEAD_DIM
    inner = SSD_HEADS * SSD_HEAD_DIM
    gs = SSD_GROUPS * SSD_STATE
    sizes = [aw, aw, aw, ATT_HEADS, inner, inner + 2 * gs, SSD_HEADS]
    sc3 = w_in.shape[1] - sum(sizes) - 3 * d
    sizes += [sc3, 3 * d]
    offs = [0]
    for s in sizes:
        offs.append(offs[-1] + s)
    part = lambda k: w_in[:, offs[k]:offs[k + 1]]

    def pad_heads(w):
        w = w.reshape(d, ATT_HEADS, ATT_HEAD_DIM)
        return jnp.pad(w, ((0, 0), (0, 0), (0, ATT_HEAD_DIM))).reshape(d, 2 * aw)

    pieces = [("k", pad_heads(part(1))), ("q", part(0) * (ATT_HEAD_DIM ** -0.5 * LOG2E)), ("v", part(2)),
              ("z", part(4)), ("x", part(5)[:, :inner]), ("B", part(5)[:, inner:inner + gs]),
              ("C", part(5)[:, inner + gs:]), ("sc_b", part(7)[:, :sc3 // 3]),
              ("sc_c", part(7)[:, sc3 // 3:2 * sc3 // 3]), ("sc_h", part(7)[:, 2 * sc3 // 3:]),
              ("g_att", part(8)[:, :d]), ("g_ssd", part(8)[:, d:2 * d]), ("g_conv", part(8)[:, 2 * d:])]
    cols, off = {}, 0
    for name, w in pieces:
        cols[name] = off
        off += w.shape[1]
    w_main = jnp.concatenate([w for _, w in pieces], axis=1).astype(BF16)
    w_small = jnp.zeros((d, 2 * LANE), F32).at[:, :ATT_HEADS].set(part(3)).at[:, LANE:LANE + SSD_HEADS].set(part(6))
    ws_hi = w_small.astype(BF16)
    ws_lo = (w_small - ws_hi.astype(F32)).astype(BF16)
    return w_main, ws_hi, ws_lo, cols


def kernel(x, meta_tokens, ln_in_g, ln_in_b, w_in, b_forget, ssd_conv_w, ssd_conv_b, ssd_dt_bias, ssd_a_log,
           ssd_d, ssd_norm_w, sc_conv_w, w_proj_attn, w_proj_ssd, w_proj_conv, w_out, ln_mix_g, ln_mix_b,
           dense_w_gu, dense_w_down, router_w, router_b, moe_w_gu, moe_w_down, ln_ffn_g, ln_ffn_b):
    batch, seq, d = x.shape
    depth = w_in.shape[0]
    alpha = (2 * depth) ** 0.25
    pad = (-(N_META + seq)) % MXU_DIM
    lp = pad + N_META + seq
    tp = batch * lp
    tk = MXU_DIM
    tq = _divisor_tile(lp, 3 * MXU_DIM, MXU_DIM)

    h = _embed_ln(x, meta_tokens, ln_in_g, ln_in_b, pad).reshape(tp, d)

    for layer in range(depth):
        w_main, ws_hi, ws_lo, cols = _prepare_in_proj(w_in[layer])
        proj, small = _in_proj(h, w_main, ws_hi, ws_lo)
        ka, q_t, v_t = _attn_prep(proj, small, cols, b_forget[layer], batch, lp, pad, tk)
        y_att_t = _attention(q_t, ka, v_t, batch, lp, pad, tq, tk)
        y_ssd = _ssd(proj, small, cols, ssd_conv_w[layer], ssd_conv_b[layer], ssd_dt_bias[layer],
                     ssd_a_log[layer], ssd_d[layer], ssd_norm_w[layer], batch, lp, pad)
        j = layer // 2
        router = None
        if layer % 2 == 1:
            rw = jnp.zeros((d, LANE), F32).at[:, :N_EXPERTS].set(router_w[j])
            rw_hi = rw.astype(BF16)
            rw_lo = (rw - rw_hi.astype(F32)).astype(BF16)
            router = (rw_hi, rw_lo, jnp.zeros((1, LANE), F32).at[0, :N_EXPERTS].set(router_b[j]))
        outs = _merge(y_att_t, y_ssd, proj, cols, h, w_proj_attn[layer].astype(BF16),
                      w_proj_ssd[layer].astype(BF16), w_proj_conv[layer].astype(BF16),
                      w_out[layer].astype(BF16), sc_conv_w[layer], ln_mix_g[layer], ln_mix_b[layer],
                      lp, pad, alpha, router)
        if layer % 2 == 0:
            h = _ffn(outs[0], dense_w_gu[j].astype(BF16), dense_w_down[j].astype(BF16),
                     ln_ffn_g[layer], ln_ffn_b[layer], lp, pad, alpha)
        else:
            h_mid, route_idx, route_w = outs
            h = _moe(h_mid, route_idx, route_w, moe_w_gu[j].astype(BF16), moe_w_down[j].astype(BF16),
                     ln_ffn_g[layer], ln_ffn_b[layer], batch, lp, pad, alpha)
    return h.reshape(batch, lp, d)[:, pad + N_META:]
```
